```python
import math
import jax, jax.numpy as jnp
from jax import lax
import numpy as np

D_MODEL = 2048
BATCH = 2
SEQ = 4096
DEPTH = 1

HEAD_DIM = 128
N_HEADS = D_MODEL // HEAD_DIM
N_DIL_HEADS = N_HEADS // 2
N_FOX_HEADS = N_HEADS - N_DIL_HEADS
DIL_WIDTH = N_DIL_HEADS * HEAD_DIM
FOX_WIDTH = N_FOX_HEADS * HEAD_DIM
DIL_PATTERNS = ((128, 1), (512, 4), (2048, 16))
Q_BLOCK = 128
IN_SPLITS = (DIL_WIDTH, 2 * DIL_WIDTH, 3 * DIL_WIDTH,
             3 * DIL_WIDTH + FOX_WIDTH, 3 * DIL_WIDTH + 2 * FOX_WIDTH,
             3 * DIL_WIDTH + 3 * FOX_WIDTH)
IN_WIDTH = 3 * DIL_WIDTH + 3 * FOX_WIDTH + N_FOX_HEADS
N_EXPERTS = 32
TOP_K = 4
D_FF = D_MODEL
SWIGLU_LIMIT = 7.0
SWIGLU_ALPHA = 1.702
EXPERT_BLOCK = 128
LN_EPS = 1e-5
DEEPNORM_ALPHA = (2.0 * DEPTH) ** 0.25
DEEPNORM_BETA = (8.0 * DEPTH) ** -0.25

kernel_name = 'hybrid_dilated_fox_moe_block'


def layer_norm(x, g, b):
    xf = x.astype(jnp.float32)
    mu = jnp.mean(xf, axis=-1, keepdims=True)
    xc = xf - mu
    var = jnp.mean(xc * xc, axis=-1, keepdims=True)
    return (xc * lax.rsqrt(var + LN_EPS) * g.astype(jnp.float32) + b.astype(jnp.float32)).astype(x.dtype)


def alibi_slopes(n):
    return 2.0 ** (-8.0 * jnp.arange(1, n + 1, dtype=jnp.float32) / n)


def dilated_window_partial(q, k, v, slopes, window, dilation):
    B, S, H, Dh = q.shape
    L = S // dilation
    nb = -(-L // Q_BLOCK)
    Lp = nb * Q_BLOCK
    reach = window // dilation

    def to_sub(t):
        t = t.reshape(B, L, dilation, H, Dh).transpose(0, 2, 3, 1, 4)
        t = jnp.pad(t, ((0, 0), (0, 0), (0, 0), (0, Lp - L), (0, 0)))
        return t.reshape(B, dilation, H, nb, Q_BLOCK, Dh)

    def with_prev(t):
        prev = jnp.pad(t, ((0, 0), (0, 0), (0, 0), (1, 0), (0, 0), (0, 0)))[:, :, :, :-1]
        return jnp.concatenate([prev, t], axis=4)

    qs = to_sub(q)
    kk = with_prev(to_sub(k))
    vv = with_prev(to_sub(v))
    scores = jnp.einsum('brhnqe,brhnke->brhnqk', qs, kk).astype(jnp.float32) / math.sqrt(Dh)
    qi = jnp.arange(Q_BLOCK)[:, None]
    kj = jnp.arange(2 * Q_BLOCK)[None, :]
    delta = qi + Q_BLOCK - kj
    kpos = (jnp.arange(nb)[:, None, None] - 1) * Q_BLOCK + kj[None]
    valid = (delta >= 0) & (delta <= reach) & (kpos >= 0)
    bias = -slopes[:, None, None, None] * (delta * dilation).astype(jnp.float32)
    scores = jnp.where(valid, scores + bias, -jnp.inf)
    m = jnp.max(scores, axis=-1, keepdims=True)
    p = jnp.exp(scores - m)
    l = jnp.sum(p, axis=-1)
    o = jnp.einsum('brhnqk,brhnke->brhnqe', p, vv.astype(jnp.float32))
    o = o.reshape(B, dilation, H, Lp, Dh)[:, :, :, :L].transpose(0, 3, 1, 2, 4).reshape(B, S, H, Dh)
    m = m.reshape(B, dilation, H, Lp)[:, :, :, :L].transpose(0, 3, 1, 2).reshape(B, S, H)
    l = l.reshape(B, dilation, H, Lp)[:, :, :, :L].transpose(0, 3, 1, 2).reshape(B, S, H)
    return o, m, l


def dilated_attention(q, k, v, slopes):
    parts = [dilated_window_partial(q, k, v, slopes, w, d) for (w, d) in DIL_PATTERNS]
    m_max = jnp.max(jnp.stack([p[1] for p in parts]), axis=0)
    num = 0.0
    den = 0.0
    for o_i, m_i, l_i in parts:
        wgt = jnp.exp(m_i - m_max)
        num = num + wgt[..., None] * o_i
        den = den + wgt * l_i
    return num / den[..., None]


def forgetting_attention(q, k, v, log_f):
    B, S, H, Dh = q.shape
    cum = jnp.cumsum(log_f, axis=1).transpose(0, 2, 1)
    kt = k.transpose(0, 2, 1, 3)
    vt = v.transpose(0, 2, 1, 3).astype(jnp.float32)
    nq = S // Q_BLOCK
    qb = q.reshape(B, nq, Q_BLOCK, H, Dh).transpose(1, 0, 3, 2, 4)
    cb = cum.reshape(B, H, nq, Q_BLOCK).transpose(2, 0, 1, 3)
    starts = jnp.arange(nq) * Q_BLOCK
    kpos = jnp.arange(S)
    scale = 1.0 / math.sqrt(Dh)

    def block(args):
        qi, ci, s0 = args
        s = jnp.einsum('bhqe,bhke->bhqk', qi, kt).astype(jnp.float32) * scale
        s = s + ci[..., None] - cum[:, :, None, :]
        qpos = s0 + jnp.arange(Q_BLOCK)
        s = jnp.where(kpos[None, :] <= qpos[:, None], s, -jnp.inf)
        p = jax.nn.softmax(s, axis=-1)
        return jnp.einsum('bhqk,bhke->bhqe', p, vt)

    o = lax.map(block, (qb, cb, starts))
    return o.transpose(1, 0, 3, 2, 4).reshape(B, S, H, Dh)


def moe_ffn(h, w_router, b_router, w_gate_up, b_gate_up, w_down, b_down):
    B, S, D = h.shape
    hf = h.reshape(-1, D)
    T = hf.shape[0]
    logits = (hf @ w_router + b_router).astype(jnp.float32)
    top_val, top_idx = lax.top_k(logits, TOP_K)
    gates = jax.nn.softmax(top_val, axis=-1)
    TK = T * TOP_K
    e_flat = top_idx.reshape(-1)
    g_flat = gates.reshape(-1)
    tok_flat = jnp.arange(TK, dtype=jnp.int32) // TOP_K
    counts = jnp.bincount(e_flat, length=N_EXPERTS)
    padded = (counts + EXPERT_BLOCK - 1) // EXPERT_BLOCK * EXPERT_BLOCK
    pad_end = jnp.cumsum(padded)
    pad_start = pad_end - padded
    grp_start = jnp.cumsum(counts) - counts
    order = jnp.argsort(e_flat)
    e_sorted = e_flat[order]
    dest = pad_start[e_sorted] + jnp.arange(TK) - grp_start[e_sorted]
    nblk = (TK + N_EXPERTS * (EXPERT_BLOCK - 1) + EXPERT_BLOCK - 1) // EXPERT_BLOCK
    row_tok = jnp.zeros((nblk * EXPERT_BLOCK,), jnp.int32).at[dest].set(tok_flat[order])
    row_gate = jnp.zeros((nblk * EXPERT_BLOCK,), jnp.float32).at[dest].set(g_flat[order])
    blk_exp = jnp.minimum(jnp.searchsorted(pad_end, jnp.arange(nblk) * EXPERT_BLOCK, side='right'),
                          N_EXPERTS - 1)

    def expert_block(args):
        tok, gate, e = args
        xb = hf[tok]
        gu = xb @ w_gate_up[e] + b_gate_up[e]
        g, u = jnp.split(gu, 2, axis=-1)
        g = jnp.minimum(g, SWIGLU_LIMIT)
        u = jnp.clip(u, -SWIGLU_LIMIT, SWIGLU_LIMIT)
        act = (u + 1.0) * (g * jax.nn.sigmoid(SWIGLU_ALPHA * g))
        y = act @ w_down[e] + b_down[e]
        return y.astype(jnp.float32) * gate[:, None]

    y = lax.map(expert_block, (row_tok.reshape(nblk, EXPERT_BLOCK),
                               row_gate.reshape(nblk, EXPERT_BLOCK), blk_exp))
    out = jnp.zeros((T, D), jnp.float32).at[row_tok].add(y.reshape(-1, D))
    return out.reshape(B, S, D).astype(h.dtype)


def setup_inputs(seed: int = 0) -> dict:
    key = jax.random.key(seed)
    ks = jax.random.split(key, 17)
    f32 = jnp.float32
    D, E, F = D_MODEL, N_EXPERTS, D_FF
    nrm = lambda k, shape, s: jax.random.normal(k, shape, f32) * s
    return {
        'x': nrm(ks[0], (BATCH, SEQ, D), 1.0),
        'c': nrm(ks[1], (BATCH, D), 1.0),
        'w_ada': nrm(ks[2], (DEPTH, D, 6 * D), D ** -0.5),
        'b_ada': nrm(ks[3], (DEPTH, 6 * D), 0.02),
        'w_in': nrm(ks[4], (DEPTH, D, IN_WIDTH), D ** -0.5),
        'b_forget': 3.0 + nrm(ks[5], (DEPTH, N_FOX_HEADS), 0.1),
        'w_out': nrm(ks[6], (DEPTH, D, D), D ** -0.5 * DEEPNORM_BETA),
        'ln1_g': 1.0 + nrm(ks[7], (DEPTH, D), 0.02),
        'ln1_b': nrm(ks[8], (DEPTH, D), 0.02),
        'w_router': nrm(ks[9], (DEPTH, D, E), D ** -0.5),
        'b_router': nrm(ks[10], (DEPTH, E), 0.01),
        'w_gate_up': nrm(ks[11], (DEPTH, E, D, 2 * F), D ** -0.5),
        'b_gate_up': nrm(ks[12], (DEPTH, E, 2 * F), 0.02),
        'w_down': nrm(ks[13], (DEPTH, E, F, D), F ** -0.5 * DEEPNORM_BETA),
        'b_down': nrm(ks[14], (DEPTH, E, D), 0.02),
        'ln2_g': 1.0 + nrm(ks[15], (DEPTH, D), 0.02),
        'ln2_b': nrm(ks[16], (DEPTH, D), 0.02),
    }


def reference(x, c, w_ada, b_ada, w_in, b_forget, w_out, ln1_g, ln1_b, w_router, b_router,
              w_gate_up, b_gate_up, w_down, b_down, ln2_g, ln2_b):
    B, S, D = x.shape
    slopes = alibi_slopes(N_DIL_HEADS)
    cond = jax.nn.silu(c)
    heads = lambda t: t.reshape(B, S, -1, HEAD_DIM)
    for layer in range(DEPTH):
        mod = cond @ w_ada[layer] + b_ada[layer]
        sh_a, sc_a, g_a, sh_m, sc_m, g_m = jnp.split(mod[:, None, :], 6, axis=-1)
        h = x * (1.0 + sc_a) + sh_a
        proj = h @ w_in[layer]
        qa, ka, va, qf, kf, vf, f_logit = jnp.split(proj, IN_SPLITS, axis=-1)
        o_dil = dilated_attention(heads(qa), heads(ka), heads(va), slopes)
        log_f = jax.nn.log_sigmoid((f_logit + b_forget[layer]).astype(jnp.float32))
        o_fox = forgetting_attention(heads(qf), heads(kf), heads(vf), log_f)
        mixed = jnp.concatenate([o_dil.reshape(B, S, DIL_WIDTH), o_fox.reshape(B, S, FOX_WIDTH)],
                                axis=-1).astype(x.dtype)
        attn = mixed @ w_out[layer]
        x = layer_norm(DEEPNORM_ALPHA * x + g_a * attn, ln1_g[layer], ln1_b[layer])
        h = x * (1.0 + sc_m) + sh_m
        ffn = moe_ffn(h, w_router[layer], b_router[layer], w_gate_up[layer], b_gate_up[layer],
                      w_down[layer], b_down[layer])
        x = layer_norm(DEEPNORM_ALPHA * x + g_m * ffn, ln2_g[layer], ln2_b[layer])
    return x
```

```python
import functools
import math

import jax
import jax.numpy as jnp
from jax import lax
from jax.experimental import pallas as pl
from jax.experimental.pallas import tpu as pltpu

F32 = jnp.float32
BF16 = jnp.bfloat16
I32 = jnp.int32
U32 = jnp.uint32

HEAD_DIM = 128
Q_BLOCK = 128
DIL_PATTERNS = ((128, 1), (512, 4), (2048, 16))
N_EXPERTS = 32
TOP_K = 4
SWIGLU_LIMIT = 7.0
SWIGLU_ALPHA = 1.702
LN_EPS = 1e-5
DEPTH = 1
DEEPNORM_ALPHA = (2.0 * DEPTH) ** 0.25

LANES = 128
VMEM_LIMIT = 56 * 1024 * 1024

ROW_BLOCK = 256
ITEM_ROWS = 1536
FF_CHUNK = 256


def _cparams(sem):
    return pltpu.CompilerParams(dimension_semantics=sem, vmem_limit_bytes=VMEM_LIMIT)


def _dot(a, b):
    return jnp.dot(a, b, preferred_element_type=F32)


def _dot_nt(a, b):
    return lax.dot_general(a, b, (((1,), (1,)), ((), ())), preferred_element_type=F32)


def _split_bf16(x):
    hi = x.astype(BF16)
    lo = (x - hi.astype(F32)).astype(BF16)
    return hi, lo


def _dot_split(x_hi, x_lo, w_hi, w_lo):
    return _dot(x_hi, w_hi) + (_dot(x_hi, w_lo) + _dot(x_lo, w_hi))


def _layer_norm(y, g, b):
    mu = jnp.mean(y, axis=-1, keepdims=True)
    yc = y - mu
    var = jnp.mean(yc * yc, axis=-1, keepdims=True)
    return yc * lax.rsqrt(var + LN_EPS) * g + b


def _ada_kernel(ct_ref, w_ref, b_ref, o_ref):
    ct = ct_ref[...]
    s = ct * jax.nn.sigmoid(ct)
    d_model, nb = ct.shape
    tn = w_ref.shape[1]
    sb = [jnp.broadcast_to(s[:, b:b + 1], (d_model, LANES)) for b in range(nb)]
    for j in range(tn // LANES):
        cols = slice(j * LANES, (j + 1) * LANES)
        w = w_ref[:, cols]
        for b in range(nb):
            acc = jnp.sum(w * sb[b], axis=0, keepdims=True)
            o_ref[b:b + 1, cols] = acc + b_ref[:, cols]


def _ada_mod(c, w, b):
    nb, d_model = c.shape
    n = w.shape[1]
    tn = 1536 if n % 1536 == 0 else LANES
    return pl.pallas_call(
        _ada_kernel,
        grid=(n // tn,),
        in_specs=[pl.BlockSpec((d_model, nb), lambda j: (0, 0)),
                  pl.BlockSpec((d_model, tn), lambda j: (0, j)),
                  pl.BlockSpec((1, tn), lambda j: (0, j))],
        out_specs=pl.BlockSpec((nb, tn), lambda j: (0, j)),
        out_shape=jax.ShapeDtypeStruct((nb, n), F32),
        compiler_params=_cparams(("arbitrary",)),
        name="ada_mod",
    )(c.T, w, b.reshape(1, n))


def _inproj_kernel(x_ref, sc_ref, sh_ref, w_ref, wfh_ref, wfl_ref,
                   odil_ref, ofox_ref, of_ref, h_scr, *, n_dil_tiles, q_tiles, scale):
    j = pl.program_id(1)

    @pl.when(j == 0)
    def _():
        h = x_ref[...] * (1.0 + sc_ref[0]) + sh_ref[0]
        h_hi, h_lo = _split_bf16(h)
        h_scr[...] = h_hi
        of_ref[...] = _dot_split(h_hi, h_lo, wfh_ref[...], wfl_ref[...])

    acc = _dot(h_scr[...], w_ref[...].astype(BF16))
    is_q = (j < q_tiles) | ((j >= n_dil_tiles) & (j < n_dil_tiles + q_tiles))
    acc = acc * jnp.where(is_q, scale, 1.0).astype(F32)

    @pl.when(j < n_dil_tiles)
    def _():
        odil_ref[...] = acc

    @pl.when(j >= n_dil_tiles)
    def _():
        ofox_ref[...] = acc.astype(BF16)


def _in_proj(x2, sc, sh, w_in, seq, dil_width, fox_width, n_fox_heads):
    t, d_model = x2.shape
    tm, tn = min(1024, seq), min(512, dil_width)
    n_dil_tiles = 3 * dil_width // tn
    n_fox_tiles = 3 * fox_width // tn
    q_tiles = dil_width // tn
    assert dil_width == fox_width and dil_width % tn == 0 and seq % tm == 0
    main = 3 * dil_width + 3 * fox_width
    wf = jnp.pad(w_in[:, main:], ((0, 0), (0, LANES - n_fox_heads)))
    wf_hi, wf_lo = _split_bf16(wf)
    kern = functools.partial(_inproj_kernel, n_dil_tiles=n_dil_tiles, q_tiles=q_tiles,
                             scale=1.0 / math.sqrt(HEAD_DIM))
    bidx = lambda i, j: (i * tm // seq, 0, 0)
    return pl.pallas_call(
        kern,
        grid=(t // tm, n_dil_tiles + n_fox_tiles),
        in_specs=[pl.BlockSpec((tm, d_model), lambda i, j: (i, 0)),
                  pl.BlockSpec((1, 1, d_model), bidx),
                  pl.BlockSpec((1, 1, d_model), bidx),
                  pl.BlockSpec((d_model, tn), lambda i, j: (0, j)),
                  pl.BlockSpec((d_model, LANES), lambda i, j: (0, 0)),
                  pl.BlockSpec((d_model, LANES), lambda i, j: (0, 0))],
        out_specs=[pl.BlockSpec((tm, tn), lambda i, j: (i, jnp.minimum(j, n_dil_tiles - 1))),
                   pl.BlockSpec((tm, tn), lambda i, j: (i, jnp.maximum(j - n_dil_tiles, 0))),
                   pl.BlockSpec((tm, LANES), lambda i, j: (i, 0))],
        out_shape=[jax.ShapeDtypeStruct((t, 3 * dil_width), F32),
                   jax.ShapeDtypeStruct((t, 3 * fox_width), BF16),
                   jax.ShapeDtypeStruct((t, LANES), F32)],
        scratch_shapes=[pltpu.VMEM((tm, d_model), BF16)],
        compiler_params=_cparams(("arbitrary", "arbitrary")),
        name="in_proj",
    )(x2, sc, sh, w_in, wf_hi, wf_lo)


def _cum_kernel(f_ref, b_ref, o_ref):
    z = f_ref[0] + b_ref[...]
    c = jnp.minimum(z, 0.0) - jnp.log1p(jnp.exp(-jnp.abs(z)))
    seq = z.shape[1]
    lane = lax.broadcasted_iota(I32, z.shape, 1)
    shift = 1
    while shift < seq:
        c = c + jnp.where(lane >= shift, pltpu.roll(c, shift, axis=1), 0.0)
        shift *= 2
    o_ref[0] = c


def _forget_cumsum(f_logit_t, b_forget):
    nb, hf, seq = f_logit_t.shape
    return pl.pallas_call(
        _cum_kernel,
        grid=(nb,),
        in_specs=[pl.BlockSpec((1, hf, seq), lambda b: (b, 0, 0)),
                  pl.BlockSpec((hf, 1), lambda b: (0, 0))],
        out_specs=pl.BlockSpec((1, hf, seq), lambda b: (b, 0, 0)),
        out_shape=jax.ShapeDtypeStruct((nb, hf, seq), F32),
        compiler_params=_cparams(("arbitrary",)),
        name="forget_cumsum",
    )(f_logit_t, b_forget.reshape(hf, 1))


def _dil_kernel(slope_ref, q_ref, k_ref, v_ref, o_ref, acc_ref, m_ref, l_ref, *, patterns, seq):
    slope = slope_ref[pl.program_id(1)]
    qb = Q_BLOCK
    row = lax.broadcasted_iota(I32, (qb, qb), 0)
    col = lax.broadcasted_iota(I32, (qb, qb), 1)
    delta_cur = row - col
    delta_prev = delta_cur + qb
    neg_inf = F32(-jnp.inf)

    for pi, (window, dil) in enumerate(patterns):
        reach = window // dil
        nblk = seq // dil // qb
        assert reach <= qb and nblk * qb * dil == seq
        sd = slope * float(dil)
        bias_cur = jnp.where((delta_cur >= 0) & (delta_cur <= reach),
                             -sd * delta_cur.astype(F32), neg_inf)
        bias_prev = jnp.where(delta_prev <= reach, -sd * delta_prev.astype(F32), neg_inf)

        def body(idx, carry, dil=dil, nblk=nblk, bias_cur=bias_cur, bias_prev=bias_prev, pi=pi):
            r = idx // nblk
            n = idx - r * nblk
            start = r + n * (qb * dil)
            pstart = jnp.maximum(start - qb * dil, r)
            cur = pl.ds(start, qb, stride=dil) if dil > 1 else pl.ds(pl.multiple_of(start, qb), qb)
            prev = pl.ds(pstart, qb, stride=dil) if dil > 1 else pl.ds(pl.multiple_of(pstart, qb), qb)
            q = q_ref[0, cur, :].astype(BF16)
            s_cur = _dot_nt(q, k_ref[0, cur, :].astype(BF16)) + bias_cur
            s_prev = _dot_nt(q, k_ref[0, prev, :].astype(BF16)) + jnp.where(n > 0, bias_prev, neg_inf)
            m_blk = jnp.maximum(jnp.max(s_cur, axis=1, keepdims=True),
                                jnp.max(s_prev, axis=1, keepdims=True))
            p_cur = jnp.exp(s_cur - m_blk)
            p_prev = jnp.exp(s_prev - m_blk)
            l_blk = jnp.sum(p_cur, axis=1, keepdims=True) + jnp.sum(p_prev, axis=1, keepdims=True)
            o_blk = (_dot(p_cur.astype(BF16), v_ref[0, cur, :].astype(BF16))
                     + _dot(p_prev.astype(BF16), v_ref[0, prev, :].astype(BF16)))
            if pi == 0:
                m_ref[cur, :] = m_blk
                l_ref[cur, :] = l_blk
                acc_ref[cur, :] = o_blk
            else:
                m_old = m_ref[cur, :]
                m_new = jnp.maximum(m_old, m_blk)
                a_old = jnp.exp(m_old - m_new)
                a_blk = jnp.exp(m_blk - m_new)
                m_ref[cur, :] = m_new
                l_ref[cur, :] = l_ref[cur, :] * a_old + l_blk * a_blk
                acc_ref[cur, :] = acc_ref[cur, :] * a_old + o_blk * a_blk
            return carry

        lax.fori_loop(0, dil * nblk, body, 0)

    chunk = 512
    for c0 in range(0, seq, chunk):
        rows = slice(c0, c0 + chunk)
        o_ref[0, rows, :] = (acc_ref[rows, :] / l_ref[rows, :]).astype(o_ref.dtype)


def _dilated_attention(qkv, slopes, nb, seq, n_heads):
    dh = HEAD_DIM
    kern = functools.partial(_dil_kernel, patterns=DIL_PATTERNS, seq=seq)
    blk = lambda off: pl.BlockSpec((1, seq, dh), lambda b, h, s: (b, 0, off + h))
    return pl.pallas_call(
        kern,
        grid_spec=pltpu.PrefetchScalarGridSpec(
            num_scalar_prefetch=1,
            grid=(nb, n_heads),
            in_specs=[blk(0), blk(n_heads), blk(2 * n_heads)],
            out_specs=pl.BlockSpec((1, seq, dh), lambda b, h, s: (b, 0, h)),
            scratch_shapes=[pltpu.VMEM((seq, dh), F32),
                            pltpu.VMEM((seq, 1), F32),
                            pltpu.VMEM((seq, 1), F32)]),
        out_shape=jax.ShapeDtypeStruct((nb, seq, n_heads * dh), BF16),
        compiler_params=_cparams(("arbitrary", "arbitrary")),
        name="dilated_attn",
    )(slopes, qkv, qkv, qkv)


def _fox_kernel(q_ref, k_ref, v_ref, c_ref, o_ref, *, blk):
    i = pl.program_id(2)
    q = q_ref[0]
    dh = q.shape[1]
    row = lax.broadcasted_iota(I32, (blk, blk), 0)
    col = lax.broadcasted_iota(I32, (blk, blk), 1)

    def step(j, carry, masked):
        m, l, acc = carry
        rows = pl.ds(pl.multiple_of(j * blk, blk), blk)
        s = _dot_nt(q, k_ref[0, rows, :]) - c_ref[0, 0, j]
        if masked:
            s = jnp.where(col <= row, s, -jnp.inf)
        m_new = jnp.maximum(m, jnp.max(s, axis=1, keepdims=True))
        alpha = jnp.exp(m - m_new)
        p = jnp.exp(s - m_new)
        l = alpha * l + jnp.sum(p, axis=1, keepdims=True)
        acc = alpha * acc + _dot(p.astype(BF16), v_ref[0, rows, :])
        return m_new, l, acc

    init = (jnp.full((blk, 1), -jnp.inf, F32), jnp.zeros((blk, 1), F32), jnp.zeros((blk, dh), F32))
    carry = lax.fori_loop(0, i, lambda j, c: step(j, c, False), init)
    m, l, acc = step(i, carry, True)
    o_ref[0] = (acc / l).astype(o_ref.dtype)


def _fox_attention(qkv, cum, nb, seq, n_heads, blk):
    dh = HEAD_DIM
    kern = functools.partial(_fox_kernel, blk=blk)
    return pl.pallas_call(
        kern,
        grid=(nb, n_heads, seq // blk),
        in_specs=[pl.BlockSpec((1, blk, dh), lambda b, h, i: (b, i, h)),
                  pl.BlockSpec((1, seq, dh), lambda b, h, i: (b, 0, n_heads + h)),
                  pl.BlockSpec((1, seq, dh), lambda b, h, i: (b, 0, 2 * n_heads + h)),
                  pl.BlockSpec((1, 1, seq // blk, 1, blk), lambda b, h, i: (b, h, 0, 0, 0))],
        out_specs=pl.BlockSpec((1, blk, dh), lambda b, h, i: (b, i, h)),
        out_shape=jax.ShapeDtypeStruct((nb, seq, n_heads * dh), BF16),
        compiler_params=_cparams(("arbitrary", "arbitrary", "arbitrary")),
        name="fox_attn",
    )(qkv, qkv, qkv, cum)


def _post_attn_kernel(od_ref, of_ref, x_ref, w_ref, ga_ref, scm_ref, shm_ref, g_ref, b_ref,
                      wrh_ref, wrl_ref, br_ref,
                      x1_ref, hp_ref, idx_ref, gate_ref, rank_ref, cnt_ref, carry_scr,
                      *, n_experts, top_k, dil_width):
    i = pl.program_id(0)

    @pl.when(i == 0)
    def _():
        carry_scr[...] = jnp.zeros_like(carry_scr)

    attn = _dot(od_ref[...], w_ref[:dil_width, :]) + _dot(of_ref[...], w_ref[dil_width:, :])
    x1 = _layer_norm(DEEPNORM_ALPHA * x_ref[...] + ga_ref[0] * attn, g_ref[...], b_ref[...])
    x1_ref[...] = x1
    h = x1 * (1.0 + scm_ref[0]) + shm_ref[0]
    h_hi, h_lo = _split_bf16(h)
    hp_ref[...] = h

    logits = _dot_split(h_hi, h_lo, wrh_ref[...], wrl_ref[...]) + br_ref[...]
    tm = logits.shape[0]
    lane = lax.broadcasted_iota(I32, (tm, LANES), 1)
    work = jnp.where(lane < n_experts, logits, -jnp.inf)
    vals, idxs = [], []
    for _ in range(top_k):
        mx = jnp.max(work, axis=1, keepdims=True)
        ix = jnp.min(jnp.where(work == mx, lane, LANES), axis=1, keepdims=True)
        vals.append(mx)
        idxs.append(ix)
        work = jnp.where(lane == ix, -jnp.inf, work)
    exps = [jnp.exp(v - vals[0]) for v in vals]
    denom = exps[0]
    for e in exps[1:]:
        denom = denom + e

    onehot = jnp.zeros((tm, LANES), F32)
    for ix in idxs:
        onehot = onehot + (lane == ix).astype(F32)
    r_i = lax.broadcasted_iota(I32, (tm, tm), 0)
    c_i = lax.broadcasted_iota(I32, (tm, tm), 1)
    tri = (c_i < r_i).astype(BF16)
    before = _dot(tri, onehot.astype(BF16)) + carry_scr[...]

    idx_out = jnp.zeros((tm, LANES), I32)
    gate_out = jnp.zeros((tm, LANES), F32)
    rank_out = jnp.zeros((tm, LANES), I32)
    for k in range(top_k):
        rank_k = jnp.sum(jnp.where(lane == idxs[k], before, 0.0), axis=1, keepdims=True)
        idx_out = jnp.where(lane == k, idxs[k], idx_out)
        gate_out = jnp.where(lane == k, exps[k] / denom, gate_out)
        rank_out = jnp.where(lane == k, rank_k.astype(I32), rank_out)
    idx_ref[...] = idx_out
    gate_ref[...] = gate_out
    rank_ref[...] = rank_out
    carry_scr[...] = carry_scr[...] + jnp.sum(onehot, axis=0, keepdims=True)
    cnt_ref[...] = jnp.broadcast_to(carry_scr[...], cnt_ref.shape)


def _post_attn(o_dil, o_fox, x2, w_out_bf, g_a, sc_m, sh_m, ln_g, ln_b, w_router, b_router, seq):
    t, d_model = x2.shape
    dil_width = o_dil.shape[1]
    n_experts = w_router.shape[1]
    tm = 256
    wr = jnp.pad(w_router, ((0, 0), (0, LANES - n_experts)))
    wr_hi, wr_lo = _split_bf16(wr)
    br = jnp.pad(b_router, (0, LANES - n_experts)).reshape(1, LANES)
    kern = functools.partial(_post_attn_kernel, n_experts=n_experts, top_k=TOP_K, dil_width=dil_width)
    row = lambda w: pl.BlockSpec((tm, w), lambda i: (i, 0))
    const = lambda r, w: pl.BlockSpec((r, w), lambda i: (0, 0))
    bvec = pl.BlockSpec((1, 1, d_model), lambda i: (i * tm // seq, 0, 0))
    return pl.pallas_call(
        kern,
        grid=(t // tm,),
        in_specs=[row(dil_width), row(o_fox.shape[1]), row(d_model), const(d_model, d_model),
                  bvec, bvec, bvec, const(1, d_model), const(1, d_model),
                  const(d_model, LANES), const(d_model, LANES), const(1, LANES)],
        out_specs=[row(d_model), row(d_model), row(LANES), row(LANES), row(LANES),
                   pl.BlockSpec((8, LANES), lambda i: (0, 0))],
        out_shape=[jax.ShapeDtypeStruct((t, d_model), F32),
                   jax.ShapeDtypeStruct((t, d_model), F32),
                   jax.ShapeDtypeStruct((t, LANES), I32),
                   jax.ShapeDtypeStruct((t, LANES), F32),
                   jax.ShapeDtypeStruct((t, LANES), I32),
                   jax.ShapeDtypeStruct((8, LANES), F32)],
        scratch_shapes=[pltpu.VMEM((1, LANES), F32)],
        compiler_params=_cparams(("arbitrary",)),
        name="post_attn",
    )(o_dil, o_fox, x2, w_out_bf, g_a, sc_m, sh_m, ln_g.reshape(1, -1), ln_b.reshape(1, -1),
      wr_hi, wr_lo, br)


def _scatter_kernel(dest_ref, hp_ref, xs_in_ref, xs_ref, sem, *, top_k):
    del xs_in_ref
    i = pl.program_id(0)
    tm = hp_ref.shape[0]

    def issue(t, carry):
        for k in range(top_k):
            d = dest_ref[(i * tm + t) * top_k + k]
            pltpu.make_async_copy(hp_ref.at[pl.ds(t, 1)], xs_ref.at[pl.ds(d, 1)], sem).start()
        return carry

    def drain(t, carry):
        for k in range(top_k):
            pltpu.make_async_copy(hp_ref.at[pl.ds(0, 1)], xs_ref.at[pl.ds(0, 1)], sem).wait()
        return carry

    lax.fori_loop(0, tm, issue, 0)
    lax.fori_loop(0, tm, drain, 0)


def _scatter_rows(dest_flat, hp, n_rows):
    t, w = hp.shape
    tm = 256
    kern = functools.partial(_scatter_kernel, top_k=TOP_K)
    return pl.pallas_call(
        kern,
        grid_spec=pltpu.PrefetchScalarGridSpec(
            num_scalar_prefetch=1,
            grid=(t // tm,),
            in_specs=[pl.BlockSpec((tm, w), lambda i, d: (i, 0)),
                      pl.BlockSpec(memory_space=pl.ANY)],
            out_specs=pl.BlockSpec(memory_space=pl.ANY),
            scratch_shapes=[pltpu.SemaphoreType.DMA(())]),
        out_shape=jax.ShapeDtypeStruct((n_rows, w), F32),
        input_output_aliases={2: 0},
        compiler_params=_cparams(("arbitrary",)),
        name="scatter_rows",
    )(dest_flat, hp, jnp.zeros((n_rows, w), F32))


def _moe_kernel(iexp_ref, irow_ref, inb_ref, used_ref, xs_ref, wg_ref, wu_ref, wd_ref, bg_ref, bu_ref,
                bd_ref, y_ref, xbuf, yacc, sem_in, sem_out, *, n_chunks):
    del iexp_ref
    it = pl.program_id(0)
    f = pl.program_id(1)
    nb = inb_ref[it]
    row0 = irow_ref[it]
    rb = ROW_BLOCK
    rows_of = lambda b: pl.ds(pl.multiple_of(b * rb, rb), rb)
    src_of = lambda b: pl.ds(pl.multiple_of(row0 + b * rb, rb), rb)

    @pl.when((f == 0) & (nb > 0))
    def _():
        def start(b, c):
            pltpu.make_async_copy(xs_ref.at[src_of(b)], xbuf.at[rows_of(b)], sem_in).start()
            return c

        def init(b, c):
            yacc[rows_of(b), :] = jnp.broadcast_to(bd_ref[0], (rb, yacc.shape[1]))
            return c

        def wait(b, c):
            pltpu.make_async_copy(xs_ref.at[src_of(b)], xbuf.at[rows_of(b)], sem_in).wait()
            return c

        lax.fori_loop(0, nb, start, 0)
        lax.fori_loop(0, nb, init, 0)
        lax.fori_loop(0, nb, wait, 0)

    wg = wg_ref[0].astype(BF16)
    wu = wu_ref[0].astype(BF16)
    wd = wd_ref[0].astype(BF16)
    bg = bg_ref[0]
    bu = bu_ref[0]

    def block(b, c):
        xb = xbuf[rows_of(b), :].astype(BF16)
        g = _dot(xb, wg) + bg
        u = _dot(xb, wu) + bu
        g = jnp.minimum(g, SWIGLU_LIMIT)
        u = jnp.clip(u, -SWIGLU_LIMIT, SWIGLU_LIMIT)
        act = (u + 1.0) * (g * jax.nn.sigmoid(SWIGLU_ALPHA * g))
        yacc[rows_of(b), :] += _dot(act.astype(BF16), wd)
        return c

    lax.fori_loop(0, nb, block, 0)

    @pl.when((f == n_chunks - 1) & (nb > 0))
    def _():
        def start(b, c):
            pltpu.make_async_copy(yacc.at[rows_of(b)], y_ref.at[src_of(b)], sem_out).start()
            return c

        def wait(b, c):
            pltpu.make_async_copy(yacc.at[rows_of(b)], y_ref.at[src_of(b)], sem_out).wait()
            return c

        lax.fori_loop(0, nb, start, 0)
        lax.fori_loop(0, nb, wait, 0)

    @pl.when((it == pl.num_programs(0) - 1) & (f == n_chunks - 1))
    def _():
        first = used_ref[0]
        last = y_ref.shape[0] // rb
        dst_of = lambda b: pl.ds(pl.multiple_of(b * rb, rb), rb)
        yacc[pl.ds(0, rb), :] = jnp.zeros((rb, yacc.shape[1]), yacc.dtype)

        def start(b, c):
            pltpu.make_async_copy(yacc.at[pl.ds(0, rb)], y_ref.at[dst_of(b)], sem_out).start()
            return c

        def wait(b, c):
            pltpu.make_async_copy(yacc.at[pl.ds(0, rb)], y_ref.at[dst_of(b)], sem_out).wait()
            return c

        lax.fori_loop(first, last, start, 0)
        lax.fori_loop(first, last, wait, 0)


def _moe_experts(item_exp, item_row, item_nblk, used_blocks, xs, w_gate_up, b_gate_up, w_down, b_down):
    n_rows = xs.shape[0]
    n_exp, d_model, two_f = w_gate_up.shape
    d_ff = two_f // 2
    fc = FF_CHUNK
    n_chunks = d_ff // fc
    n_items = item_exp.shape[0]
    kern = functools.partial(_moe_kernel, n_chunks=n_chunks)

    def fsel(it, f, nbk):
        return jnp.where(nbk[it] > 0, f, n_chunks - 1)

    return pl.pallas_call(
        kern,
        grid_spec=pltpu.PrefetchScalarGridSpec(
            num_scalar_prefetch=4,
            grid=(n_items, n_chunks),
            in_specs=[
                pl.BlockSpec(memory_space=pl.ANY),
                pl.BlockSpec((1, d_model, fc), lambda it, f, ex, ro, nbk, us: (ex[it], 0, fsel(it, f, nbk))),
                pl.BlockSpec((1, d_model, fc),
                             lambda it, f, ex, ro, nbk, us: (ex[it], 0, n_chunks + fsel(it, f, nbk))),
                pl.BlockSpec((1, fc, d_model), lambda it, f, ex, ro, nbk, us: (ex[it], fsel(it, f, nbk), 0)),
                pl.BlockSpec((1, 1, fc), lambda it, f, ex, ro, nbk, us: (ex[it], 0, fsel(it, f, nbk))),
                pl.BlockSpec((1, 1, fc),
                             lambda it, f, ex, ro, nbk, us: (ex[it], 0, n_chunks + fsel(it, f, nbk))),
                pl.BlockSpec((1, 1, d_model), lambda it, f, ex, ro, nbk, us: (ex[it], 0, 0)),
            ],
            out_specs=pl.BlockSpec(memory_space=pl.ANY),
            scratch_shapes=[pltpu.VMEM((ITEM_ROWS, d_model), F32),
                            pltpu.VMEM((ITEM_ROWS, d_model), F32),
                            pltpu.SemaphoreType.DMA(()),
                            pltpu.SemaphoreType.DMA(())]),
        out_shape=jax.ShapeDtypeStruct((n_rows, d_model), F32),
        compiler_params=_cparams(("arbitrary", "arbitrary")),
        name="moe_experts",
    )(item_exp, item_row, item_nblk, used_blocks, xs, w_gate_up, w_gate_up, w_down,
      b_gate_up.reshape(n_exp, 1, two_f), b_gate_up.reshape(n_exp, 1, two_f),
      b_down.reshape(n_exp, 1, d_model))


def _combine_kernel(dest_ref, y_ref, x1_ref, gate_ref, gm_ref, g_ref, b_ref, o_ref, ybuf, sem, *, top_k):
    i = pl.program_id(0)
    tm = x1_ref.shape[0]

    def issue(t, carry):
        for k in range(top_k):
            d = dest_ref[(i * tm + t) * top_k + k]
            pltpu.make_async_copy(y_ref.at[pl.ds(d, 1)], ybuf.at[k, pl.ds(t, 1)], sem).start()
        return carry

    def drain(t, carry):
        for k in range(top_k):
            pltpu.make_async_copy(y_ref.at[pl.ds(0, 1)], ybuf.at[k, pl.ds(0, 1)], sem).wait()
        return carry

    lax.fori_loop(0, tm, issue, 0)
    lax.fori_loop(0, tm, drain, 0)
    gates = gate_ref[...]
    ffn = gates[:, 0:1] * ybuf[0]
    for k in range(1, top_k):
        ffn = ffn + gates[:, k:k + 1] * ybuf[k]
    o_ref[...] = _layer_norm(DEEPNORM_ALPHA * x1_ref[...] + gm_ref[0] * ffn, g_ref[...], b_ref[...])


def _combine(dest_flat, y, x1, gates, g_m, ln_g, ln_b, seq):
    t, d_model = x1.shape
    tm = 128
    kern = functools.partial(_combine_kernel, top_k=TOP_K)
    return pl.pallas_call(
        kern,
        grid_spec=pltpu.PrefetchScalarGridSpec(
            num_scalar_prefetch=1,
            grid=(t // tm,),
            in_specs=[pl.BlockSpec(memory_space=pl.ANY),
                      pl.BlockSpec((tm, d_model), lambda i, d: (i, 0)),
                      pl.BlockSpec((tm, LANES), lambda i, d: (i, 0)),
                      pl.BlockSpec((1, 1, d_model), lambda i, d: (i * tm // seq, 0, 0)),
                      pl.BlockSpec((1, d_model), lambda i, d: (0, 0)),
                      pl.BlockSpec((1, d_model), lambda i, d: (0, 0))],
            out_specs=pl.BlockSpec((tm, d_model), lambda i, d: (i, 0)),
            scratch_shapes=[pltpu.VMEM((TOP_K, tm, d_model), F32),
                            pltpu.SemaphoreType.DMA(())]),
        out_shape=jax.ShapeDtypeStruct((t, d_model), F32),
        compiler_params=_cparams(("arbitrary",)),
        name="combine_ln2",
    )(dest_flat, y, x1, gates, g_m, ln_g.reshape(1, -1), ln_b.reshape(1, -1))


def _routing_plan(counts, idx, rank, n_tokens):
    n_exp = counts.shape[0]
    padded = (counts + ROW_BLOCK - 1) // ROW_BLOCK * ROW_BLOCK
    pad_end = jnp.cumsum(padded)
    pad_start = pad_end - padded
    dest = (pad_start[idx] + rank).reshape(-1).astype(I32)

    max_rows = (n_tokens * TOP_K + n_exp * (ROW_BLOCK - 1) + ROW_BLOCK - 1) // ROW_BLOCK * ROW_BLOCK
    max_items = n_exp + max_rows // ITEM_ROWS
    items_per = (padded + ITEM_ROWS - 1) // ITEM_ROWS
    items_end = jnp.cumsum(items_per)
    total = items_end[-1]
    it = jnp.arange(max_items, dtype=I32)
    last = jnp.maximum(total - 1, 0)
    it_c = jnp.minimum(it, last)
    exp_of = jnp.sum((items_end[None, :] <= it_c[:, None]).astype(I32), axis=1)
    exp_of = jnp.minimum(exp_of, n_exp - 1)
    local = it_c - (items_end[exp_of] - items_per[exp_of])
    row0 = pad_start[exp_of] + local * ITEM_ROWS
    rows = jnp.minimum(padded[exp_of] - local * ITEM_ROWS, ITEM_ROWS)
    nblk = jnp.where(it < total, rows // ROW_BLOCK, 0)
    used_blocks = (pad_end[-1:] // ROW_BLOCK).astype(I32)
    return dest, (exp_of.astype(I32), row0.astype(I32), nblk.astype(I32), used_blocks), max_rows


def kernel(x, c, w_ada, b_ada, w_in, b_forget, w_out, ln1_g, ln1_b, w_router, b_router,
           w_gate_up, b_gate_up, w_down, b_down, ln2_g, ln2_b):
    nb, seq, d_model = x.shape
    n_heads = d_model // HEAD_DIM
    n_dil = n_heads // 2
    n_fox = n_heads - n_dil
    dil_width, fox_width = n_dil * HEAD_DIM, n_fox * HEAD_DIM
    t = nb * seq
    layer = 0

    mod = _ada_mod(c, w_ada[layer], b_ada[layer])
    sh_a, sc_a, g_a, sh_m, sc_m, g_m = [m.reshape(nb, 1, d_model) for m in jnp.split(mod, 6, axis=-1)]

    x2 = x.reshape(t, d_model)
    qkv_dil, qkv_fox, f_logit = _in_proj(x2, sc_a, sh_a, w_in[layer], seq, dil_width, fox_width, n_fox)

    slopes = 2.0 ** (-8.0 * jnp.arange(1, n_dil + 1, dtype=F32) / n_dil)
    o_dil = _dilated_attention(qkv_dil.reshape(nb, seq, 3 * dil_width), slopes, nb, seq, n_dil)

    fox_blk = 256
    f_t = f_logit[:, :n_fox].reshape(nb, seq, n_fox).transpose(0, 2, 1)
    cum = _forget_cumsum(f_t, b_forget[layer]).reshape(nb, n_fox, seq // fox_blk, 1, fox_blk)
    o_fox = _fox_attention(qkv_fox.reshape(nb, seq, 3 * fox_width), cum, nb, seq, n_fox, fox_blk)

    x1, hp, idx, gates, rank, cnt = _post_attn(
        o_dil.reshape(t, dil_width), o_fox.reshape(t, fox_width), x2, w_out[layer].astype(BF16),
        g_a, sc_m, sh_m, ln1_g[layer], ln1_b[layer], w_router[layer], b_router[layer], seq)

    counts = cnt[0, :N_EXPERTS].astype(I32)
    dest, items, max_rows = _routing_plan(counts, idx[:, :TOP_K], rank[:, :TOP_K], t)
    xs = _scatter_rows(dest, hp, max_rows)
    y = _moe_experts(*items, xs, w_gate_up[layer], b_gate_up[layer], w_down[layer], b_down[layer])
    out = _combine(dest, y, x1, gates, g_m, ln2_g[layer], ln2_b[layer], seq)
    return out.reshape(nb, seq, d_model)
```

```python
import functools
import math

import jax
import jax.numpy as jnp
from jax import lax
from jax.experimental import pallas as pl
from jax.experimental.pallas import tpu as pltpu

F32 = jnp.float32
BF16 = jnp.bfloat16
I32 = jnp.int32
U32 = jnp.uint32

HEAD_DIM = 128
Q_BLOCK = 128
DIL_PATTERNS = ((128, 1), (512, 4), (2048, 16))
N_EXPERTS = 32
TOP_K = 4
SWIGLU_LIMIT = 7.0
SWIGLU_ALPHA = 1.702
LN_EPS = 1e-5
DEPTH = 1
DEEPNORM_ALPHA = (2.0 * DEPTH) ** 0.25

LOG2E = math.log2(math.e)
DIL_GROUP = 4
FOX_HEADS_PER_STEP = 4

LANES = 128
VMEM_LIMIT = 56 * 1024 * 1024

ROW_BLOCK = 256
ITEM_ROWS = 1536
FF_CHUNK = 256


def _cparams(sem):
    return pltpu.CompilerParams(dimension_semantics=sem, vmem_limit_bytes=VMEM_LIMIT)


def _dot(a, b):
    return jnp.dot(a, b, preferred_element_type=F32)


def _dot_nt(a, b):
    return lax.dot_general(a, b, (((1,), (1,)), ((), ())), preferred_element_type=F32)


def _split_bf16(x):
    hi = x.astype(BF16)
    lo = (x - hi.astype(F32)).astype(BF16)
    return hi, lo


def _dot_split(x_hi, x_lo, w_hi, w_lo):
    return _dot(x_hi, w_hi) + (_dot(x_hi, w_lo) + _dot(x_lo, w_hi))


def _layer_norm(y, g, b):
    mu = jnp.mean(y, axis=-1, keepdims=True)
    yc = y - mu
    var = jnp.mean(yc * yc, axis=-1, keepdims=True)
    return yc * lax.rsqrt(var + LN_EPS) * g + b


def _ada_kernel(ct_ref, w_ref, b_ref, o_ref):
    ct = ct_ref[...]
    s = ct * jax.nn.sigmoid(ct)
    d_model, nb = ct.shape
    tn = w_ref.shape[1]
    sb = [jnp.broadcast_to(s[:, b:b + 1], (d_model, LANES)) for b in range(nb)]
    for j in range(tn // LANES):
        cols = slice(j * LANES, (j + 1) * LANES)
        w = w_ref[:, cols]
        for b in range(nb):
            acc = jnp.sum(w * sb[b], axis=0, keepdims=True)
            o_ref[b:b + 1, cols] = acc + b_ref[:, cols]


def _ada_mod(c, w, b):
    nb, d_model = c.shape
    n = w.shape[1]
    tn = 1536 if n % 1536 == 0 else LANES
    return pl.pallas_call(
        _ada_kernel,
        grid=(n // tn,),
        in_specs=[pl.BlockSpec((d_model, nb), lambda j: (0, 0)),
                  pl.BlockSpec((d_model, tn), lambda j: (0, j)),
                  pl.BlockSpec((1, tn), lambda j: (0, j))],
        out_specs=pl.BlockSpec((nb, tn), lambda j: (0, j)),
        out_shape=jax.ShapeDtypeStruct((nb, n), F32),
        compiler_params=_cparams(("arbitrary",)),
        name="ada_mod",
    )(c.T, w, b.reshape(1, n))


def _inproj_kernel(x_ref, sc_ref, sh_ref, w_ref, wfh_ref, wfl_ref,
                   odil_ref, ofox_ref, of_ref, h_scr, *, n_dil_tiles, q_tiles, scale):
    j = pl.program_id(1)

    @pl.when(j == 0)
    def _():
        h = x_ref[...] * (1.0 + sc_ref[0]) + sh_ref[0]
        h_hi, h_lo = _split_bf16(h)
        h_scr[...] = h_hi
        of_ref[...] = _dot_split(h_hi, h_lo, wfh_ref[...], wfl_ref[...])

    acc = _dot(h_scr[...], w_ref[...].astype(BF16))
    is_q = (j < q_tiles) | ((j >= n_dil_tiles) & (j < n_dil_tiles + q_tiles))
    acc = acc * jnp.where(is_q, scale, 1.0).astype(F32)

    @pl.when(j < n_dil_tiles)
    def _():
        odil_ref[...] = acc

    @pl.when(j >= n_dil_tiles)
    def _():
        ofox_ref[...] = acc.astype(BF16)


def _in_proj(x2, sc, sh, w_in, seq, dil_width, fox_width, n_fox_heads):
    t, d_model = x2.shape
    tm, tn = min(1024, seq), min(512, dil_width)
    n_dil_tiles = 3 * dil_width // tn
    n_fox_tiles = 3 * fox_width // tn
    q_tiles = dil_width // tn
    assert dil_width == fox_width and dil_width % tn == 0 and seq % tm == 0
    main = 3 * dil_width + 3 * fox_width
    wf = jnp.pad(w_in[:, main:], ((0, 0), (0, LANES - n_fox_heads)))
    wf_hi, wf_lo = _split_bf16(wf)
    kern = functools.partial(_inproj_kernel, n_dil_tiles=n_dil_tiles, q_tiles=q_tiles,
                             scale=LOG2E / math.sqrt(HEAD_DIM))
    bidx = lambda i, j: (i * tm // seq, 0, 0)
    return pl.pallas_call(
        kern,
        grid=(t // tm, n_dil_tiles + n_fox_tiles),
        in_specs=[pl.BlockSpec((tm, d_model), lambda i, j: (i, 0)),
                  pl.BlockSpec((1, 1, d_model), bidx),
                  pl.BlockSpec((1, 1, d_model), bidx),
                  pl.BlockSpec((d_model, tn), lambda i, j: (0, j)),
                  pl.BlockSpec((d_model, LANES), lambda i, j: (0, 0)),
                  pl.BlockSpec((d_model, LANES), lambda i, j: (0, 0))],
        out_specs=[pl.BlockSpec((tm, tn), lambda i, j: (i, jnp.minimum(j, n_dil_tiles - 1))),
                   pl.BlockSpec((tm, tn), lambda i, j: (i, jnp.maximum(j - n_dil_tiles, 0))),
                   pl.BlockSpec((tm, LANES), lambda i, j: (i, 0))],
        out_shape=[jax.ShapeDtypeStruct((t, 3 * dil_width), F32),
                   jax.ShapeDtypeStruct((t, 3 * fox_width), BF16),
                   jax.ShapeDtypeStruct((t, LANES), F32)],
        scratch_shapes=[pltpu.VMEM((tm, d_model), BF16)],
        compiler_params=_cparams(("arbitrary", "arbitrary")),
        name="in_proj",
    )(x2, sc, sh, w_in, wf_hi, wf_lo)


def _cum_kernel(f_ref, b_ref, o_ref):
    z = f_ref[0] + b_ref[...]
    c = jnp.minimum(z, 0.0) - jnp.log1p(jnp.exp(-jnp.abs(z)))
    seq = z.shape[1]
    lane = lax.broadcasted_iota(I32, z.shape, 1)
    shift = 1
    while shift < seq:
        c = c + jnp.where(lane >= shift, pltpu.roll(c, shift, axis=1), 0.0)
        shift *= 2
    o_ref[0] = c * LOG2E


def _forget_cumsum(f_logit_t, b_forget):
    nb, hf, seq = f_logit_t.shape
    return pl.pallas_call(
        _cum_kernel,
        grid=(nb,),
        in_specs=[pl.BlockSpec((1, hf, seq), lambda b: (b, 0, 0)),
                  pl.BlockSpec((hf, 1), lambda b: (0, 0))],
        out_specs=pl.BlockSpec((1, hf, seq), lambda b: (b, 0, 0)),
        out_shape=jax.ShapeDtypeStruct((nb, hf, seq), F32),
        compiler_params=_cparams(("arbitrary",)),
        name="forget_cumsum",
    )(f_logit_t, b_forget.reshape(hf, 1))


def _dil_kernel(slope_ref, q_ref, k_ref, v_ref, o_ref, acc_ref, l_ref, m_ref, bias_ref, *, patterns, seq):
    slope = slope_ref[pl.program_id(1)]
    qb = Q_BLOCK
    row = lax.broadcasted_iota(I32, (qb, 2 * qb), 0)
    col = lax.broadcasted_iota(I32, (qb, 2 * qb), 1)
    delta = row + qb - col
    is_prev = col < qb
    neg_inf = F32(-jnp.inf)
    ones = jnp.ones((2 * qb, HEAD_DIM), BF16)

    for pi, (window, dil) in enumerate(patterns):
        reach = window // dil
        nblk = seq // dil // qb
        n_blocks = dil * nblk
        assert reach <= qb and nblk * qb * dil == seq and n_blocks % DIL_GROUP == 0
        sd = slope * (float(dil) * LOG2E)
        bias_ref[...] = jnp.where((delta >= 0) & (delta <= reach), -sd * delta.astype(F32), neg_inf)

        def block(idx, dil=dil, nblk=nblk, pi=pi):
            r = idx // nblk
            n = idx - r * nblk
            start = r + n * (qb * dil)
            pstart = jnp.maximum(start - qb * dil, r)
            cur = pl.ds(start, qb, stride=dil) if dil > 1 else pl.ds(pl.multiple_of(start, qb), qb)
            prev = pl.ds(pstart, qb, stride=dil) if dil > 1 else pl.ds(pl.multiple_of(pstart, qb), qb)
            q = q_ref[0, cur, :].astype(BF16)
            kk = jnp.concatenate([k_ref[0, prev, :], k_ref[0, cur, :]], axis=0).astype(BF16)
            vv = jnp.concatenate([v_ref[0, prev, :], v_ref[0, cur, :]], axis=0).astype(BF16)
            s = _dot_nt(q, kk) + bias_ref[...]
            s = jnp.where(is_prev & (n == 0), neg_inf, s)
            m_blk = jnp.max(s, axis=1, keepdims=True)
            p = jnp.exp2(s - m_blk)
            pv = _dot(p.astype(BF16), jnp.concatenate([vv, ones], axis=1))
            o_blk, l_blk = pv[:, :HEAD_DIM], pv[:, HEAD_DIM:]
            if pi == 0:
                m_ref[cur, :] = jnp.broadcast_to(m_blk, (qb, LANES))
                acc_ref[cur, :] = o_blk
                l_ref[cur, :] = l_blk
            else:
                m_old = m_ref[cur, :]
                m_new = jnp.maximum(m_old, m_blk)
                a_old = jnp.exp2(m_old - m_new)
                a_blk = jnp.exp2(m_blk - m_new)
                m_ref[cur, :] = m_new
                acc_ref[cur, :] = acc_ref[cur, :] * a_old + o_blk * a_blk
                l_ref[cur, :] = l_ref[cur, :] * a_old + l_blk * a_blk

        def body(g, carry, block=block):
            for u in range(DIL_GROUP):
                block(g * DIL_GROUP + u)
            return carry

        lax.fori_loop(0, n_blocks // DIL_GROUP, body, 0)

    chunk = 512
    for c0 in range(0, seq, chunk):
        rows = slice(c0, c0 + chunk)
        o_ref[0, rows, :] = (acc_ref[rows, :] / l_ref[rows, :]).astype(o_ref.dtype)


def _dilated_attention(qkv, slopes, nb, seq, n_heads):
    dh = HEAD_DIM
    kern = functools.partial(_dil_kernel, patterns=DIL_PATTERNS, seq=seq)
    blk = lambda off: pl.BlockSpec((1, seq, dh), lambda b, h, s: (b, 0, off + h))
    return pl.pallas_call(
        kern,
        grid_spec=pltpu.PrefetchScalarGridSpec(
            num_scalar_prefetch=1,
            grid=(nb, n_heads),
            in_specs=[blk(0), blk(n_heads), blk(2 * n_heads)],
            out_specs=pl.BlockSpec((1, seq, dh), lambda b, h, s: (b, 0, h)),
            scratch_shapes=[pltpu.VMEM((seq, dh), F32),
                            pltpu.VMEM((seq, LANES), F32),
                            pltpu.VMEM((seq, LANES), F32),
                            pltpu.VMEM((Q_BLOCK, 2 * Q_BLOCK), F32)]),
        out_shape=jax.ShapeDtypeStruct((nb, seq, n_heads * dh), BF16),
        compiler_params=_cparams(("arbitrary", "arbitrary")),
        name="dilated_attn",
    )(slopes, qkv, qkv, qkv)


def _fox_kernel(q_ref, k_ref, v_ref, c_ref, o_ref, m_scr, acc_scr, *, blk, hb):
    i = pl.program_id(2)
    dh = HEAD_DIM
    row = lax.broadcasted_iota(I32, (blk, blk), 0)
    col = lax.broadcasted_iota(I32, (blk, blk), 1)
    m_scr[...] = jnp.full(m_scr.shape, -jnp.inf, F32)
    acc_scr[...] = jnp.zeros(acc_scr.shape, F32)
    ones = jnp.ones((blk, dh), BF16)
    n_slab = blk // LANES

    def step(j, masked):
        rows = pl.ds(pl.multiple_of(j * blk, blk), blk)
        for h in range(hb):
            cols = slice(h * dh, (h + 1) * dh)
            s = _dot_nt(q_ref[0, :, cols], k_ref[0, rows, cols]) - c_ref[0, h, j]
            if masked:
                s = jnp.where(col <= row, s, -jnp.inf)
            m_old = m_scr[h]
            m_new = jnp.maximum(m_old, jnp.max(s, axis=1, keepdims=True))
            alpha = jnp.exp2(m_old - m_new)
            p = jnp.concatenate(
                [jnp.exp2(s[:, t * LANES:(t + 1) * LANES] - m_new) for t in range(n_slab)], axis=1)
            pv = _dot(p.astype(BF16), jnp.concatenate([v_ref[0, rows, cols], ones], axis=1))
            acc_scr[h] = jnp.concatenate([alpha, alpha], axis=1) * acc_scr[h] + pv
            m_scr[h] = m_new

    def body(j, carry):
        step(j, False)
        return carry

    lax.fori_loop(0, i, body, 0)
    step(i, True)
    for h in range(hb):
        acc = acc_scr[h]
        o_ref[0, :, h * dh:(h + 1) * dh] = (acc[:, :dh] / acc[:, dh:]).astype(o_ref.dtype)


def _fox_attention(qkv, cum, nb, seq, n_heads, blk):
    dh = HEAD_DIM
    hb = min(FOX_HEADS_PER_STEP, n_heads)
    assert n_heads % hb == 0
    ng = n_heads // hb
    w = hb * dh
    kern = functools.partial(_fox_kernel, blk=blk, hb=hb)
    return pl.pallas_call(
        kern,
        grid=(nb, ng, seq // blk),
        in_specs=[pl.BlockSpec((1, blk, w), lambda b, g, i: (b, i, g)),
                  pl.BlockSpec((1, seq, w), lambda b, g, i: (b, 0, ng + g)),
                  pl.BlockSpec((1, seq, w), lambda b, g, i: (b, 0, 2 * ng + g)),
                  pl.BlockSpec((1, hb, seq // blk, 1, blk), lambda b, g, i: (b, g, 0, 0, 0))],
        out_specs=pl.BlockSpec((1, blk, w), lambda b, g, i: (b, i, g)),
        out_shape=jax.ShapeDtypeStruct((nb, seq, n_heads * dh), BF16),
        scratch_shapes=[pltpu.VMEM((hb, blk, LANES), F32),
                        pltpu.VMEM((hb, blk, 2 * dh), F32)],
        compiler_params=_cparams(("arbitrary", "arbitrary", "arbitrary")),
        name="fox_attn",
    )(qkv, qkv, qkv, cum)


def _post_attn_kernel(od_ref, of_ref, x_ref, w_ref, ga_ref, scm_ref, shm_ref, g_ref, b_ref,
                      wrh_ref, wrl_ref, br_ref,
                      x1_ref, hp_ref, idx_ref, gate_ref, rank_ref, cnt_ref, carry_scr,
                      *, n_experts, top_k, dil_width):
    i = pl.program_id(0)

    @pl.when(i == 0)
    def _():
        carry_scr[...] = jnp.zeros_like(carry_scr)

    attn = _dot(od_ref[...], w_ref[:dil_width, :]) + _dot(of_ref[...], w_ref[dil_width:, :])
    x1 = _layer_norm(DEEPNORM_ALPHA * x_ref[...] + ga_ref[0] * attn, g_ref[...], b_ref[...])
    x1_ref[...] = x1
    h = x1 * (1.0 + scm_ref[0]) + shm_ref[0]
    h_hi, h_lo = _split_bf16(h)
    hp_ref[...] = h

    logits = _dot_split(h_hi, h_lo, wrh_ref[...], wrl_ref[...]) + br_ref[...]
    tm = logits.shape[0]
    lane = lax.broadcasted_iota(I32, (tm, LANES), 1)
    work = jnp.where(lane < n_experts, logits, -jnp.inf)
    vals, idxs = [], []
    for _ in range(top_k):
        mx = jnp.max(work, axis=1, keepdims=True)
        ix = jnp.min(jnp.where(work == mx, lane, LANES), axis=1, keepdims=True)
        vals.append(mx)
        idxs.append(ix)
        work = jnp.where(lane == ix, -jnp.inf, work)
    exps = [jnp.exp(v - vals[0]) for v in vals]
    denom = exps[0]
    for e in exps[1:]:
        denom = denom + e

    onehot = jnp.zeros((tm, LANES), F32)
    for ix in idxs:
        onehot = onehot + (lane == ix).astype(F32)
    r_i = lax.broadcasted_iota(I32, (tm, tm), 0)
    c_i = lax.broadcasted_iota(I32, (tm, tm), 1)
    tri = (c_i < r_i).astype(BF16)
    before = _dot(tri, onehot.astype(BF16)) + carry_scr[...]

    idx_out = jnp.zeros((tm, LANES), I32)
    gate_out = jnp.zeros((tm, LANES), F32)
    rank_out = jnp.zeros((tm, LANES), I32)
    for k in range(top_k):
        rank_k = jnp.sum(jnp.where(lane == idxs[k], before, 0.0), axis=1, keepdims=True)
        idx_out = jnp.where(lane == k, idxs[k], idx_out)
        gate_out = jnp.where(lane == k, exps[k] / denom, gate_out)
        rank_out = jnp.where(lane == k, rank_k.astype(I32), rank_out)
    idx_ref[...] = idx_out
    gate_ref[...] = gate_out
    rank_ref[...] = rank_out
    carry_scr[...] = carry_scr[...] + jnp.sum(onehot, axis=0, keepdims=True)
    cnt_ref[...] = jnp.broadcast_to(carry_scr[...], cnt_ref.shape)


def _post_attn(o_dil, o_fox, x2, w_out_bf, g_a, sc_m, sh_m, ln_g, ln_b, w_router, b_router, seq):
    t, d_model = x2.shape
    dil_width = o_dil.shape[1]
    n_experts = w_router.shape[1]
    tm = 256
    wr = jnp.pad(w_router, ((0, 0), (0, LANES - n_experts)))
    wr_hi, wr_lo = _split_bf16(wr)
    br = jnp.pad(b_router, (0, LANES - n_experts)).reshape(1, LANES)
    kern = functools.partial(_post_attn_kernel, n_experts=n_experts, top_k=TOP_K, dil_width=dil_width)
    row = lambda w: pl.BlockSpec((tm, w), lambda i: (i, 0))
    const = lambda r, w: pl.BlockSpec((r, w), lambda i: (0, 0))
    bvec = pl.BlockSpec((1, 1, d_model), lambda i: (i * tm // seq, 0, 0))
    return pl.pallas_call(
        kern,
        grid=(t // tm,),
        in_specs=[row(dil_width), row(o_fox.shape[1]), row(d_model), const(d_model, d_model),
                  bvec, bvec, bvec, const(1, d_model), const(1, d_model),
                  const(d_model, LANES), const(d_model, LANES), const(1, LANES)],
        out_specs=[row(d_model), row(d_model), row(LANES), row(LANES), row(LANES),
                   pl.BlockSpec((8, LANES), lambda i: (0, 0))],
        out_shape=[jax.ShapeDtypeStruct((t, d_model), F32),
                   jax.ShapeDtypeStruct((t, d_model), F32),
                   jax.ShapeDtypeStruct((t, LANES), I32),
                   jax.ShapeDtypeStruct((t, LANES), F32),
                   jax.ShapeDtypeStruct((t, LANES), I32),
                   jax.ShapeDtypeStruct((8, LANES), F32)],
        scratch_shapes=[pltpu.VMEM((1, LANES), F32)],
        compiler_params=_cparams(("arbitrary",)),
        name="post_attn",
    )(o_dil, o_fox, x2, w_out_bf, g_a, sc_m, sh_m, ln_g.reshape(1, -1), ln_b.reshape(1, -1),
      wr_hi, wr_lo, br)


def _scatter_kernel(dest_ref, hp_ref, xs_in_ref, xs_ref, sem, *, top_k):
    del xs_in_ref
    i = pl.program_id(0)
    tm = hp_ref.shape[0]

    def issue(t, carry):
        for k in range(top_k):
            d = dest_ref[(i * tm + t) * top_k + k]
            pltpu.make_async_copy(hp_ref.at[pl.ds(t, 1)], xs_ref.at[pl.ds(d, 1)], sem).start()
        return carry

    def drain(t, carry):
        for k in range(top_k):
            pltpu.make_async_copy(hp_ref.at[pl.ds(0, 1)], xs_ref.at[pl.ds(0, 1)], sem).wait()
        return carry

    lax.fori_loop(0, tm, issue, 0)
    lax.fori_loop(0, tm, drain, 0)


def _scatter_rows(dest_flat, hp, n_rows):
    t, w = hp.shape
    tm = 256
    kern = functools.partial(_scatter_kernel, top_k=TOP_K)
    return pl.pallas_call(
        kern,
        grid_spec=pltpu.PrefetchScalarGridSpec(
            num_scalar_prefetch=1,
            grid=(t // tm,),
            in_specs=[pl.BlockSpec((tm, w), lambda i, d: (i, 0)),
                      pl.BlockSpec(memory_space=pl.ANY)],
            out_specs=pl.BlockSpec(memory_space=pl.ANY),
            scratch_shapes=[pltpu.SemaphoreType.DMA(())]),
        out_shape=jax.ShapeDtypeStruct((n_rows, w), F32),
        input_output_aliases={2: 0},
        compiler_params=_cparams(("arbitrary",)),
        name="scatter_rows",
    )(dest_flat, hp, jnp.zeros((n_rows, w), F32))


def _moe_kernel(iexp_ref, irow_ref, inb_ref, used_ref, xs_ref, wg_ref, wu_ref, wd_ref, bg_ref, bu_ref,
                bd_ref, y_ref, xbuf, yacc, sem_in, sem_out, *, n_chunks):
    del iexp_ref
    it = pl.program_id(0)
    f = pl.program_id(1)
    nb = inb_ref[it]
    row0 = irow_ref[it]
    rb = ROW_BLOCK
    rows_of = lambda b: pl.ds(pl.multiple_of(b * rb, rb), rb)
    src_of = lambda b: pl.ds(pl.multiple_of(row0 + b * rb, rb), rb)

    @pl.when((f == 0) & (nb > 0))
    def _():
        def start(b, c):
            pltpu.make_async_copy(xs_ref.at[src_of(b)], xbuf.at[rows_of(b)], sem_in).start()
            return c

        def init(b, c):
            yacc[rows_of(b), :] = jnp.broadcast_to(bd_ref[0], (rb, yacc.shape[1]))
            return c

        def wait(b, c):
            pltpu.make_async_copy(xs_ref.at[src_of(b)], xbuf.at[rows_of(b)], sem_in).wait()
            return c

        lax.fori_loop(0, nb, start, 0)
        lax.fori_loop(0, nb, init, 0)
        lax.fori_loop(0, nb, wait, 0)

    wg = wg_ref[0].astype(BF16)
    wu = wu_ref[0].astype(BF16)
    wd = wd_ref[0].astype(BF16)
    bg = bg_ref[0]
    bu = bu_ref[0]

    def block(b, c):
        xb = xbuf[rows_of(b), :].astype(BF16)
        g = _dot(xb, wg) + bg
        u = _dot(xb, wu) + bu
        g = jnp.minimum(g, SWIGLU_LIMIT)
        u = jnp.clip(u, -SWIGLU_LIMIT, SWIGLU_LIMIT)
        act = (u + 1.0) * (g * jax.nn.sigmoid(SWIGLU_ALPHA * g))
        yacc[rows_of(b), :] += _dot(act.astype(BF16), wd)
        return c

    lax.fori_loop(0, nb, block, 0)

    @pl.when((f == n_chunks - 1) & (nb > 0))
    def _():
        def start(b, c):
            pltpu.make_async_copy(yacc.at[rows_of(b)], y_ref.at[src_of(b)], sem_out).start()
            return c

        def wait(b, c):
            pltpu.make_async_copy(yacc.at[rows_of(b)], y_ref.at[src_of(b)], sem_out).wait()
            return c

        lax.fori_loop(0, nb, start, 0)
        lax.fori_loop(0, nb, wait, 0)

    @pl.when((it == pl.num_programs(0) - 1) & (f == n_chunks - 1))
    def _():
        first = used_ref[0]
        last = y_ref.shape[0] // rb
        dst_of = lambda b: pl.ds(pl.multiple_of(b * rb, rb), rb)
        yacc[pl.ds(0, rb), :] = jnp.zeros((rb, yacc.shape[1]), yacc.dtype)

        def start(b, c):
            pltpu.make_async_copy(yacc.at[pl.ds(0, rb)], y_ref.at[dst_of(b)], sem_out).start()
            return c

        def wait(b, c):
            pltpu.make_async_copy(yacc.at[pl.ds(0, rb)], y_ref.at[dst_of(b)], sem_out).wait()
            return c

        lax.fori_loop(first, last, start, 0)
        lax.fori_loop(first, last, wait, 0)


def _moe_experts(item_exp, item_row, item_nblk, used_blocks, xs, w_gate_up, b_gate_up, w_down, b_down):
    n_rows = xs.shape[0]
    n_exp, d_model, two_f = w_gate_up.shape
    d_ff = two_f // 2
    fc = FF_CHUNK
    n_chunks = d_ff // fc
    n_items = item_exp.shape[0]
    kern = functools.partial(_moe_kernel, n_chunks=n_chunks)

    def fsel(it, f, nbk):
        return jnp.where(nbk[it] > 0, f, n_chunks - 1)

    return pl.pallas_call(
        kern,
        grid_spec=pltpu.PrefetchScalarGridSpec(
            num_scalar_prefetch=4,
            grid=(n_items, n_chunks),
            in_specs=[
                pl.BlockSpec(memory_space=pl.ANY),
                pl.BlockSpec((1, d_model, fc), lambda it, f, ex, ro, nbk, us: (ex[it], 0, fsel(it, f, nbk))),
                pl.BlockSpec((1, d_model, fc),
                             lambda it, f, ex, ro, nbk, us: (ex[it], 0, n_chunks + fsel(it, f, nbk))),
                pl.BlockSpec((1, fc, d_model), lambda it, f, ex, ro, nbk, us: (ex[it], fsel(it, f, nbk), 0)),
                pl.BlockSpec((1, 1, fc), lambda it, f, ex, ro, nbk, us: (ex[it], 0, fsel(it, f, nbk))),
                pl.BlockSpec((1, 1, fc),
                             lambda it, f, ex, ro, nbk, us: (ex[it], 0, n_chunks + fsel(it, f, nbk))),
                pl.BlockSpec((1, 1, d_model), lambda it, f, ex, ro, nbk, us: (ex[it], 0, 0)),
            ],
            out_specs=pl.BlockSpec(memory_space=pl.ANY),
            scratch_shapes=[pltpu.VMEM((ITEM_ROWS, d_model), F32),
                            pltpu.VMEM((ITEM_ROWS, d_model), F32),
                            pltpu.SemaphoreType.DMA(()),
                            pltpu.SemaphoreType.DMA(())]),
        out_shape=jax.ShapeDtypeStruct((n_rows, d_model), F32),
        compiler_params=_cparams(("arbitrary", "arbitrary")),
        name="moe_experts",
    )(item_exp, item_row, item_nblk, used_blocks, xs, w_gate_up, w_gate_up, w_down,
      b_gate_up.reshape(n_exp, 1, two_f), b_gate_up.reshape(n_exp, 1, two_f),
      b_down.reshape(n_exp, 1, d_model))


def _combine_kernel(dest_ref, y_ref, x1_ref, gate_ref, gm_ref, g_ref, b_ref, o_ref, ybuf, sem, *, top_k):
    i = pl.program_id(0)
    tm = x1_ref.shape[0]

    def issue(t, carry):
        for k in range(top_k):
            d = dest_ref[(i * tm + t) * top_k + k]
            pltpu.make_async_copy(y_ref.at[pl.ds(d, 1)], ybuf.at[k, pl.ds(t, 1)], sem).start()
        return carry

    def drain(t, carry):
        for k in range(top_k):
            pltpu.make_async_copy(y_ref.at[pl.ds(0, 1)], ybuf.at[k, pl.ds(0, 1)], sem).wait()
        return carry

    lax.fori_loop(0, tm, issue, 0)
    lax.fori_loop(0, tm, drain, 0)
    gates = gate_ref[...]
    ffn = gates[:, 0:1] * ybuf[0]
    for k in range(1, top_k):
        ffn = ffn + gates[:, k:k + 1] * ybuf[k]
    o_ref[...] = _layer_norm(DEEPNORM_ALPHA * x1_ref[...] + gm_ref[0] * ffn, g_ref[...], b_ref[...])


def _combine(dest_flat, y, x1, gates, g_m, ln_g, ln_b, seq):
    t, d_model = x1.shape
    tm = 128
    kern = functools.partial(_combine_kernel, top_k=TOP_K)
    return pl.pallas_call(
        kern,
        grid_spec=pltpu.PrefetchScalarGridSpec(
            num_scalar_prefetch=1,
            grid=(t // tm,),
            in_specs=[pl.BlockSpec(memory_space=pl.ANY),
                      pl.BlockSpec((tm, d_model), lambda i, d: (i, 0)),
                      pl.BlockSpec((tm, LANES), lambda i, d: (i, 0)),
                      pl.BlockSpec((1, 1, d_model), lambda i, d: (i * tm // seq, 0, 0)),
                      pl.BlockSpec((1, d_model), lambda i, d: (0, 0)),
                      pl.BlockSpec((1, d_model), lambda i, d: (0, 0))],
            out_specs=pl.BlockSpec((tm, d_model), lambda i, d: (i, 0)),
            scratch_shapes=[pltpu.VMEM((TOP_K, tm, d_model), F32),
                            pltpu.SemaphoreType.DMA(())]),
        out_shape=jax.ShapeDtypeStruct((t, d_model), F32),
        compiler_params=_cparams(("arbitrary",)),
        name="combine_ln2",
    )(dest_flat, y, x1, gates, g_m, ln_g.reshape(1, -1), ln_b.reshape(1, -1))


def _routing_plan(counts, idx, rank, n_tokens):
    n_exp = counts.shape[0]
    padded = (counts + ROW_BLOCK - 1) // ROW_BLOCK * ROW_BLOCK
    pad_end = jnp.cumsum(padded)
    pad_start = pad_end - padded
    dest = (pad_start[idx] + rank).reshape(-1).astype(I32)

    max_rows = (n_tokens * TOP_K + n_exp * (ROW_BLOCK - 1) + ROW_BLOCK - 1) // ROW_BLOCK * ROW_BLOCK
    max_items = n_exp + max_rows // ITEM_ROWS
    items_per = (padded + ITEM_ROWS - 1) // ITEM_ROWS
    items_end = jnp.cumsum(items_per)
    total = items_end[-1]
    it = jnp.arange(max_items, dtype=I32)
    last = jnp.maximum(total - 1, 0)
    it_c = jnp.minimum(it, last)
    exp_of = jnp.sum((items_end[None, :] <= it_c[:, None]).astype(I32), axis=1)
    exp_of = jnp.minimum(exp_of, n_exp - 1)
    local = it_c - (items_end[exp_of] - items_per[exp_of])
    row0 = pad_start[exp_of] + local * ITEM_ROWS
    rows = jnp.minimum(padded[exp_of] - local * ITEM_ROWS, ITEM_ROWS)
    nblk = jnp.where(it < total, rows // ROW_BLOCK, 0)
    used_blocks = (pad_end[-1:] // ROW_BLOCK).astype(I32)
    return dest, (exp_of.astype(I32), row0.astype(I32), nblk.astype(I32), used_blocks), max_rows


def kernel(x, c, w_ada, b_ada, w_in, b_forget, w_out, ln1_g, ln1_b, w_router, b_router,
           w_gate_up, b_gate_up, w_down, b_down, ln2_g, ln2_b):
    nb, seq, d_model = x.shape
    n_heads = d_model // HEAD_DIM
    n_dil = n_heads // 2
    n_fox = n_heads - n_dil
    dil_width, fox_width = n_dil * HEAD_DIM, n_fox * HEAD_DIM
    t = nb * seq
    layer = 0

    mod = _ada_mod(c, w_ada[layer], b_ada[layer])
    sh_a, sc_a, g_a, sh_m, sc_m, g_m = [m.reshape(nb, 1, d_model) for m in jnp.split(mod, 6, axis=-1)]

    x2 = x.reshape(t, d_model)
    qkv_dil, qkv_fox, f_logit = _in_proj(x2, sc_a, sh_a, w_in[layer], seq, dil_width, fox_width, n_fox)

    slopes = 2.0 ** (-8.0 * jnp.arange(1, n_dil + 1, dtype=F32) / n_dil)
    o_dil = _dilated_attention(qkv_dil.reshape(nb, seq, 3 * dil_width), slopes, nb, seq, n_dil)

    fox_blk = 256
    f_t = f_logit[:, :n_fox].reshape(nb, seq, n_fox).transpose(0, 2, 1)
    cum = _forget_cumsum(f_t, b_forget[layer]).reshape(nb, n_fox, seq // fox_blk, 1, fox_blk)
    o_fox = _fox_attention(qkv_fox.reshape(nb, seq, 3 * fox_width), cum, nb, seq, n_fox, fox_blk)

    x1, hp, idx, gates, rank, cnt = _post_attn(
        o_dil.reshape(t, dil_width), o_fox.reshape(t, fox_width), x2, w_out[layer].astype(BF16),
        g_a, sc_m, sh_m, ln1_g[layer], ln1_b[layer], w_router[layer], b_router[layer], seq)

    counts = cnt[0, :N_EXPERTS].astype(I32)
    dest, items, max_rows = _routing_plan(counts, idx[:, :TOP_K], rank[:, :TOP_K], t)
    xs = _scatter_rows(dest, hp, max_rows)
    y = _moe_experts(*items, xs, w_gate_up[layer], b_gate_up[layer], w_down[layer], b_down[layer])
    out = _combine(dest, y, x1, gates, g_m, ln2_g[layer], ln2_b[layer], seq)
    return out.reshape(nb, seq, d_model)
```

```python
import functools
import math

import jax
import jax.numpy as jnp
from jax import lax
from jax.experimental import pallas as pl
from jax.experimental.pallas import tpu as pltpu

F32 = jnp.float32
BF16 = jnp.bfloat16
I32 = jnp.int32
U32 = jnp.uint32

HEAD_DIM = 128
Q_BLOCK = 128
DIL_PATTERNS = ((128, 1), (512, 4), (2048, 16))
N_EXPERTS = 32
TOP_K = 4
SWIGLU_LIMIT = 7.0
SWIGLU_ALPHA = 1.702
LN_EPS = 1e-5
DEPTH = 1
DEEPNORM_ALPHA = (2.0 * DEPTH) ** 0.25

LOG2E = math.log2(math.e)
DIL_GROUP = 4
DEINT = 4
FOX_HEADS_PER_STEP = 4

LANES = 128
VMEM_LIMIT = 56 * 1024 * 1024

ROW_BLOCK = 256
ITEM_ROWS = 1536
FF_CHUNK = 256


def _cparams(sem):
    return pltpu.CompilerParams(dimension_semantics=sem, vmem_limit_bytes=VMEM_LIMIT)


def _dot(a, b):
    return jnp.dot(a, b, preferred_element_type=F32)


def _dot_nt(a, b):
    return lax.dot_general(a, b, (((1,), (1,)), ((), ())), preferred_element_type=F32)


def _split_bf16(x):
    hi = x.astype(BF16)
    lo = (x - hi.astype(F32)).astype(BF16)
    return hi, lo


def _dot_split(x_hi, x_lo, w_hi, w_lo):
    return _dot(x_hi, w_hi) + (_dot(x_hi, w_lo) + _dot(x_lo, w_hi))


def _layer_norm(y, g, b):
    mu = jnp.mean(y, axis=-1, keepdims=True)
    yc = y - mu
    var = jnp.mean(yc * yc, axis=-1, keepdims=True)
    return yc * lax.rsqrt(var + LN_EPS) * g + b


def _ada_kernel(ct_ref, w_ref, b_ref, o_ref):
    ct = ct_ref[...]
    s = ct * jax.nn.sigmoid(ct)
    d_model, nb = ct.shape
    tn = w_ref.shape[1]
    sb = [jnp.broadcast_to(s[:, b:b + 1], (d_model, LANES)) for b in range(nb)]
    for j in range(tn // LANES):
        cols = slice(j * LANES, (j + 1) * LANES)
        w = w_ref[:, cols]
        for b in range(nb):
            acc = jnp.sum(w * sb[b], axis=0, keepdims=True)
            o_ref[b:b + 1, cols] = acc + b_ref[:, cols]


def _ada_mod(c, w, b):
    nb, d_model = c.shape
    n = w.shape[1]
    tn = 1536 if n % 1536 == 0 else LANES
    return pl.pallas_call(
        _ada_kernel,
        grid=(n // tn,),
        in_specs=[pl.BlockSpec((d_model, nb), lambda j: (0, 0)),
                  pl.BlockSpec((d_model, tn), lambda j: (0, j)),
                  pl.BlockSpec((1, tn), lambda j: (0, j))],
        out_specs=pl.BlockSpec((nb, tn), lambda j: (0, j)),
        out_shape=jax.ShapeDtypeStruct((nb, n), F32),
        compiler_params=_cparams(("arbitrary",)),
        name="ada_mod",
    )(c.T, w, b.reshape(1, n))


def _inproj_kernel(x_ref, sc_ref, sh_ref, w_ref, wfh_ref, wfl_ref,
                   odil_ref, ofox_ref, of_ref, h_scr, *, n_dil_tiles, q_tiles, scale):
    j = pl.program_id(1)

    @pl.when(j == 0)
    def _():
        h = x_ref[...] * (1.0 + sc_ref[0]) + sh_ref[0]
        h_hi, h_lo = _split_bf16(h)
        h_scr[...] = h_hi
        of_ref[...] = _dot_split(h_hi, h_lo, wfh_ref[...], wfl_ref[...])

    acc = _dot(h_scr[...], w_ref[...])
    is_q = (j < q_tiles) | ((j >= n_dil_tiles) & (j < n_dil_tiles + q_tiles))
    acc = acc * jnp.where(is_q, scale, 1.0).astype(F32)

    @pl.when(j < n_dil_tiles)
    def _():
        odil_ref[...] = acc

    @pl.when(j >= n_dil_tiles)
    def _():
        ofox_ref[...] = acc.astype(BF16)


def _in_proj(x2, sc, sh, w_in, seq, dil_width, fox_width, n_fox_heads):
    t, d_model = x2.shape
    tm, tn = min(1024, seq), min(512, dil_width)
    n_dil_tiles = 3 * dil_width // tn
    n_fox_tiles = 3 * fox_width // tn
    q_tiles = dil_width // tn
    assert dil_width == fox_width and dil_width % tn == 0 and seq % tm == 0
    main = 3 * dil_width + 3 * fox_width
    wf = jnp.pad(w_in[:, main:], ((0, 0), (0, LANES - n_fox_heads)))
    wf_hi, wf_lo = _split_bf16(wf)
    kern = functools.partial(_inproj_kernel, n_dil_tiles=n_dil_tiles, q_tiles=q_tiles,
                             scale=LOG2E / math.sqrt(HEAD_DIM))
    bidx = lambda i, j: (i * tm // seq, 0, 0)
    return pl.pallas_call(
        kern,
        grid=(t // tm, n_dil_tiles + n_fox_tiles),
        in_specs=[pl.BlockSpec((tm, d_model), lambda i, j: (i, 0)),
                  pl.BlockSpec((1, 1, d_model), bidx),
                  pl.BlockSpec((1, 1, d_model), bidx),
                  pl.BlockSpec((d_model, tn), lambda i, j: (0, j)),
                  pl.BlockSpec((d_model, LANES), lambda i, j: (0, 0)),
                  pl.BlockSpec((d_model, LANES), lambda i, j: (0, 0))],
        out_specs=[pl.BlockSpec((tm, tn), lambda i, j: (i, jnp.minimum(j, n_dil_tiles - 1))),
                   pl.BlockSpec((tm, tn), lambda i, j: (i, jnp.maximum(j - n_dil_tiles, 0))),
                   pl.BlockSpec((tm, LANES), lambda i, j: (i, 0))],
        out_shape=[jax.ShapeDtypeStruct((t, 3 * dil_width), F32),
                   jax.ShapeDtypeStruct((t, 3 * fox_width), BF16),
                   jax.ShapeDtypeStruct((t, LANES), F32)],
        scratch_shapes=[pltpu.VMEM((tm, d_model), BF16)],
        compiler_params=_cparams(("arbitrary", "arbitrary")),
        name="in_proj",
    )(x2, sc, sh, w_in.astype(BF16), wf_hi, wf_lo)


def _cum_kernel(f_ref, b_ref, o_ref):
    z = f_ref[0] + b_ref[...]
    c = jnp.minimum(z, 0.0) - jnp.log1p(jnp.exp(-jnp.abs(z)))
    seq = z.shape[1]
    lane = lax.broadcasted_iota(I32, z.shape, 1)
    shift = 1
    while shift < seq:
        c = c + jnp.where(lane >= shift, pltpu.roll(c, shift, axis=1), 0.0)
        shift *= 2
    o_ref[0] = c * LOG2E


def _forget_cumsum(f_logit_t, b_forget):
    nb, hf, seq = f_logit_t.shape
    return pl.pallas_call(
        _cum_kernel,
        grid=(nb,),
        in_specs=[pl.BlockSpec((1, hf, seq), lambda b: (b, 0, 0)),
                  pl.BlockSpec((hf, 1), lambda b: (0, 0))],
        out_specs=pl.BlockSpec((1, hf, seq), lambda b: (b, 0, 0)),
        out_shape=jax.ShapeDtypeStruct((nb, hf, seq), F32),
        compiler_params=_cparams(("arbitrary",)),
        name="forget_cumsum",
    )(f_logit_t, b_forget.reshape(hf, 1))


def _dil_kernel(slope_ref, q_ref, k_ref, v_ref, o_ref, q4_ref, k4_ref, v4_ref, acc_ref, l_ref, m_ref,
                bias_ref, *, patterns, seq):
    slope = slope_ref[pl.program_id(1)]
    qb = Q_BLOCK
    row = lax.broadcasted_iota(I32, (qb, 2 * qb), 0)
    col = lax.broadcasted_iota(I32, (qb, 2 * qb), 1)
    delta = row + qb - col
    is_prev = col < qb
    neg_inf = F32(-jnp.inf)
    ones = jnp.ones((2 * qb, HEAD_DIM), BF16)

    sub = seq // DEINT
    chunk = min(256, sub)
    for r4 in range(DEINT):
        for c0 in range(0, sub, chunk):
            src = pl.ds(r4 + DEINT * c0, chunk, stride=DEINT)
            dst = pl.ds(r4 * sub + c0, chunk)
            q4_ref[dst, :] = q_ref[0, src, :]
            k4_ref[dst, :] = k_ref[0, src, :]
            v4_ref[dst, :] = v_ref[0, src, :]

    for pi, (window, dil) in enumerate(sorted(patterns, key=lambda wd: -wd[1])):
        reach = window // dil
        nblk = seq // dil // qb
        n_blocks = dil * nblk
        assert reach <= qb and nblk * qb * dil == seq and n_blocks % DIL_GROUP == 0
        sd = slope * (float(dil) * LOG2E)
        bias_ref[...] = jnp.where((delta >= 0) & (delta <= reach), -sd * delta.astype(F32), neg_inf)

        def block(idx, dil=dil, nblk=nblk, pi=pi):
            r = idx // nblk
            n = idx - r * nblk
            start = r + n * (qb * dil)
            cur = pl.ds(start, qb, stride=dil) if dil > 1 else pl.ds(pl.multiple_of(start, qb), qb)
            if dil % DEINT == 0:
                step = dil // DEINT
                first = (r % DEINT) * sub + r // DEINT
                lstart = first + n * (qb * step)
                lprev = jnp.maximum(lstart - qb * step, first)
                if step > 1:
                    src_cur, src_prev = pl.ds(lstart, qb, stride=step), pl.ds(lprev, qb, stride=step)
                else:
                    src_cur = pl.ds(pl.multiple_of(lstart, qb), qb)
                    src_prev = pl.ds(pl.multiple_of(lprev, qb), qb)
                q = q4_ref[src_cur, :].astype(BF16)
                kk = jnp.concatenate([k4_ref[src_prev, :], k4_ref[src_cur, :]], axis=0).astype(BF16)
                vv = jnp.concatenate([v4_ref[src_prev, :], v4_ref[src_cur, :]], axis=0).astype(BF16)
            else:
                pstart = jnp.maximum(start - qb * dil, r)
                prev = (pl.ds(pstart, qb, stride=dil) if dil > 1
                        else pl.ds(pl.multiple_of(pstart, qb), qb))
                q = q_ref[0, cur, :].astype(BF16)
                kk = jnp.concatenate([k_ref[0, prev, :], k_ref[0, cur, :]], axis=0).astype(BF16)
                vv = jnp.concatenate([v_ref[0, prev, :], v_ref[0, cur, :]], axis=0).astype(BF16)
            s = _dot_nt(q, kk) + bias_ref[...]
            s = jnp.where(is_prev & (n == 0), neg_inf, s)
            m_blk = jnp.max(s, axis=1, keepdims=True)
            p = jnp.exp2(s - m_blk)
            pv = _dot(p.astype(BF16), jnp.concatenate([vv, ones], axis=1))
            o_blk, l_blk = pv[:, :HEAD_DIM], pv[:, HEAD_DIM:]
            if pi == 0:
                m_ref[cur, :] = jnp.broadcast_to(m_blk, (qb, LANES))
                acc_ref[cur, :] = o_blk
                l_ref[cur, :] = l_blk
            else:
                m_old = m_ref[cur, :]
                m_new = jnp.maximum(m_old, m_blk)
                a_old = jnp.exp2(m_old - m_new)
                a_blk = jnp.exp2(m_blk - m_new)
                m_ref[cur, :] = m_new
                acc_ref[cur, :] = acc_ref[cur, :] * a_old + o_blk * a_blk
                l_ref[cur, :] = l_ref[cur, :] * a_old + l_blk * a_blk

        def body(g, carry, block=block):
            for u in range(DIL_GROUP):
                block(g * DIL_GROUP + u)
            return carry

        lax.fori_loop(0, n_blocks // DIL_GROUP, body, 0)

    chunk = 512
    for c0 in range(0, seq, chunk):
        rows = slice(c0, c0 + chunk)
        o_ref[0, rows, :] = (acc_ref[rows, :] / l_ref[rows, :]).astype(o_ref.dtype)


def _dilated_attention(qkv, slopes, nb, seq, n_heads):
    dh = HEAD_DIM
    kern = functools.partial(_dil_kernel, patterns=DIL_PATTERNS, seq=seq)
    blk = lambda off: pl.BlockSpec((1, seq, dh), lambda b, h, s: (b, 0, off + h))
    return pl.pallas_call(
        kern,
        grid_spec=pltpu.PrefetchScalarGridSpec(
            num_scalar_prefetch=1,
            grid=(nb, n_heads),
            in_specs=[blk(0), blk(n_heads), blk(2 * n_heads)],
            out_specs=pl.BlockSpec((1, seq, dh), lambda b, h, s: (b, 0, h)),
            scratch_shapes=[pltpu.VMEM((seq, dh), F32),
                            pltpu.VMEM((seq, dh), F32),
                            pltpu.VMEM((seq, dh), F32),
                            pltpu.VMEM((seq, dh), F32),
                            pltpu.VMEM((seq, LANES), F32),
                            pltpu.VMEM((seq, LANES), F32),
                            pltpu.VMEM((Q_BLOCK, 2 * Q_BLOCK), F32)]),
        out_shape=jax.ShapeDtypeStruct((nb, seq, n_heads * dh), BF16),
        compiler_params=_cparams(("arbitrary", "arbitrary")),
        name="dilated_attn",
    )(slopes, qkv, qkv, qkv)


def _fox_kernel(q_ref, k_ref, v_ref, c_ref, o_ref, m_scr, acc_scr, *, blk, hb):
    i = pl.program_id(2)
    dh = HEAD_DIM
    row = lax.broadcasted_iota(I32, (blk, blk), 0)
    col = lax.broadcasted_iota(I32, (blk, blk), 1)
    m_scr[...] = jnp.full(m_scr.shape, -jnp.inf, F32)
    acc_scr[...] = jnp.zeros(acc_scr.shape, F32)
    ones = jnp.ones((blk, dh), BF16)
    n_slab = blk // LANES

    def step(j, masked):
        rows = pl.ds(pl.multiple_of(j * blk, blk), blk)
        for h in range(hb):
            cols = slice(h * dh, (h + 1) * dh)
            s = _dot_nt(q_ref[0, :, cols], k_ref[0, rows, cols]) - c_ref[0, h, j]
            if masked:
                s = jnp.where(col <= row, s, -jnp.inf)
            m_old = m_scr[h]
            m_new = jnp.maximum(m_old, jnp.max(s, axis=1, keepdims=True))
            alpha = jnp.exp2(m_old - m_new)
            p = jnp.concatenate(
                [jnp.exp2(s[:, t * LANES:(t + 1) * LANES] - m_new) for t in range(n_slab)], axis=1)
            pv = _dot(p.astype(BF16), jnp.concatenate([v_ref[0, rows, cols], ones], axis=1))
            acc_scr[h] = jnp.concatenate([alpha, alpha], axis=1) * acc_scr[h] + pv
            m_scr[h] = m_new

    def body(j, carry):
        step(j, False)
        return carry

    lax.fori_loop(0, i, body, 0)
    step(i, True)
    for h in range(hb):
        acc = acc_scr[h]
        o_ref[0, :, h * dh:(h + 1) * dh] = (acc[:, :dh] / acc[:, dh:]).astype(o_ref.dtype)


def _fox_attention(qkv, cum, nb, seq, n_heads, blk):
    dh = HEAD_DIM
    hb = min(FOX_HEADS_PER_STEP, n_heads)
    assert n_heads % hb == 0
    ng = n_heads // hb
    w = hb * dh
    kern = functools.partial(_fox_kernel, blk=blk, hb=hb)
    return pl.pallas_call(
        kern,
        grid=(nb, ng, seq // blk),
        in_specs=[pl.BlockSpec((1, blk, w), lambda b, g, i: (b, i, g)),
                  pl.BlockSpec((1, seq, w), lambda b, g, i: (b, 0, ng + g)),
                  pl.BlockSpec((1, seq, w), lambda b, g, i: (b, 0, 2 * ng + g)),
                  pl.BlockSpec((1, hb, seq // blk, 1, blk), lambda b, g, i: (b, g, 0, 0, 0))],
        out_specs=pl.BlockSpec((1, blk, w), lambda b, g, i: (b, i, g)),
        out_shape=jax.ShapeDtypeStruct((nb, seq, n_heads * dh), BF16),
        scratch_shapes=[pltpu.VMEM((hb, blk, LANES), F32),
                        pltpu.VMEM((hb, blk, 2 * dh), F32)],
        compiler_params=_cparams(("arbitrary", "arbitrary", "arbitrary")),
        name="fox_attn",
    )(qkv, qkv, qkv, cum)


def _post_attn_kernel(od_ref, of_ref, x_ref, w_ref, ga_ref, scm_ref, shm_ref, g_ref, b_ref,
                      wrh_ref, wrl_ref, br_ref,
                      x1_ref, hp_ref, idx_ref, gate_ref, rank_ref, cnt_ref, carry_scr,
                      *, n_experts, top_k, dil_width):
    i = pl.program_id(0)

    @pl.when(i == 0)
    def _():
        carry_scr[...] = jnp.zeros_like(carry_scr)

    attn = _dot(od_ref[...], w_ref[:dil_width, :]) + _dot(of_ref[...], w_ref[dil_width:, :])
    x1 = _layer_norm(DEEPNORM_ALPHA * x_ref[...] + ga_ref[0] * attn, g_ref[...], b_ref[...])
    x1_ref[...] = x1
    h = x1 * (1.0 + scm_ref[0]) + shm_ref[0]
    h_hi, h_lo = _split_bf16(h)
    hp_ref[...] = h

    logits = _dot_split(h_hi, h_lo, wrh_ref[...], wrl_ref[...]) + br_ref[...]
    tm = logits.shape[0]
    lane = lax.broadcasted_iota(I32, (tm, LANES), 1)
    work = jnp.where(lane < n_experts, logits, -jnp.inf)
    vals, idxs = [], []
    for _ in range(top_k):
        mx = jnp.max(work, axis=1, keepdims=True)
        ix = jnp.min(jnp.where(work == mx, lane, LANES), axis=1, keepdims=True)
        vals.append(mx)
        idxs.append(ix)
        work = jnp.where(lane == ix, -jnp.inf, work)
    exps = [jnp.exp(v - vals[0]) for v in vals]
    denom = exps[0]
    for e in exps[1:]:
        denom = denom + e

    onehot = jnp.zeros((tm, LANES), F32)
    for ix in idxs:
        onehot = onehot + (lane == ix).astype(F32)
    r_i = lax.broadcasted_iota(I32, (tm, tm), 0)
    c_i = lax.broadcasted_iota(I32, (tm, tm), 1)
    tri = (c_i < r_i).astype(BF16)
    before = _dot(tri, onehot.astype(BF16)) + carry_scr[...]

    idx_out = jnp.zeros((tm, LANES), I32)
    gate_out = jnp.zeros((tm, LANES), F32)
    rank_out = jnp.zeros((tm, LANES), I32)
    for k in range(top_k):
        rank_k = jnp.sum(jnp.where(lane == idxs[k], before, 0.0), axis=1, keepdims=True)
        idx_out = jnp.where(lane == k, idxs[k], idx_out)
        gate_out = jnp.where(lane == k, exps[k] / denom, gate_out)
        rank_out = jnp.where(lane == k, rank_k.astype(I32), rank_out)
    idx_ref[...] = idx_out
    gate_ref[...] = gate_out
    rank_ref[...] = rank_out
    carry_scr[...] = carry_scr[...] + jnp.sum(onehot, axis=0, keepdims=True)
    cnt_ref[...] = jnp.broadcast_to(carry_scr[...], cnt_ref.shape)


def _post_attn(o_dil, o_fox, x2, w_out_bf, g_a, sc_m, sh_m, ln_g, ln_b, w_router, b_router, seq):
    t, d_model = x2.shape
    dil_width = o_dil.shape[1]
    n_experts = w_router.shape[1]
    tm = 256
    wr = jnp.pad(w_router, ((0, 0), (0, LANES - n_experts)))
    wr_hi, wr_lo = _split_bf16(wr)
    br = jnp.pad(b_router, (0, LANES - n_experts)).reshape(1, LANES)
    kern = functools.partial(_post_attn_kernel, n_experts=n_experts, top_k=TOP_K, dil_width=dil_width)
    row = lambda w: pl.BlockSpec((tm, w), lambda i: (i, 0))
    const = lambda r, w: pl.BlockSpec((r, w), lambda i: (0, 0))
    bvec = pl.BlockSpec((1, 1, d_model), lambda i: (i * tm // seq, 0, 0))
    return pl.pallas_call(
        kern,
        grid=(t // tm,),
        in_specs=[row(dil_width), row(o_fox.shape[1]), row(d_model), const(d_model, d_model),
                  bvec, bvec, bvec, const(1, d_model), const(1, d_model),
                  const(d_model, LANES), const(d_model, LANES), const(1, LANES)],
        out_specs=[row(d_model), row(d_model), row(LANES), row(LANES), row(LANES),
                   pl.BlockSpec((8, LANES), lambda i: (0, 0))],
        out_shape=[jax.ShapeDtypeStruct((t, d_model), F32),
                   jax.ShapeDtypeStruct((t, d_model), F32),
                   jax.ShapeDtypeStruct((t, LANES), I32),
                   jax.ShapeDtypeStruct((t, LANES), F32),
                   jax.ShapeDtypeStruct((t, LANES), I32),
                   jax.ShapeDtypeStruct((8, LANES), F32)],
        scratch_shapes=[pltpu.VMEM((1, LANES), F32)],
        compiler_params=_cparams(("arbitrary",)),
        name="post_attn",
    )(o_dil, o_fox, x2, w_out_bf, g_a, sc_m, sh_m, ln_g.reshape(1, -1), ln_b.reshape(1, -1),
      wr_hi, wr_lo, br)


def _scatter_kernel(dest_ref, fill_lo_ref, fill_hi_ref, used_ref, hp_ref, xs_ref, zbuf, sem, zsem,
                    *, top_k):
    i = pl.program_id(0)
    tm = hp_ref.shape[0]
    rb = ROW_BLOCK

    @pl.when(i == 0)
    def _():
        zbuf[...] = jnp.zeros(zbuf.shape, zbuf.dtype)

        def tails(wait):
            def row(r, c):
                copy = pltpu.make_async_copy(zbuf.at[pl.ds(0, 1)], xs_ref.at[pl.ds(r, 1)], zsem.at[0])
                copy.wait() if wait else copy.start()
                return c

            def expert(e, c):
                return lax.fori_loop(fill_lo_ref[e], fill_hi_ref[e], row, c)

            lax.fori_loop(0, fill_lo_ref.shape[0], expert, 0)

        def rest(wait):
            def blk(b, c):
                copy = pltpu.make_async_copy(zbuf, xs_ref.at[pl.ds(pl.multiple_of(b * rb, rb), rb)],
                                             zsem.at[1])
                copy.wait() if wait else copy.start()
                return c

            lax.fori_loop(used_ref[0], xs_ref.shape[0] // rb, blk, 0)

        tails(False)
        rest(False)
        tails(True)
        rest(True)

    base = i * (tm * top_k)
    for t in range(tm):
        for k in range(top_k):
            d = dest_ref[base + (t * top_k + k)]
            pltpu.make_async_copy(hp_ref.at[pl.ds(t, 1)], xs_ref.at[pl.ds(d, 1)], sem).start()

    def drain(t, carry):
        for k in range(top_k):
            pltpu.make_async_copy(hp_ref.at[pl.ds(0, 1)], xs_ref.at[pl.ds(0, 1)], sem).wait()
        return carry

    lax.fori_loop(0, tm, drain, 0, unroll=8)


def _scatter_rows(dest_flat, fill_lo, fill_hi, used_blocks, hp, n_rows):
    t, w = hp.shape
    tm = 256
    kern = functools.partial(_scatter_kernel, top_k=TOP_K)
    return pl.pallas_call(
        kern,
        grid_spec=pltpu.PrefetchScalarGridSpec(
            num_scalar_prefetch=4,
            grid=(t // tm,),
            in_specs=[pl.BlockSpec((tm, w), lambda i, *_: (i, 0))],
            out_specs=pl.BlockSpec(memory_space=pl.ANY),
            scratch_shapes=[pltpu.VMEM((ROW_BLOCK, w), F32),
                            pltpu.SemaphoreType.DMA(()),
                            pltpu.SemaphoreType.DMA((2,))]),
        out_shape=jax.ShapeDtypeStruct((n_rows, w), F32),
        compiler_params=_cparams(("arbitrary",)),
        name="scatter_rows",
    )(dest_flat, fill_lo, fill_hi, used_blocks, hp)


def _moe_kernel(iexp_ref, irow_ref, inb_ref, used_ref, xs_ref, wg_ref, wu_ref, wd_ref, bg_ref, bu_ref,
                bd_ref, y_ref, xbuf, yacc, sem_in, sem_out, *, n_chunks):
    del iexp_ref
    it = pl.program_id(0)
    f = pl.program_id(1)
    nb = inb_ref[it]
    row0 = irow_ref[it]
    rb = ROW_BLOCK
    rows_of = lambda b: pl.ds(pl.multiple_of(b * rb, rb), rb)
    src_of = lambda b: pl.ds(pl.multiple_of(row0 + b * rb, rb), rb)

    @pl.when((f == 0) & (nb > 0))
    def _():
        def start(b, c):
            pltpu.make_async_copy(xs_ref.at[src_of(b)], xbuf.at[rows_of(b)], sem_in).start()
            return c

        def init(b, c):
            yacc[rows_of(b), :] = jnp.broadcast_to(bd_ref[0], (rb, yacc.shape[1]))
            return c

        def wait(b, c):
            pltpu.make_async_copy(xs_ref.at[src_of(b)], xbuf.at[rows_of(b)], sem_in).wait()
            return c

        lax.fori_loop(0, nb, start, 0)
        lax.fori_loop(0, nb, init, 0)
        lax.fori_loop(0, nb, wait, 0)

    wg = wg_ref[0].astype(BF16)
    wu = wu_ref[0].astype(BF16)
    wd = wd_ref[0].astype(BF16)
    bg = bg_ref[0]
    bu = bu_ref[0]

    def block(b):
        xb = xbuf[rows_of(b), :].astype(BF16)
        g = _dot(xb, wg) + bg
        u = _dot(xb, wu) + bu
        g = jnp.minimum(g, SWIGLU_LIMIT)
        u = jnp.clip(u, -SWIGLU_LIMIT, SWIGLU_LIMIT)
        act = (u + 1.0) * (g * jax.nn.sigmoid(SWIGLU_ALPHA * g))
        yacc[rows_of(b), :] += _dot(act.astype(BF16), wd)

    def pair(t, c):
        block(2 * t)
        block(2 * t + 1)
        return c

    lax.fori_loop(0, lax.shift_right_logical(nb, 1), pair, 0)

    @pl.when((nb & 1) == 1)
    def _():
        block(nb - 1)

    @pl.when((f == n_chunks - 1) & (nb > 0))
    def _():
        def start(b, c):
            pltpu.make_async_copy(yacc.at[rows_of(b)], y_ref.at[src_of(b)], sem_out).start()
            return c

        def wait(b, c):
            pltpu.make_async_copy(yacc.at[rows_of(b)], y_ref.at[src_of(b)], sem_out).wait()
            return c

        lax.fori_loop(0, nb, start, 0)
        lax.fori_loop(0, nb, wait, 0)

    @pl.when((it == pl.num_programs(0) - 1) & (f == n_chunks - 1))
    def _():
        first = used_ref[0]
        last = y_ref.shape[0] // rb
        dst_of = lambda b: pl.ds(pl.multiple_of(b * rb, rb), rb)
        yacc[pl.ds(0, rb), :] = jnp.zeros((rb, yacc.shape[1]), yacc.dtype)

        def start(b, c):
            pltpu.make_async_copy(yacc.at[pl.ds(0, rb)], y_ref.at[dst_of(b)], sem_out).start()
            return c

        def wait(b, c):
            pltpu.make_async_copy(yacc.at[pl.ds(0, rb)], y_ref.at[dst_of(b)], sem_out).wait()
            return c

        lax.fori_loop(first, last, start, 0)
        lax.fori_loop(first, last, wait, 0)


def _moe_experts(item_exp, item_row, item_nblk, used_blocks, xs, w_gate_up, b_gate_up, w_down, b_down):
    n_rows = xs.shape[0]
    n_exp, d_model, two_f = w_gate_up.shape
    d_ff = two_f // 2
    fc = FF_CHUNK
    n_chunks = d_ff // fc
    n_items = item_exp.shape[0]
    kern = functools.partial(_moe_kernel, n_chunks=n_chunks)

    def fsel(it, f, nbk):
        return jnp.where(nbk[it] > 0, f, n_chunks - 1)

    return pl.pallas_call(
        kern,
        grid_spec=pltpu.PrefetchScalarGridSpec(
            num_scalar_prefetch=4,
            grid=(n_items, n_chunks),
            in_specs=[
                pl.BlockSpec(memory_space=pl.ANY),
                pl.BlockSpec((1, d_model, fc), lambda it, f, ex, ro, nbk, us: (ex[it], 0, fsel(it, f, nbk))),
                pl.BlockSpec((1, d_model, fc),
                             lambda it, f, ex, ro, nbk, us: (ex[it], 0, n_chunks + fsel(it, f, nbk))),
                pl.BlockSpec((1, fc, d_model), lambda it, f, ex, ro, nbk, us: (ex[it], fsel(it, f, nbk), 0)),
                pl.BlockSpec((1, 1, fc), lambda it, f, ex, ro, nbk, us: (ex[it], 0, fsel(it, f, nbk))),
                pl.BlockSpec((1, 1, fc),
                             lambda it, f, ex, ro, nbk, us: (ex[it], 0, n_chunks + fsel(it, f, nbk))),
                pl.BlockSpec((1, 1, d_model), lambda it, f, ex, ro, nbk, us: (ex[it], 0, 0)),
            ],
            out_specs=pl.BlockSpec(memory_space=pl.ANY),
            scratch_shapes=[pltpu.VMEM((ITEM_ROWS, d_model), F32),
                            pltpu.VMEM((ITEM_ROWS, d_model), F32),
                            pltpu.SemaphoreType.DMA(()),
                            pltpu.SemaphoreType.DMA(())]),
        out_shape=jax.ShapeDtypeStruct((n_rows, d_model), F32),
        compiler_params=_cparams(("arbitrary", "arbitrary")),
        name="moe_experts",
    )(item_exp, item_row, item_nblk, used_blocks, xs, w_gate_up, w_gate_up, w_down,
      b_gate_up.reshape(n_exp, 1, two_f), b_gate_up.reshape(n_exp, 1, two_f),
      b_down.reshape(n_exp, 1, d_model))


def _combine_kernel(dest_ref, y_ref, x1_ref, gate_ref, gm_ref, g_ref, b_ref, o_ref, ybuf, sem, *, top_k):
    i = pl.program_id(0)
    n_steps = pl.num_programs(0)
    tm = x1_ref.shape[0]
    slot = i % 2

    def gather(step, slot_):
        base = step * (tm * top_k)
        for t in range(tm):
            for k in range(top_k):
                d = dest_ref[base + (t * top_k + k)]
                pltpu.make_async_copy(y_ref.at[pl.ds(d, 1)], ybuf.at[slot_, pl.ds(k * tm + t, 1)],
                                      sem.at[slot_]).start()

    @pl.when(i == 0)
    def _():
        gather(0, 0)

    @pl.when(i + 1 < n_steps)
    def _():
        gather(i + 1, 1 - slot)

    pltpu.make_async_copy(y_ref.at[pl.ds(0, top_k * tm)], ybuf.at[slot], sem.at[slot]).wait()
    gates = gate_ref[...]
    ffn = gates[:, 0:1] * ybuf[slot, pl.ds(0, tm), :]
    for k in range(1, top_k):
        ffn = ffn + gates[:, k:k + 1] * ybuf[slot, pl.ds(k * tm, tm), :]
    o_ref[...] = _layer_norm(DEEPNORM_ALPHA * x1_ref[...] + gm_ref[0] * ffn, g_ref[...], b_ref[...])


def _combine(dest_flat, y, x1, gates, g_m, ln_g, ln_b, seq):
    t, d_model = x1.shape
    tm = 128
    kern = functools.partial(_combine_kernel, top_k=TOP_K)
    return pl.pallas_call(
        kern,
        grid_spec=pltpu.PrefetchScalarGridSpec(
            num_scalar_prefetch=1,
            grid=(t // tm,),
            in_specs=[pl.BlockSpec(memory_space=pl.ANY),
                      pl.BlockSpec((tm, d_model), lambda i, d: (i, 0)),
                      pl.BlockSpec((tm, LANES), lambda i, d: (i, 0)),
                      pl.BlockSpec((1, 1, d_model), lambda i, d: (i * tm // seq, 0, 0)),
                      pl.BlockSpec((1, d_model), lambda i, d: (0, 0)),
                      pl.BlockSpec((1, d_model), lambda i, d: (0, 0))],
            out_specs=pl.BlockSpec((tm, d_model), lambda i, d: (i, 0)),
            scratch_shapes=[pltpu.VMEM((2, TOP_K * tm, d_model), F32),
                            pltpu.SemaphoreType.DMA((2,))]),
        out_shape=jax.ShapeDtypeStruct((t, d_model), F32),
        compiler_params=_cparams(("arbitrary",)),
        name="combine_ln2",
    )(dest_flat, y, x1, gates, g_m, ln_g.reshape(1, -1), ln_b.reshape(1, -1))


def _routing_plan(counts, idx, rank, n_tokens):
    n_exp = counts.shape[0]
    padded = (counts + ROW_BLOCK - 1) // ROW_BLOCK * ROW_BLOCK
    pad_end = jnp.cumsum(padded)
    pad_start = pad_end - padded
    dest = (pad_start[idx] + rank).reshape(-1).astype(I32)

    max_rows = (n_tokens * TOP_K + n_exp * (ROW_BLOCK - 1) + ROW_BLOCK - 1) // ROW_BLOCK * ROW_BLOCK
    max_items = n_exp + max_rows // ITEM_ROWS
    items_per = (padded + ITEM_ROWS - 1) // ITEM_ROWS
    items_end = jnp.cumsum(items_per)
    total = items_end[-1]
    it = jnp.arange(max_items, dtype=I32)
    last = jnp.maximum(total - 1, 0)
    it_c = jnp.minimum(it, last)
    exp_of = jnp.sum((items_end[None, :] <= it_c[:, None]).astype(I32), axis=1)
    exp_of = jnp.minimum(exp_of, n_exp - 1)
    local = it_c - (items_end[exp_of] - items_per[exp_of])
    row0 = pad_start[exp_of] + local * ITEM_ROWS
    rows = jnp.minimum(padded[exp_of] - local * ITEM_ROWS, ITEM_ROWS)
    nblk = jnp.where(it < total, rows // ROW_BLOCK, 0)
    used_blocks = (pad_end[-1:] // ROW_BLOCK).astype(I32)
    fill = ((pad_start + counts).astype(I32), pad_end.astype(I32))
    return dest, fill, (exp_of.astype(I32), row0.astype(I32), nblk.astype(I32), used_blocks), max_rows


def kernel(x, c, w_ada, b_ada, w_in, b_forget, w_out, ln1_g, ln1_b, w_router, b_router,
           w_gate_up, b_gate_up, w_down, b_down, ln2_g, ln2_b):
    nb, seq, d_model = x.shape
    n_heads = d_model // HEAD_DIM
    n_dil = n_heads // 2
    n_fox = n_heads - n_dil
    dil_width, fox_width = n_dil * HEAD_DIM, n_fox * HEAD_DIM
    t = nb * seq
    layer = 0

    mod = _ada_mod(c, w_ada[layer], b_ada[layer])
    sh_a, sc_a, g_a, sh_m, sc_m, g_m = [m.reshape(nb, 1, d_model) for m in jnp.split(mod, 6, axis=-1)]

    x2 = x.reshape(t, d_model)
    qkv_dil, qkv_fox, f_logit = _in_proj(x2, sc_a, sh_a, w_in[layer], seq, dil_width, fox_width, n_fox)

    slopes = 2.0 ** (-8.0 * jnp.arange(1, n_dil + 1, dtype=F32) / n_dil)
    o_dil = _dilated_attention(qkv_dil.reshape(nb, seq, 3 * dil_width), slopes, nb, seq, n_dil)

    fox_blk = 256
    f_t = f_logit[:, :n_fox].reshape(nb, seq, n_fox).transpose(0, 2, 1)
    cum = _forget_cumsum(f_t, b_forget[layer]).reshape(nb, n_fox, seq // fox_blk, 1, fox_blk)
    o_fox = _fox_attention(qkv_fox.reshape(nb, seq, 3 * fox_width), cum, nb, seq, n_fox, fox_blk)

    x1, hp, idx, gates, rank, cnt = _post_attn(
        o_dil.reshape(t, dil_width), o_fox.reshape(t, fox_width), x2, w_out[layer].astype(BF16),
        g_a, sc_m, sh_m, ln1_g[layer], ln1_b[layer], w_router[layer], b_router[layer], seq)

    counts = cnt[0, :N_EXPERTS].astype(I32)
    dest, fill, items, max_rows = _routing_plan(counts, idx[:, :TOP_K], rank[:, :TOP_K], t)
    xs = _scatter_rows(dest, *fill, items[3], hp, max_rows)
    y = _moe_experts(*items, xs, w_gate_up[layer], b_gate_up[layer], w_down[layer], b_down[layer])
    out = _combine(dest, y, x1, gates, g_m, ln2_g[layer], ln2_b[layer], seq)
    return out.reshape(nb, seq, d_model)
```

```python
import functools
import math

import jax
import jax.numpy as jnp
from jax import lax
from jax.experimental import pallas as pl
from jax.experimental.pallas import tpu as pltpu

F32 = jnp.float32
BF16 = jnp.bfloat16
I32 = jnp.int32
U32 = jnp.uint32

HEAD_DIM = 128
Q_BLOCK = 128
DIL_PATTERNS = ((128, 1), (512, 4), (2048, 16))
N_EXPERTS = 32
TOP_K = 4
SWIGLU_LIMIT = 7.0
SWIGLU_ALPHA = 1.702
LN_EPS = 1e-5
DEPTH = 1
DEEPNORM_ALPHA = (2.0 * DEPTH) ** 0.25

LOG2E = math.log2(math.e)
DIL_GROUP = 4
DEINT = 4
FOX_HEADS_PER_STEP = 4

LANES = 128
VMEM_LIMIT = 56 * 1024 * 1024

ROW_BLOCK = 128
ITEM_ROWS = 1536
FF_CHUNK = 256


def _cparams(sem):
    return pltpu.CompilerParams(dimension_semantics=sem, vmem_limit_bytes=VMEM_LIMIT)


def _dot(a, b):
    return jnp.dot(a, b, preferred_element_type=F32)


def _dot_nt(a, b):
    return lax.dot_general(a, b, (((1,), (1,)), ((), ())), preferred_element_type=F32)


def _split_bf16(x):
    hi = x.astype(BF16)
    lo = (x - hi.astype(F32)).astype(BF16)
    return hi, lo


def _dot_split(x_hi, x_lo, w_hi, w_lo):
    return _dot(x_hi, w_hi) + (_dot(x_hi, w_lo) + _dot(x_lo, w_hi))


def _layer_norm(y, g, b):
    mu = jnp.mean(y, axis=-1, keepdims=True)
    yc = y - mu
    var = jnp.mean(yc * yc, axis=-1, keepdims=True)
    return yc * lax.rsqrt(var + LN_EPS) * g + b


def _ada_kernel(ct_ref, w_ref, b_ref, o_ref):
    ct = ct_ref[...]
    s = ct * jax.nn.sigmoid(ct)
    d_model, nb = ct.shape
    tn = w_ref.shape[1]
    sb = [jnp.broadcast_to(s[:, b:b + 1], (d_model, LANES)) for b in range(nb)]
    for j in range(tn // LANES):
        cols = slice(j * LANES, (j + 1) * LANES)
        w = w_ref[:, cols]
        for b in range(nb):
            acc = jnp.sum(w * sb[b], axis=0, keepdims=True)
            o_ref[b:b + 1, cols] = acc + b_ref[:, cols]


def _ada_mod(c, w, b):
    nb, d_model = c.shape
    n = w.shape[1]
    tn = 1536 if n % 1536 == 0 else LANES
    return pl.pallas_call(
        _ada_kernel,
        grid=(n // tn,),
        in_specs=[pl.BlockSpec((d_model, nb), lambda j: (0, 0)),
                  pl.BlockSpec((d_model, tn), lambda j: (0, j)),
                  pl.BlockSpec((1, tn), lambda j: (0, j))],
        out_specs=pl.BlockSpec((nb, tn), lambda j: (0, j)),
        out_shape=jax.ShapeDtypeStruct((nb, n), F32),
        compiler_params=_cparams(("arbitrary",)),
        name="ada_mod",
    )(c.T, w, b.reshape(1, n))


def _inproj_kernel(x_ref, sc_ref, sh_ref, w_ref, wfh_ref, wfl_ref,
                   odil_ref, ofox_ref, of_ref, h_scr, *, n_dil_tiles, q_tiles, scale):
    j = pl.program_id(1)

    @pl.when(j == 0)
    def _():
        h = x_ref[...] * (1.0 + sc_ref[0]) + sh_ref[0]
        h_hi, h_lo = _split_bf16(h)
        h_scr[...] = h_hi
        of_ref[...] = _dot_split(h_hi, h_lo, wfh_ref[...], wfl_ref[...])

    acc = _dot(h_scr[...], w_ref[...])
    is_q = (j < q_tiles) | ((j >= n_dil_tiles) & (j < n_dil_tiles + q_tiles))
    acc = acc * jnp.where(is_q, scale, 1.0).astype(F32)

    @pl.when(j < n_dil_tiles)
    def _():
        odil_ref[...] = acc

    @pl.when(j >= n_dil_tiles)
    def _():
        ofox_ref[...] = acc.astype(BF16)


def _in_proj(x2, sc, sh, w_in, seq, dil_width, fox_width, n_fox_heads):
    t, d_model = x2.shape
    tm, tn = min(1024, seq), min(512, dil_width)
    n_dil_tiles = 3 * dil_width // tn
    n_fox_tiles = 3 * fox_width // tn
    q_tiles = dil_width // tn
    assert dil_width == fox_width and dil_width % tn == 0 and seq % tm == 0
    main = 3 * dil_width + 3 * fox_width
    wf = jnp.pad(w_in[:, main:], ((0, 0), (0, LANES - n_fox_heads)))
    wf_hi, wf_lo = _split_bf16(wf)
    kern = functools.partial(_inproj_kernel, n_dil_tiles=n_dil_tiles, q_tiles=q_tiles,
                             scale=LOG2E / math.sqrt(HEAD_DIM))
    bidx = lambda i, j: (i * tm // seq, 0, 0)
    return pl.pallas_call(
        kern,
        grid=(t // tm, n_dil_tiles + n_fox_tiles),
        in_specs=[pl.BlockSpec((tm, d_model), lambda i, j: (i, 0)),
                  pl.BlockSpec((1, 1, d_model), bidx),
                  pl.BlockSpec((1, 1, d_model), bidx),
                  pl.BlockSpec((d_model, tn), lambda i, j: (0, j)),
                  pl.BlockSpec((d_model, LANES), lambda i, j: (0, 0)),
                  pl.BlockSpec((d_model, LANES), lambda i, j: (0, 0))],
        out_specs=[pl.BlockSpec((tm, tn), lambda i, j: (i, jnp.minimum(j, n_dil_tiles - 1))),
                   pl.BlockSpec((tm, tn), lambda i, j: (i, jnp.maximum(j - n_dil_tiles, 0))),
                   pl.BlockSpec((tm, LANES), lambda i, j: (i, 0))],
        out_shape=[jax.ShapeDtypeStruct((t, 3 * dil_width), F32),
                   jax.ShapeDtypeStruct((t, 3 * fox_width), BF16),
                   jax.ShapeDtypeStruct((t, LANES), F32)],
        scratch_shapes=[pltpu.VMEM((tm, d_model), BF16)],
        compiler_params=_cparams(("arbitrary", "arbitrary")),
        name="in_proj",
    )(x2, sc, sh, w_in.astype(BF16), wf_hi, wf_lo)


def _cum_kernel(f_ref, b_ref, o_ref):
    hf = o_ref.shape[1]
    z = f_ref[...].T[:hf, :] + b_ref[...]
    c = jnp.minimum(z, 0.0) - jnp.log1p(jnp.exp(-jnp.abs(z)))
    seq = z.shape[1]
    lane = lax.broadcasted_iota(I32, z.shape, 1)
    shift = 1
    while shift < seq:
        c = c + jnp.where(lane >= shift, pltpu.roll(c, shift, axis=1), 0.0)
        shift *= 2
    o_ref[0] = c * LOG2E


def _forget_cumsum(f_logit, b_forget, nb, seq):
    hf = b_forget.shape[0]
    return pl.pallas_call(
        _cum_kernel,
        grid=(nb,),
        in_specs=[pl.BlockSpec((seq, LANES), lambda b: (b, 0)),
                  pl.BlockSpec((hf, 1), lambda b: (0, 0))],
        out_specs=pl.BlockSpec((1, hf, seq), lambda b: (b, 0, 0)),
        out_shape=jax.ShapeDtypeStruct((nb, hf, seq), F32),
        compiler_params=_cparams(("arbitrary",)),
        name="forget_cumsum",
    )(f_logit, b_forget.reshape(hf, 1))


def _dil_kernel(slope_ref, q_ref, k_ref, v_ref, o_ref, q4_ref, k4_ref, v4_ref, acc_ref, l_ref, m_ref,
                bias_ref, *, patterns, seq):
    slope = slope_ref[pl.program_id(1)]
    qb = Q_BLOCK
    row = lax.broadcasted_iota(I32, (qb, 2 * qb), 0)
    col = lax.broadcasted_iota(I32, (qb, 2 * qb), 1)
    delta = row + qb - col
    is_prev = col < qb
    neg_inf = F32(-jnp.inf)
    ones = jnp.ones((2 * qb, HEAD_DIM), BF16)

    sub = seq // DEINT
    chunk = min(256, sub)
    for r4 in range(DEINT):
        for c0 in range(0, sub, chunk):
            src = pl.ds(r4 + DEINT * c0, chunk, stride=DEINT)
            dst = pl.ds(r4 * sub + c0, chunk)
            q4_ref[dst, :] = q_ref[0, src, :]
            k4_ref[dst, :] = k_ref[0, src, :]
            v4_ref[dst, :] = v_ref[0, src, :]

    for pi, (window, dil) in enumerate(sorted(patterns, key=lambda wd: -wd[1])):
        reach = window // dil
        nblk = seq // dil // qb
        n_blocks = dil * nblk
        assert reach <= qb and nblk * qb * dil == seq and n_blocks % DIL_GROUP == 0
        sd = slope * (float(dil) * LOG2E)
        bias_ref[...] = jnp.where((delta >= 0) & (delta <= reach), -sd * delta.astype(F32), neg_inf)

        def block(idx, dil=dil, nblk=nblk, pi=pi):
            r = idx // nblk
            n = idx - r * nblk
            start = r + n * (qb * dil)
            cur = pl.ds(start, qb, stride=dil) if dil > 1 else pl.ds(pl.multiple_of(start, qb), qb)
            if dil % DEINT == 0:
                step = dil // DEINT
                first = (r % DEINT) * sub + r // DEINT
                lstart = first + n * (qb * step)
                lprev = jnp.maximum(lstart - qb * step, first)
                if step > 1:
                    src_cur, src_prev = pl.ds(lstart, qb, stride=step), pl.ds(lprev, qb, stride=step)
                else:
                    src_cur = pl.ds(pl.multiple_of(lstart, qb), qb)
                    src_prev = pl.ds(pl.multiple_of(lprev, qb), qb)
                q = q4_ref[src_cur, :].astype(BF16)
                kk = jnp.concatenate([k4_ref[src_prev, :], k4_ref[src_cur, :]], axis=0).astype(BF16)
                vv = jnp.concatenate([v4_ref[src_prev, :], v4_ref[src_cur, :]], axis=0).astype(BF16)
            else:
                pstart = jnp.maximum(start - qb * dil, r)
                prev = (pl.ds(pstart, qb, stride=dil) if dil > 1
                        else pl.ds(pl.multiple_of(pstart, qb), qb))
                q = q_ref[0, cur, :].astype(BF16)
                kk = jnp.concatenate([k_ref[0, prev, :], k_ref[0, cur, :]], axis=0).astype(BF16)
                vv = jnp.concatenate([v_ref[0, prev, :], v_ref[0, cur, :]], axis=0).astype(BF16)
            s = _dot_nt(q, kk) + bias_ref[...]
            s = jnp.where(is_prev & (n == 0), neg_inf, s)
            m_blk = jnp.max(s, axis=1, keepdims=True)
            p = jnp.exp2(s - m_blk)
            pv = _dot(p.astype(BF16), jnp.concatenate([vv, ones], axis=1))
            o_blk, l_blk = pv[:, :HEAD_DIM], pv[:, HEAD_DIM:]
            if pi == 0:
                m_ref[cur, :] = jnp.broadcast_to(m_blk, (qb, LANES))
                acc_ref[cur, :] = o_blk
                l_ref[cur, :] = l_blk
            else:
                m_old = m_ref[cur, :]
                m_new = jnp.maximum(m_old, m_blk)
                a_old = jnp.exp2(m_old - m_new)
                a_blk = jnp.exp2(m_blk - m_new)
                m_ref[cur, :] = m_new
                acc_ref[cur, :] = acc_ref[cur, :] * a_old + o_blk * a_blk
                l_ref[cur, :] = l_ref[cur, :] * a_old + l_blk * a_blk

        def body(g, carry, block=block):
            for u in range(DIL_GROUP):
                block(g * DIL_GROUP + u)
            return carry

        lax.fori_loop(0, n_blocks // DIL_GROUP, body, 0)

    chunk = 512
    for c0 in range(0, seq, chunk):
        rows = slice(c0, c0 + chunk)
        o_ref[0, rows, :] = (acc_ref[rows, :] / l_ref[rows, :]).astype(o_ref.dtype)


def _dilated_attention(qkv, slopes, nb, seq, n_heads):
    dh = HEAD_DIM
    kern = functools.partial(_dil_kernel, patterns=DIL_PATTERNS, seq=seq)
    blk = lambda off: pl.BlockSpec((1, seq, dh), lambda b, h, s: (b, 0, off + h))
    return pl.pallas_call(
        kern,
        grid_spec=pltpu.PrefetchScalarGridSpec(
            num_scalar_prefetch=1,
            grid=(nb, n_heads),
            in_specs=[blk(0), blk(n_heads), blk(2 * n_heads)],
            out_specs=pl.BlockSpec((1, seq, dh), lambda b, h, s: (b, 0, h)),
            scratch_shapes=[pltpu.VMEM((seq, dh), F32),
                            pltpu.VMEM((seq, dh), F32),
                            pltpu.VMEM((seq, dh), F32),
                            pltpu.VMEM((seq, dh), F32),
                            pltpu.VMEM((seq, LANES), F32),
                            pltpu.VMEM((seq, LANES), F32),
                            pltpu.VMEM((Q_BLOCK, 2 * Q_BLOCK), F32)]),
        out_shape=jax.ShapeDtypeStruct((nb, seq, n_heads * dh), BF16),
        compiler_params=_cparams(("arbitrary", "arbitrary")),
        name="dilated_attn",
    )(slopes, qkv, qkv, qkv)


def _fox_kernel(q_ref, k_ref, v_ref, c_ref, o_ref, m_scr, acc_scr, *, blk, hb):
    i = pl.program_id(2)
    dh = HEAD_DIM
    row = lax.broadcasted_iota(I32, (blk, blk), 0)
    col = lax.broadcasted_iota(I32, (blk, blk), 1)
    m_scr[...] = jnp.full(m_scr.shape, -jnp.inf, F32)
    acc_scr[...] = jnp.zeros(acc_scr.shape, F32)
    ones = jnp.ones((blk, dh), BF16)
    n_slab = blk // LANES

    def step(j, masked):
        rows = pl.ds(pl.multiple_of(j * blk, blk), blk)
        for h in range(hb):
            cols = slice(h * dh, (h + 1) * dh)
            s = _dot_nt(q_ref[0, :, cols], k_ref[0, rows, cols]) - c_ref[0, h, j]
            if masked:
                s = jnp.where(col <= row, s, -jnp.inf)
            m_old = m_scr[h]
            m_new = jnp.maximum(m_old, jnp.max(s, axis=1, keepdims=True))
            alpha = jnp.exp2(m_old - m_new)
            p = jnp.concatenate(
                [jnp.exp2(s[:, t * LANES:(t + 1) * LANES] - m_new) for t in range(n_slab)], axis=1)
            pv = _dot(p.astype(BF16), jnp.concatenate([v_ref[0, rows, cols], ones], axis=1))
            acc_scr[h] = jnp.concatenate([alpha, alpha], axis=1) * acc_scr[h] + pv
            m_scr[h] = m_new

    def body(j, carry):
        step(j, False)
        return carry

    lax.fori_loop(0, i, body, 0)
    step(i, True)
    for h in range(hb):
        acc = acc_scr[h]
        o_ref[0, :, h * dh:(h + 1) * dh] = (acc[:, :dh] / acc[:, dh:]).astype(o_ref.dtype)


def _fox_attention(qkv, cum, nb, seq, n_heads, blk):
    dh = HEAD_DIM
    hb = min(FOX_HEADS_PER_STEP, n_heads)
    assert n_heads % hb == 0
    ng = n_heads // hb
    w = hb * dh
    kern = functools.partial(_fox_kernel, blk=blk, hb=hb)
    return pl.pallas_call(
        kern,
        grid=(nb, ng, seq // blk),
        in_specs=[pl.BlockSpec((1, blk, w), lambda b, g, i: (b, i, g)),
                  pl.BlockSpec((1, seq, w), lambda b, g, i: (b, 0, ng + g)),
                  pl.BlockSpec((1, seq, w), lambda b, g, i: (b, 0, 2 * ng + g)),
                  pl.BlockSpec((1, hb, seq // blk, 1, blk), lambda b, g, i: (b, g, 0, 0, 0))],
        out_specs=pl.BlockSpec((1, blk, w), lambda b, g, i: (b, i, g)),
        out_shape=jax.ShapeDtypeStruct((nb, seq, n_heads * dh), BF16),
        scratch_shapes=[pltpu.VMEM((hb, blk, LANES), F32),
                        pltpu.VMEM((hb, blk, 2 * dh), F32)],
        compiler_params=_cparams(("arbitrary", "arbitrary", "arbitrary")),
        name="fox_attn",
    )(qkv, qkv, qkv, cum)


def _post_attn_kernel(od_ref, of_ref, x_ref, w_ref, ga_ref, scm_ref, shm_ref, g_ref, b_ref,
                      wrh_ref, wrl_ref, br_ref,
                      x1_ref, hp_ref, idx_ref, gate_ref, rank_ref, cnt_ref, carry_scr,
                      *, n_experts, top_k, dil_width):
    i = pl.program_id(0)

    @pl.when(i == 0)
    def _():
        carry_scr[...] = jnp.zeros_like(carry_scr)

    attn = _dot(od_ref[...], w_ref[:dil_width, :]) + _dot(of_ref[...], w_ref[dil_width:, :])
    x1 = _layer_norm(DEEPNORM_ALPHA * x_ref[...] + ga_ref[0] * attn, g_ref[...], b_ref[...])
    x1_ref[...] = x1
    h = x1 * (1.0 + scm_ref[0]) + shm_ref[0]
    h_hi, h_lo = _split_bf16(h)
    hp_ref[...] = h

    logits = _dot_split(h_hi, h_lo, wrh_ref[...], wrl_ref[...]) + br_ref[...]
    tm = logits.shape[0]
    lane = lax.broadcasted_iota(I32, (tm, LANES), 1)
    work = jnp.where(lane < n_experts, logits, -jnp.inf)
    vals, idxs = [], []
    for _ in range(top_k):
        mx = jnp.max(work, axis=1, keepdims=True)
        ix = jnp.min(jnp.where(work == mx, lane, LANES), axis=1, keepdims=True)
        vals.append(mx)
        idxs.append(ix)
        work = jnp.where(lane == ix, -jnp.inf, work)
    exps = [jnp.exp(v - vals[0]) for v in vals]
    denom = exps[0]
    for e in exps[1:]:
        denom = denom + e

    onehot = jnp.zeros((tm, LANES), F32)
    for ix in idxs:
        onehot = onehot + (lane == ix).astype(F32)
    r_i = lax.broadcasted_iota(I32, (tm, tm), 0)
    c_i = lax.broadcasted_iota(I32, (tm, tm), 1)
    tri = (c_i < r_i).astype(BF16)
    before = _dot(tri, onehot.astype(BF16)) + carry_scr[...]

    idx_out = jnp.zeros((tm, LANES), I32)
    gate_out = jnp.zeros((tm, LANES), F32)
    rank_out = jnp.zeros((tm, LANES), I32)
    for k in range(top_k):
        rank_k = jnp.sum(jnp.where(lane == idxs[k], before, 0.0), axis=1, keepdims=True)
        idx_out = jnp.where(lane == k, idxs[k], idx_out)
        gate_out = jnp.where(lane == k, exps[k] / denom, gate_out)
        rank_out = jnp.where(lane == k, rank_k.astype(I32), rank_out)
    idx_ref[...] = idx_out
    gate_ref[...] = gate_out
    rank_ref[...] = rank_out
    carry_scr[...] = carry_scr[...] + jnp.sum(onehot, axis=0, keepdims=True)
    cnt_ref[...] = jnp.broadcast_to(carry_scr[...], cnt_ref.shape)


def _post_attn(o_dil, o_fox, x2, w_out_bf, g_a, sc_m, sh_m, ln_g, ln_b, w_router, b_router, seq):
    t, d_model = x2.shape
    dil_width = o_dil.shape[1]
    n_experts = w_router.shape[1]
    tm = 256
    wr = jnp.pad(w_router, ((0, 0), (0, LANES - n_experts)))
    wr_hi, wr_lo = _split_bf16(wr)
    br = jnp.pad(b_router, (0, LANES - n_experts)).reshape(1, LANES)
    kern = functools.partial(_post_attn_kernel, n_experts=n_experts, top_k=TOP_K, dil_width=dil_width)
    row = lambda w: pl.BlockSpec((tm, w), lambda i: (i, 0))
    const = lambda r, w: pl.BlockSpec((r, w), lambda i: (0, 0))
    bvec = pl.BlockSpec((1, 1, d_model), lambda i: (i * tm // seq, 0, 0))
    return pl.pallas_call(
        kern,
        grid=(t // tm,),
        in_specs=[row(dil_width), row(o_fox.shape[1]), row(d_model), const(d_model, d_model),
                  bvec, bvec, bvec, const(1, d_model), const(1, d_model),
                  const(d_model, LANES), const(d_model, LANES), const(1, LANES)],
        out_specs=[row(d_model), row(d_model), row(LANES), row(LANES), row(LANES),
                   pl.BlockSpec((8, LANES), lambda i: (0, 0))],
        out_shape=[jax.ShapeDtypeStruct((t, d_model), F32),
                   jax.ShapeDtypeStruct((t, d_model), F32),
                   jax.ShapeDtypeStruct((t, LANES), I32),
                   jax.ShapeDtypeStruct((t, LANES), F32),
                   jax.ShapeDtypeStruct((t, LANES), I32),
                   jax.ShapeDtypeStruct((8, LANES), F32)],
        scratch_shapes=[pltpu.VMEM((1, LANES), F32)],
        compiler_params=_cparams(("arbitrary",)),
        name="post_attn",
    )(o_dil, o_fox, x2, w_out_bf, g_a, sc_m, sh_m, ln_g.reshape(1, -1), ln_b.reshape(1, -1),
      wr_hi, wr_lo, br)


def _scatter_kernel(dest_ref, fill_lo_ref, fill_hi_ref, used_ref, hp_ref, xs_ref, zbuf, sem, zsem,
                    *, top_k):
    i = pl.program_id(0)
    tm = hp_ref.shape[0]
    rb = ROW_BLOCK

    @pl.when(i == 0)
    def _():
        zbuf[...] = jnp.zeros(zbuf.shape, zbuf.dtype)

        def tails(wait):
            def row(r, c):
                copy = pltpu.make_async_copy(zbuf.at[pl.ds(0, 1)], xs_ref.at[pl.ds(r, 1)], zsem.at[0])
                copy.wait() if wait else copy.start()
                return c

            def expert(e, c):
                return lax.fori_loop(fill_lo_ref[e], fill_hi_ref[e], row, c)

            lax.fori_loop(0, fill_lo_ref.shape[0], expert, 0)

        def rest(wait):
            def blk(b, c):
                copy = pltpu.make_async_copy(zbuf, xs_ref.at[pl.ds(pl.multiple_of(b * rb, rb), rb)],
                                             zsem.at[1])
                copy.wait() if wait else copy.start()
                return c

            lax.fori_loop(used_ref[0], xs_ref.shape[0] // rb, blk, 0)

        tails(False)
        rest(False)
        tails(True)
        rest(True)

    base = i * (tm * top_k)
    for t in range(tm):
        for k in range(top_k):
            d = dest_ref[base + (t * top_k + k)]
            pltpu.make_async_copy(hp_ref.at[pl.ds(t, 1)], xs_ref.at[pl.ds(d, 1)], sem).start()

    def drain(t, carry):
        for k in range(top_k):
            pltpu.make_async_copy(hp_ref.at[pl.ds(0, 1)], xs_ref.at[pl.ds(0, 1)], sem).wait()
        return carry

    lax.fori_loop(0, tm, drain, 0, unroll=8)


def _scatter_rows(dest_flat, fill_lo, fill_hi, used_blocks, hp, n_rows):
    t, w = hp.shape
    tm = 256
    kern = functools.partial(_scatter_kernel, top_k=TOP_K)
    return pl.pallas_call(
        kern,
        grid_spec=pltpu.PrefetchScalarGridSpec(
            num_scalar_prefetch=4,
            grid=(t // tm,),
            in_specs=[pl.BlockSpec((tm, w), lambda i, *_: (i, 0))],
            out_specs=pl.BlockSpec(memory_space=pl.ANY),
            scratch_shapes=[pltpu.VMEM((ROW_BLOCK, w), F32),
                            pltpu.SemaphoreType.DMA(()),
                            pltpu.SemaphoreType.DMA((2,))]),
        out_shape=jax.ShapeDtypeStruct((n_rows, w), F32),
        compiler_params=_cparams(("arbitrary",)),
        name="scatter_rows",
    )(dest_flat, fill_lo, fill_hi, used_blocks, hp)


def _moe_kernel(iexp_ref, irow_ref, inu_ref, used_ref, xs_ref, wg_ref, wu_ref, wd_ref, bg_ref, bu_ref,
                bd_ref, y_ref, xbuf, yacc, sem_in, sem_out, *, n_chunks):
    del iexp_ref
    it = pl.program_id(0)
    f = pl.program_id(1)
    n_items = pl.num_programs(0)
    nu = inu_ref[it]
    rb = ROW_BLOCK
    rows_of = lambda k, n=1: pl.ds(pl.multiple_of(k * rb, rb), n * rb)

    def x_copy(item, k):
        src = pl.ds(pl.multiple_of(irow_ref[item] + k * rb, rb), rb)
        return pltpu.make_async_copy(xs_ref.at[src], xbuf.at[rows_of(k)], sem_in)

    def y_copy(k):
        dst = pl.ds(pl.multiple_of(irow_ref[it] + k * rb, rb), rb)
        return pltpu.make_async_copy(yacc.at[rows_of(k)], y_ref.at[dst], sem_out)

    def for_units(n, fn):
        def body(k, c):
            fn(k)
            return c
        lax.fori_loop(0, n, body, 0)

    def init_unit(k):
        yacc[rows_of(k), :] = jnp.broadcast_to(bd_ref[0], (rb, yacc.shape[1]))

    @pl.when(nu > 0)
    def _():
        first = f == 0
        last = f == n_chunks - 1

        @pl.when(first & (it == 0))
        def _():
            for_units(nu, lambda k: x_copy(0, k).start())

        @pl.when(first)
        def _():
            for_units(nu, init_unit)

        wg = wg_ref[0].astype(BF16)
        wu = wu_ref[0].astype(BF16)
        wd = wd_ref[0].astype(BF16)
        bg = bg_ref[0]
        bu = bu_ref[0]

        @pl.when(first)
        def _():
            for_units(nu, lambda k: x_copy(it, k).wait())

        def block(k, n):
            xb = xbuf[rows_of(k, n), :].astype(BF16)
            g = _dot(xb, wg) + bg
            u = _dot(xb, wu) + bu
            g = jnp.minimum(g, SWIGLU_LIMIT)
            u = jnp.clip(u, -SWIGLU_LIMIT, SWIGLU_LIMIT)
            act = (u + 1.0) * (g * jax.nn.sigmoid(SWIGLU_ALPHA * g))
            yacc[rows_of(k, n), :] += _dot(act.astype(BF16), wd)

        def quad(t, c):
            block(4 * t, 2)
            block(4 * t + 2, 2)
            return c

        n_quads = lax.shift_right_logical(nu, 2)
        lax.fori_loop(0, n_quads, quad, 0)
        done = n_quads * 4

        @pl.when((nu & 2) != 0)
        def _():
            block(done, 2)

        @pl.when((nu & 1) != 0)
        def _():
            block(done + (nu & 2), 1)

        @pl.when(last)
        def _():
            for_units(nu, lambda k: y_copy(k).start())
            nxt = jnp.minimum(it + 1, n_items - 1)
            n_next = jnp.where(it + 1 < n_items, inu_ref[nxt], 0)
            for_units(n_next, lambda k: x_copy(nxt, k).start())
            for_units(nu, lambda k: y_copy(k).wait())

    @pl.when((it == n_items - 1) & (f == n_chunks - 1))
    def _():
        first = used_ref[0]
        last = y_ref.shape[0] // rb
        dst_of = lambda b: pl.ds(pl.multiple_of(b * rb, rb), rb)
        yacc[pl.ds(0, rb), :] = jnp.zeros((rb, yacc.shape[1]), yacc.dtype)

        def start(b, c):
            pltpu.make_async_copy(yacc.at[pl.ds(0, rb)], y_ref.at[dst_of(b)], sem_out).start()
            return c

        def wait(b, c):
            pltpu.make_async_copy(yacc.at[pl.ds(0, rb)], y_ref.at[dst_of(b)], sem_out).wait()
            return c

        lax.fori_loop(first, last, start, 0)
        lax.fori_loop(first, last, wait, 0)


def _moe_experts(item_exp, item_row, item_nblk, used_blocks, xs, w_gate_up, b_gate_up, w_down, b_down):
    n_rows = xs.shape[0]
    n_exp, d_model, two_f = w_gate_up.shape
    d_ff = two_f // 2
    fc = FF_CHUNK
    n_chunks = d_ff // fc
    n_items = item_exp.shape[0]
    kern = functools.partial(_moe_kernel, n_chunks=n_chunks)

    def fsel(it, f, nbk):
        return jnp.where(nbk[it] > 0, f, n_chunks - 1)

    return pl.pallas_call(
        kern,
        grid_spec=pltpu.PrefetchScalarGridSpec(
            num_scalar_prefetch=4,
            grid=(n_items, n_chunks),
            in_specs=[
                pl.BlockSpec(memory_space=pl.ANY),
                pl.BlockSpec((1, d_model, fc), lambda it, f, ex, ro, nbk, us: (ex[it], 0, fsel(it, f, nbk))),
                pl.BlockSpec((1, d_model, fc),
                             lambda it, f, ex, ro, nbk, us: (ex[it], 0, n_chunks + fsel(it, f, nbk))),
                pl.BlockSpec((1, fc, d_model), lambda it, f, ex, ro, nbk, us: (ex[it], fsel(it, f, nbk), 0)),
                pl.BlockSpec((1, 1, fc), lambda it, f, ex, ro, nbk, us: (ex[it], 0, fsel(it, f, nbk))),
                pl.BlockSpec((1, 1, fc),
                             lambda it, f, ex, ro, nbk, us: (ex[it], 0, n_chunks + fsel(it, f, nbk))),
                pl.BlockSpec((1, 1, d_model), lambda it, f, ex, ro, nbk, us: (ex[it], 0, 0)),
            ],
            out_specs=pl.BlockSpec(memory_space=pl.ANY),
            scratch_shapes=[pltpu.VMEM((ITEM_ROWS, d_model), F32),
                            pltpu.VMEM((ITEM_ROWS, d_model), F32),
                            pltpu.SemaphoreType.DMA(()),
                            pltpu.SemaphoreType.DMA(())]),
        out_shape=jax.ShapeDtypeStruct((n_rows, d_model), F32),
        compiler_params=_cparams(("arbitrary", "arbitrary")),
        name="moe_experts",
    )(item_exp, item_row, item_nblk, used_blocks, xs, w_gate_up, w_gate_up, w_down,
      b_gate_up.reshape(n_exp, 1, two_f), b_gate_up.reshape(n_exp, 1, two_f),
      b_down.reshape(n_exp, 1, d_model))


def _combine_kernel(dest_ref, y_ref, x1_ref, gate_ref, gm_ref, g_ref, b_ref, o_ref, ybuf, sem, *, top_k):
    i = pl.program_id(0)
    n_steps = pl.num_programs(0)
    tm = x1_ref.shape[0]
    slot = i % 2

    def gather(step, slot_):
        base = step * (tm * top_k)
        for t in range(tm):
            for k in range(top_k):
                d = dest_ref[base + (t * top_k + k)]
                pltpu.make_async_copy(y_ref.at[pl.ds(d, 1)], ybuf.at[slot_, pl.ds(k * tm + t, 1)],
                                      sem.at[slot_]).start()

    @pl.when(i == 0)
    def _():
        gather(0, 0)

    @pl.when(i + 1 < n_steps)
    def _():
        gather(i + 1, 1 - slot)

    pltpu.make_async_copy(y_ref.at[pl.ds(0, top_k * tm)], ybuf.at[slot], sem.at[slot]).wait()
    gates = gate_ref[...]
    ffn = gates[:, 0:1] * ybuf[slot, pl.ds(0, tm), :]
    for k in range(1, top_k):
        ffn = ffn + gates[:, k:k + 1] * ybuf[slot, pl.ds(k * tm, tm), :]
    o_ref[...] = _layer_norm(DEEPNORM_ALPHA * x1_ref[...] + gm_ref[0] * ffn, g_ref[...], b_ref[...])


def _combine(dest_flat, y, x1, gates, g_m, ln_g, ln_b, seq):
    t, d_model = x1.shape
    tm = 128
    kern = functools.partial(_combine_kernel, top_k=TOP_K)
    return pl.pallas_call(
        kern,
        grid_spec=pltpu.PrefetchScalarGridSpec(
            num_scalar_prefetch=1,
            grid=(t // tm,),
            in_specs=[pl.BlockSpec(memory_space=pl.ANY),
                      pl.BlockSpec((tm, d_model), lambda i, d: (i, 0)),
                      pl.BlockSpec((tm, LANES), lambda i, d: (i, 0)),
                      pl.BlockSpec((1, 1, d_model), lambda i, d: (i * tm // seq, 0, 0)),
                      pl.BlockSpec((1, d_model), lambda i, d: (0, 0)),
                      pl.BlockSpec((1, d_model), lambda i, d: (0, 0))],
            out_specs=pl.BlockSpec((tm, d_model), lambda i, d: (i, 0)),
            scratch_shapes=[pltpu.VMEM((2, TOP_K * tm, d_model), F32),
                            pltpu.SemaphoreType.DMA((2,))]),
        out_shape=jax.ShapeDtypeStruct((t, d_model), F32),
        compiler_params=_cparams(("arbitrary",)),
        name="combine_ln2",
    )(dest_flat, y, x1, gates, g_m, ln_g.reshape(1, -1), ln_b.reshape(1, -1))


def _routing_plan(counts, idx, rank, n_tokens):
    n_exp = counts.shape[0]
    padded = (counts + ROW_BLOCK - 1) // ROW_BLOCK * ROW_BLOCK
    pad_end = jnp.cumsum(padded)
    pad_start = pad_end - padded
    dest = (pad_start[idx] + rank).reshape(-1).astype(I32)

    max_rows = (n_tokens * TOP_K + n_exp * (ROW_BLOCK - 1) + ROW_BLOCK - 1) // ROW_BLOCK * ROW_BLOCK
    max_items = n_exp + max_rows // ITEM_ROWS
    items_per = (padded + ITEM_ROWS - 1) // ITEM_ROWS
    items_end = jnp.cumsum(items_per)
    total = items_end[-1]
    it = jnp.arange(max_items, dtype=I32)
    last = jnp.maximum(total - 1, 0)
    it_c = jnp.minimum(it, last)
    exp_of = jnp.sum((items_end[None, :] <= it_c[:, None]).astype(I32), axis=1)
    exp_of = jnp.minimum(exp_of, n_exp - 1)
    local = it_c - (items_end[exp_of] - items_per[exp_of])
    row0 = pad_start[exp_of] + local * ITEM_ROWS
    rows = jnp.minimum(padded[exp_of] - local * ITEM_ROWS, ITEM_ROWS)
    nblk = jnp.where(it < total, rows // ROW_BLOCK, 0)
    used_blocks = (pad_end[-1:] // ROW_BLOCK).astype(I32)
    fill = ((pad_start + counts).astype(I32), pad_end.astype(I32))
    return dest, fill, (exp_of.astype(I32), row0.astype(I32), nblk.astype(I32), used_blocks), max_rows


def kernel(x, c, w_ada, b_ada, w_in, b_forget, w_out, ln1_g, ln1_b, w_router, b_router,
           w_gate_up, b_gate_up, w_down, b_down, ln2_g, ln2_b):
    nb, seq, d_model = x.shape
    n_heads = d_model // HEAD_DIM
    n_dil = n_heads // 2
    n_fox = n_heads - n_dil
    dil_width, fox_width = n_dil * HEAD_DIM, n_fox * HEAD_DIM
    t = nb * seq
    layer = 0

    mod = _ada_mod(c, w_ada[layer], b_ada[layer])
    sh_a, sc_a, g_a, sh_m, sc_m, g_m = [m.reshape(nb, 1, d_model) for m in jnp.split(mod, 6, axis=-1)]

    x2 = x.reshape(t, d_model)
    qkv_dil, qkv_fox, f_logit = _in_proj(x2, sc_a, sh_a, w_in[layer], seq, dil_width, fox_width, n_fox)

    slopes = 2.0 ** (-8.0 * jnp.arange(1, n_dil + 1, dtype=F32) / n_dil)
    o_dil = _dilated_attention(qkv_dil.reshape(nb, seq, 3 * dil_width), slopes, nb, seq, n_dil)

    fox_blk = 256
    cum = _forget_cumsum(f_logit, b_forget[layer], nb, seq).reshape(
        nb, n_fox, seq // fox_blk, 1, fox_blk)
    o_fox = _fox_attention(qkv_fox.reshape(nb, seq, 3 * fox_width), cum, nb, seq, n_fox, fox_blk)

    x1, hp, idx, gates, rank, cnt = _post_attn(
        o_dil.reshape(t, dil_width), o_fox.reshape(t, fox_width), x2, w_out[layer].astype(BF16),
        g_a, sc_m, sh_m, ln1_g[layer], ln1_b[layer], w_router[layer], b_router[layer], seq)

    counts = cnt[0, :N_EXPERTS].astype(I32)
    dest, fill, items, max_rows = _routing_plan(counts, idx[:, :TOP_K], rank[:, :TOP_K], t)
    xs = _scatter_rows(dest, *fill, items[3], hp, max_rows)
    y = _moe_experts(*items, xs, w_gate_up[layer], b_gate_up[layer], w_down[layer], b_down[layer])
    out = _combine(dest, y, x1, gates, g_m, ln2_g[layer], ln2_b[layer], seq)
    return out.reshape(nb, seq, d_model)
```

```python
import functools
import math

import jax
import jax.numpy as jnp
from jax import lax
from jax.experimental import pallas as pl
from jax.experimental.pallas import tpu as pltpu

F32 = jnp.float32
BF16 = jnp.bfloat16
I32 = jnp.int32
U32 = jnp.uint32

HEAD_DIM = 128
Q_BLOCK = 128
DIL_PATTERNS = ((128, 1), (512, 4), (2048, 16))
N_EXPERTS = 32
TOP_K = 4
SWIGLU_LIMIT = 7.0
SWIGLU_ALPHA = 1.702
LN_EPS = 1e-5
DEPTH = 1
DEEPNORM_ALPHA = (2.0 * DEPTH) ** 0.25

LOG2E = math.log2(math.e)
DIL_GROUP = 4
DEINT = 4
FOX_HEADS_PER_STEP = 4

LANES = 128
VMEM_LIMIT = 56 * 1024 * 1024

ROW_BLOCK = 128
ITEM_ROWS = 1536
FF_CHUNK = 256


def _cparams(sem):
    return pltpu.CompilerParams(dimension_semantics=sem, vmem_limit_bytes=VMEM_LIMIT)


def _dot(a, b):
    return jnp.dot(a, b, preferred_element_type=F32)


def _dot_nt(a, b):
    return lax.dot_general(a, b, (((1,), (1,)), ((), ())), preferred_element_type=F32)


def _split_bf16(x):
    hi = x.astype(BF16)
    lo = (x - hi.astype(F32)).astype(BF16)
    return hi, lo


def _split_weight(w):
    return jnp.concatenate(_split_bf16(w), axis=1)


def _dot_split(x_hi, x_lo, w_cat):
    r = _dot(x_hi, w_cat) + _dot(x_lo, w_cat)
    return r[:, :LANES] + r[:, LANES:]


def _layer_norm(y, g, b):
    mu = jnp.mean(y, axis=-1, keepdims=True)
    yc = y - mu
    var = jnp.mean(yc * yc, axis=-1, keepdims=True)
    return yc * lax.rsqrt(var + LN_EPS) * g + b


def _ada_kernel(ct_ref, w_ref, b_ref, o_ref):
    ct = ct_ref[...]
    s = ct * jax.nn.sigmoid(ct)
    d_model, nb = ct.shape
    tn = w_ref.shape[1]
    sb = [jnp.broadcast_to(s[:, b:b + 1], (d_model, LANES)) for b in range(nb)]
    for j in range(tn // LANES):
        cols = slice(j * LANES, (j + 1) * LANES)
        w = w_ref[:, cols]
        for b in range(nb):
            acc = jnp.sum(w * sb[b], axis=0, keepdims=True)
            o_ref[b:b + 1, cols] = acc + b_ref[:, cols]


def _ada_mod(c, w, b):
    nb, d_model = c.shape
    n = w.shape[1]
    tn = 1536 if n % 1536 == 0 else LANES
    return pl.pallas_call(
        _ada_kernel,
        grid=(n // tn,),
        in_specs=[pl.BlockSpec((d_model, nb), lambda j: (0, 0)),
                  pl.BlockSpec((d_model, tn), lambda j: (0, j)),
                  pl.BlockSpec((1, tn), lambda j: (0, j))],
        out_specs=pl.BlockSpec((nb, tn), lambda j: (0, j)),
        out_shape=jax.ShapeDtypeStruct((nb, n), F32),
        compiler_params=_cparams(("arbitrary",)),
        name="ada_mod",
    )(c.T, w, b.reshape(1, n))


def _inproj_kernel(x_ref, sc_ref, sh_ref, w_ref, wf_ref,
                   odil_ref, ofox_ref, of_ref, h_scr, *, n_dil_tiles, q_tiles, scale):
    j = pl.program_id(1)

    @pl.when(j == 0)
    def _():
        h = x_ref[...] * (1.0 + sc_ref[0]) + sh_ref[0]
        h_hi, h_lo = _split_bf16(h)
        h_scr[...] = h_hi
        of_ref[...] = _dot_split(h_hi, h_lo, wf_ref[...])

    acc = _dot(h_scr[...], w_ref[...])
    is_q = (j < q_tiles) | ((j >= n_dil_tiles) & (j < n_dil_tiles + q_tiles))
    acc = acc * jnp.where(is_q, scale, 1.0).astype(F32)

    @pl.when(j < n_dil_tiles)
    def _():
        odil_ref[...] = acc

    @pl.when(j >= n_dil_tiles)
    def _():
        ofox_ref[...] = acc.astype(BF16)


def _in_proj(x2, sc, sh, w_in, layer, seq, dil_width, fox_width, n_fox_heads):
    t, d_model = x2.shape
    tm, tn = min(1024, seq), min(1024, dil_width)
    n_dil_tiles = 3 * dil_width // tn
    n_fox_tiles = 3 * fox_width // tn
    q_tiles = dil_width // tn
    assert dil_width == fox_width and dil_width % tn == 0 and seq % tm == 0
    main = 3 * dil_width + 3 * fox_width
    wf = _split_weight(jnp.pad(w_in[layer, :, main:], ((0, 0), (0, LANES - n_fox_heads))))
    kern = functools.partial(_inproj_kernel, n_dil_tiles=n_dil_tiles, q_tiles=q_tiles,
                             scale=LOG2E / math.sqrt(HEAD_DIM))
    bidx = lambda i, j: (i * tm // seq, 0, 0)
    return pl.pallas_call(
        kern,
        grid=(t // tm, n_dil_tiles + n_fox_tiles),
        in_specs=[pl.BlockSpec((tm, d_model), lambda i, j: (i, 0)),
                  pl.BlockSpec((1, 1, d_model), bidx),
                  pl.BlockSpec((1, 1, d_model), bidx),
                  pl.BlockSpec((None, d_model, tn), lambda i, j: (layer, 0, j)),
                  pl.BlockSpec((d_model, 2 * LANES), lambda i, j: (0, 0))],
        out_specs=[pl.BlockSpec((tm, tn), lambda i, j: (i, jnp.minimum(j, n_dil_tiles - 1))),
                   pl.BlockSpec((tm, tn), lambda i, j: (i, jnp.maximum(j - n_dil_tiles, 0))),
                   pl.BlockSpec((tm, LANES), lambda i, j: (i, 0))],
        out_shape=[jax.ShapeDtypeStruct((t, 3 * dil_width), F32),
                   jax.ShapeDtypeStruct((t, 3 * fox_width), BF16),
                   jax.ShapeDtypeStruct((t, LANES), F32)],
        scratch_shapes=[pltpu.VMEM((tm, d_model), BF16)],
        compiler_params=_cparams(("arbitrary", "arbitrary")),
        name="in_proj",
    )(x2, sc, sh, w_in.astype(BF16), wf)


def _cum_kernel(f_ref, b_ref, o_ref):
    hf = o_ref.shape[1]
    z = f_ref[...].T[:hf, :] + b_ref[...]
    c = jnp.minimum(z, 0.0) - jnp.log1p(jnp.exp(-jnp.abs(z)))
    seq = z.shape[1]
    lane = lax.broadcasted_iota(I32, z.shape, 1)
    shift = 1
    while shift < seq:
        c = c + jnp.where(lane >= shift, pltpu.roll(c, shift, axis=1), 0.0)
        shift *= 2
    o_ref[0] = c * LOG2E


def _forget_cumsum(f_logit, b_forget, nb, seq):
    hf = b_forget.shape[0]
    return pl.pallas_call(
        _cum_kernel,
        grid=(nb,),
        in_specs=[pl.BlockSpec((seq, LANES), lambda b: (b, 0)),
                  pl.BlockSpec((hf, 1), lambda b: (0, 0))],
        out_specs=pl.BlockSpec((1, hf, seq), lambda b: (b, 0, 0)),
        out_shape=jax.ShapeDtypeStruct((nb, hf, seq), F32),
        compiler_params=_cparams(("arbitrary",)),
        name="forget_cumsum",
    )(f_logit, b_forget.reshape(hf, 1))


def _dil_kernel(slope_ref, q_ref, k_ref, v_ref, o_ref, q4_ref, k4_ref, v4_ref, acc_ref, l_ref, m_ref,
                bias_ref, *, patterns, seq):
    slope = slope_ref[pl.program_id(1)]
    qb = Q_BLOCK
    row = lax.broadcasted_iota(I32, (qb, 2 * qb), 0)
    col = lax.broadcasted_iota(I32, (qb, 2 * qb), 1)
    delta = row + qb - col
    is_prev = col < qb
    neg_inf = F32(-jnp.inf)
    ones = jnp.ones((2 * qb, HEAD_DIM), BF16)

    sub = seq // DEINT
    chunk = min(256, sub)
    for r4 in range(DEINT):
        for c0 in range(0, sub, chunk):
            src = pl.ds(r4 + DEINT * c0, chunk, stride=DEINT)
            dst = pl.ds(r4 * sub + c0, chunk)
            q4_ref[dst, :] = q_ref[0, src, :]
            k4_ref[dst, :] = k_ref[0, src, :]
            v4_ref[dst, :] = v_ref[0, src, :]

    for pi, (window, dil) in enumerate(sorted(patterns, key=lambda wd: -wd[1])):
        reach = window // dil
        nblk = seq // dil // qb
        n_blocks = dil * nblk
        assert reach <= qb and nblk * qb * dil == seq and n_blocks % DIL_GROUP == 0
        sd = slope * (float(dil) * LOG2E)
        bias_ref[...] = jnp.where((delta >= 0) & (delta <= reach), -sd * delta.astype(F32), neg_inf)

        def block(idx, dil=dil, nblk=nblk, pi=pi):
            r = idx // nblk
            n = idx - r * nblk
            start = r + n * (qb * dil)
            cur = pl.ds(start, qb, stride=dil) if dil > 1 else pl.ds(pl.multiple_of(start, qb), qb)
            if dil % DEINT == 0:
                step = dil // DEINT
                first = (r % DEINT) * sub + r // DEINT
                lstart = first + n * (qb * step)
                lprev = jnp.maximum(lstart - qb * step, first)
                if step > 1:
                    src_cur, src_prev = pl.ds(lstart, qb, stride=step), pl.ds(lprev, qb, stride=step)
                else:
                    src_cur = pl.ds(pl.multiple_of(lstart, qb), qb)
                    src_prev = pl.ds(pl.multiple_of(lprev, qb), qb)
                q = q4_ref[src_cur, :].astype(BF16)
                kk = jnp.concatenate([k4_ref[src_prev, :], k4_ref[src_cur, :]], axis=0).astype(BF16)
                vv = jnp.concatenate([v4_ref[src_prev, :], v4_ref[src_cur, :]], axis=0).astype(BF16)
            else:
                pstart = jnp.maximum(start - qb * dil, r)
                prev = (pl.ds(pstart, qb, stride=dil) if dil > 1
                        else pl.ds(pl.multiple_of(pstart, qb), qb))
                q = q_ref[0, cur, :].astype(BF16)
                kk = jnp.concatenate([k_ref[0, prev, :], k_ref[0, cur, :]], axis=0).astype(BF16)
                vv = jnp.concatenate([v_ref[0, prev, :], v_ref[0, cur, :]], axis=0).astype(BF16)
            s = _dot_nt(q, kk) + bias_ref[...]
            s = jnp.where(is_prev & (n == 0), neg_inf, s)
            m_blk = jnp.max(s, axis=1, keepdims=True)
            p = jnp.exp2(s - m_blk)
            pv = _dot(p.astype(BF16), jnp.concatenate([vv, ones], axis=1))
            o_blk, l_blk = pv[:, :HEAD_DIM], pv[:, HEAD_DIM:]
            if pi == 0:
                m_ref[cur, :] = jnp.broadcast_to(m_blk, (qb, LANES))
                acc_ref[cur, :] = o_blk
                l_ref[cur, :] = l_blk
            else:
                m_old = m_ref[cur, :]
                m_new = jnp.maximum(m_old, m_blk)
                a_old = jnp.exp2(m_old - m_new)
                a_blk = jnp.exp2(m_blk - m_new)
                m_ref[cur, :] = m_new
                acc_ref[cur, :] = acc_ref[cur, :] * a_old + o_blk * a_blk
                l_ref[cur, :] = l_ref[cur, :] * a_old + l_blk * a_blk

        def body(g, carry, block=block):
            for u in range(DIL_GROUP):
                block(g * DIL_GROUP + u)
            return carry

        lax.fori_loop(0, n_blocks // DIL_GROUP, body, 0)

    chunk = 512
    for c0 in range(0, seq, chunk):
        rows = slice(c0, c0 + chunk)
        o_ref[0, rows, :] = (acc_ref[rows, :] / l_ref[rows, :]).astype(o_ref.dtype)


def _dilated_attention(qkv, slopes, nb, seq, n_heads):
    dh = HEAD_DIM
    kern = functools.partial(_dil_kernel, patterns=DIL_PATTERNS, seq=seq)
    blk = lambda off: pl.BlockSpec((1, seq, dh), lambda b, h, s: (b, 0, off + h))
    return pl.pallas_call(
        kern,
        grid_spec=pltpu.PrefetchScalarGridSpec(
            num_scalar_prefetch=1,
            grid=(nb, n_heads),
            in_specs=[blk(0), blk(n_heads), blk(2 * n_heads)],
            out_specs=pl.BlockSpec((1, seq, dh), lambda b, h, s: (b, 0, h)),
            scratch_shapes=[pltpu.VMEM((seq, dh), F32),
                            pltpu.VMEM((seq, dh), F32),
                            pltpu.VMEM((seq, dh), F32),
                            pltpu.VMEM((seq, dh), F32),
                            pltpu.VMEM((seq, LANES), F32),
                            pltpu.VMEM((seq, LANES), F32),
                            pltpu.VMEM((Q_BLOCK, 2 * Q_BLOCK), F32)]),
        out_shape=jax.ShapeDtypeStruct((nb, seq, n_heads * dh), BF16),
        compiler_params=_cparams(("arbitrary", "arbitrary")),
        name="dilated_attn",
    )(slopes, qkv, qkv, qkv)


def _fox_kernel(q_ref, k_ref, v_ref, c_ref, o_ref, m_scr, acc_scr, *, blk, hb):
    i = pl.program_id(2)
    dh = HEAD_DIM
    row = lax.broadcasted_iota(I32, (blk, blk), 0)
    col = lax.broadcasted_iota(I32, (blk, blk), 1)
    m_scr[...] = jnp.full(m_scr.shape, -jnp.inf, F32)
    acc_scr[...] = jnp.zeros(acc_scr.shape, F32)
    ones = jnp.ones((blk, dh), BF16)
    n_slab = blk // LANES

    def step(j, masked):
        rows = pl.ds(pl.multiple_of(j * blk, blk), blk)
        for h in range(hb):
            cols = slice(h * dh, (h + 1) * dh)
            s = _dot_nt(q_ref[0, :, cols], k_ref[0, rows, cols]) - c_ref[0, h, j]
            if masked:
                s = jnp.where(col <= row, s, -jnp.inf)
            m_old = m_scr[h]
            m_new = jnp.maximum(m_old, jnp.max(s, axis=1, keepdims=True))
            alpha = jnp.exp2(m_old - m_new)
            p = jnp.concatenate(
                [jnp.exp2(s[:, t * LANES:(t + 1) * LANES] - m_new) for t in range(n_slab)], axis=1)
            pv = _dot(p.astype(BF16), jnp.concatenate([v_ref[0, rows, cols], ones], axis=1))
            acc_scr[h] = jnp.concatenate([alpha, alpha], axis=1) * acc_scr[h] + pv
            m_scr[h] = m_new

    def body(j, carry):
        step(j, False)
        return carry

    lax.fori_loop(0, i, body, 0)
    step(i, True)
    for h in range(hb):
        acc = acc_scr[h]
        o_ref[0, :, h * dh:(h + 1) * dh] = (acc[:, :dh] / acc[:, dh:]).astype(o_ref.dtype)


def _fox_attention(qkv, cum, nb, seq, n_heads, blk):
    dh = HEAD_DIM
    hb = min(FOX_HEADS_PER_STEP, n_heads)
    assert n_heads % hb == 0
    ng = n_heads // hb
    w = hb * dh
    kern = functools.partial(_fox_kernel, blk=blk, hb=hb)
    return pl.pallas_call(
        kern,
        grid=(nb, ng, seq // blk),
        in_specs=[pl.BlockSpec((1, blk, w), lambda b, g, i: (b, i, g)),
                  pl.BlockSpec((1, seq, w), lambda b, g, i: (b, 0, ng + g)),
                  pl.BlockSpec((1, seq, w), lambda b, g, i: (b, 0, 2 * ng + g)),
                  pl.BlockSpec((1, hb, seq // blk, 1, blk), lambda b, g, i: (b, g, 0, 0, 0))],
        out_specs=pl.BlockSpec((1, blk, w), lambda b, g, i: (b, i, g)),
        out_shape=jax.ShapeDtypeStruct((nb, seq, n_heads * dh), BF16),
        scratch_shapes=[pltpu.VMEM((hb, blk, LANES), F32),
                        pltpu.VMEM((hb, blk, 2 * dh), F32)],
        compiler_params=_cparams(("arbitrary", "arbitrary", "arbitrary")),
        name="fox_attn",
    )(qkv, qkv, qkv, cum)


def _post_attn_kernel(od_ref, of_ref, x_ref, w_ref, ga_ref, scm_ref, shm_ref, g_ref, b_ref,
                      wr_ref, br_ref,
                      x1_ref, hp_ref, idx_ref, gate_ref, rank_ref, cnt_ref, carry_scr,
                      *, n_experts, top_k, dil_width, sub):
    i = pl.program_id(0)

    @pl.when(i == 0)
    def _():
        carry_scr[...] = jnp.zeros_like(carry_scr)

    for r0 in range(0, x_ref.shape[0], sub):
        _post_attn_rows(slice(r0, r0 + sub), od_ref, of_ref, x_ref, w_ref, ga_ref, scm_ref, shm_ref,
                        g_ref, b_ref, wr_ref, br_ref, x1_ref, hp_ref, idx_ref, gate_ref, rank_ref,
                        carry_scr, n_experts=n_experts, top_k=top_k, dil_width=dil_width)
    cnt_ref[...] = jnp.broadcast_to(carry_scr[...], cnt_ref.shape)


def _post_attn_rows(rows, od_ref, of_ref, x_ref, w_ref, ga_ref, scm_ref, shm_ref, g_ref, b_ref,
                    wr_ref, br_ref, x1_ref, hp_ref, idx_ref, gate_ref, rank_ref, carry_scr,
                    *, n_experts, top_k, dil_width):
    attn = (_dot(od_ref[rows, :], w_ref[:dil_width, :]) + _dot(of_ref[rows, :], w_ref[dil_width:, :]))
    x1 = _layer_norm(DEEPNORM_ALPHA * x_ref[rows, :] + ga_ref[0] * attn, g_ref[...], b_ref[...])
    x1_ref[rows, :] = x1
    h = x1 * (1.0 + scm_ref[0]) + shm_ref[0]
    h_hi, h_lo = _split_bf16(h)
    hp_ref[rows, :] = h

    logits = _dot_split(h_hi, h_lo, wr_ref[...]) + br_ref[...]
    tm = logits.shape[0]
    lane = lax.broadcasted_iota(I32, (tm, LANES), 1)
    work = jnp.where(lane < n_experts, logits, -jnp.inf)
    vals, idxs = [], []
    for _ in range(top_k):
        mx = jnp.max(work, axis=1, keepdims=True)
        ix = jnp.min(jnp.where(work == mx, lane, LANES), axis=1, keepdims=True)
        vals.append(mx)
        idxs.append(ix)
        work = jnp.where(lane == ix, -jnp.inf, work)
    exps = [jnp.exp(v - vals[0]) for v in vals]
    denom = exps[0]
    for e in exps[1:]:
        denom = denom + e

    onehot = jnp.zeros((tm, LANES), F32)
    for ix in idxs:
        onehot = onehot + (lane == ix).astype(F32)
    r_i = lax.broadcasted_iota(I32, (tm, tm), 0)
    c_i = lax.broadcasted_iota(I32, (tm, tm), 1)
    tri = (c_i < r_i).astype(BF16)
    before = _dot(tri, onehot.astype(BF16)) + carry_scr[...]

    idx_out = jnp.zeros((tm, LANES), I32)
    gate_out = jnp.zeros((tm, LANES), F32)
    rank_out = jnp.zeros((tm, LANES), I32)
    for k in range(top_k):
        rank_k = jnp.sum(jnp.where(lane == idxs[k], before, 0.0), axis=1, keepdims=True)
        idx_out = jnp.where(lane == k, idxs[k], idx_out)
        gate_out = jnp.where(lane == k, exps[k] / denom, gate_out)
        rank_out = jnp.where(lane == k, rank_k.astype(I32), rank_out)
    gate_ref[rows, :] = gate_out
    idx_ref[:, rows] = idx_out.T[:idx_ref.shape[0], :]
    rank_ref[:, rows] = rank_out.T[:rank_ref.shape[0], :]
    carry_scr[...] = carry_scr[...] + jnp.sum(onehot, axis=0, keepdims=True)


def _post_attn(o_dil, o_fox, x2, w_out_bf, g_a, sc_m, sh_m, ln_g, ln_b, w_router, b_router, seq):
    t, d_model = x2.shape
    dil_width = o_dil.shape[1]
    n_experts = w_router.shape[1]
    tm, sub = 256, 256
    wr = _split_weight(jnp.pad(w_router, ((0, 0), (0, LANES - n_experts))))
    br = jnp.pad(b_router, (0, LANES - n_experts)).reshape(1, LANES)
    kern = functools.partial(_post_attn_kernel, n_experts=n_experts, top_k=TOP_K, dil_width=dil_width,
                             sub=sub)
    row = lambda w: pl.BlockSpec((tm, w), lambda i: (i, 0))
    col = pl.BlockSpec((8, tm), lambda i: (0, i))
    const = lambda r, w: pl.BlockSpec((r, w), lambda i: (0, 0))
    bvec = pl.BlockSpec((1, 1, d_model), lambda i: (i * tm // seq, 0, 0))
    return pl.pallas_call(
        kern,
        grid=(t // tm,),
        in_specs=[row(dil_width), row(o_fox.shape[1]), row(d_model), const(d_model, d_model),
                  bvec, bvec, bvec, const(1, d_model), const(1, d_model),
                  const(d_model, 2 * LANES), const(1, LANES)],
        out_specs=[row(d_model), row(d_model), col, row(LANES), col,
                   pl.BlockSpec((8, LANES), lambda i: (0, 0))],
        out_shape=[jax.ShapeDtypeStruct((t, d_model), F32),
                   jax.ShapeDtypeStruct((t, d_model), F32),
                   jax.ShapeDtypeStruct((8, t), I32),
                   jax.ShapeDtypeStruct((t, LANES), F32),
                   jax.ShapeDtypeStruct((8, t), I32),
                   jax.ShapeDtypeStruct((8, LANES), F32)],
        scratch_shapes=[pltpu.VMEM((1, LANES), F32)],
        compiler_params=_cparams(("arbitrary",)),
        name="post_attn",
    )(o_dil, o_fox, x2, w_out_bf, g_a, sc_m, sh_m, ln_g.reshape(1, -1), ln_b.reshape(1, -1),
      wr, br)


def _scatter_kernel(dest_ref, fill_lo_ref, fill_hi_ref, used_ref, hp_ref, xs_ref, zbuf, sem, zsem,
                    *, top_k):
    i = pl.program_id(0)
    tm = hp_ref.shape[0]
    rb = ROW_BLOCK

    @pl.when(i == 0)
    def _():
        zbuf[...] = jnp.zeros(zbuf.shape, zbuf.dtype)

        def tails(wait):
            def row(r, c):
                copy = pltpu.make_async_copy(zbuf.at[pl.ds(0, 1)], xs_ref.at[pl.ds(r, 1)], zsem.at[0])
                copy.wait() if wait else copy.start()
                return c

            def expert(e, c):
                return lax.fori_loop(fill_lo_ref[e], fill_hi_ref[e], row, c)

            lax.fori_loop(0, fill_lo_ref.shape[0], expert, 0)

        def rest(wait):
            def blk(b, c):
                copy = pltpu.make_async_copy(zbuf, xs_ref.at[pl.ds(pl.multiple_of(b * rb, rb), rb)],
                                             zsem.at[1])
                copy.wait() if wait else copy.start()
                return c

            lax.fori_loop(used_ref[0], xs_ref.shape[0] // rb, blk, 0)

        tails(False)
        rest(False)
        tails(True)
        rest(True)

    n_tokens = dest_ref.shape[0] // top_k
    base = i * tm
    for t in range(tm):
        for k in range(top_k):
            d = dest_ref[base + (k * n_tokens + t)]
            pltpu.make_async_copy(hp_ref.at[pl.ds(t, 1)], xs_ref.at[pl.ds(d, 1)], sem).start()

    def drain(t, carry):
        for k in range(top_k):
            pltpu.make_async_copy(hp_ref.at[pl.ds(0, 1)], xs_ref.at[pl.ds(0, 1)], sem).wait()
        return carry

    lax.fori_loop(0, tm, drain, 0, unroll=8)


def _scatter_rows(dest_flat, fill_lo, fill_hi, used_blocks, hp, n_rows):
    t, w = hp.shape
    tm = 256
    kern = functools.partial(_scatter_kernel, top_k=TOP_K)
    return pl.pallas_call(
        kern,
        grid_spec=pltpu.PrefetchScalarGridSpec(
            num_scalar_prefetch=4,
            grid=(t // tm,),
            in_specs=[pl.BlockSpec((tm, w), lambda i, *_: (i, 0))],
            out_specs=pl.BlockSpec(memory_space=pl.ANY),
            scratch_shapes=[pltpu.VMEM((ROW_BLOCK, w), F32),
                            pltpu.SemaphoreType.DMA(()),
                            pltpu.SemaphoreType.DMA((2,))]),
        out_shape=jax.ShapeDtypeStruct((n_rows, w), F32),
        compiler_params=_cparams(("arbitrary",)),
        name="scatter_rows",
    )(dest_flat, fill_lo, fill_hi, used_blocks, hp)


def _moe_kernel(iexp_ref, irow_ref, inu_ref, used_ref, xs_ref, wg_ref, wu_ref, wd_ref, bg_ref, bu_ref,
                bd_ref, y_ref, xbuf, yacc, sem_in, sem_out, *, n_chunks):
    del iexp_ref
    it = pl.program_id(0)
    f = pl.program_id(1)
    n_items = pl.num_programs(0)
    nu = inu_ref[it]
    rb = ROW_BLOCK
    rows_of = lambda k, n=1: pl.ds(pl.multiple_of(k * rb, rb), n * rb)

    def x_copy(item, k):
        src = pl.ds(pl.multiple_of(irow_ref[item] + k * rb, rb), rb)
        return pltpu.make_async_copy(xs_ref.at[src], xbuf.at[rows_of(k)], sem_in)

    def y_copy(k):
        dst = pl.ds(pl.multiple_of(irow_ref[it] + k * rb, rb), rb)
        return pltpu.make_async_copy(yacc.at[rows_of(k)], y_ref.at[dst], sem_out)

    def for_units(n, fn):
        def body(k, c):
            fn(k)
            return c
        lax.fori_loop(0, n, body, 0)

    def init_unit(k):
        yacc[rows_of(k), :] = jnp.broadcast_to(bd_ref[0], (rb, yacc.shape[1]))

    @pl.when(nu > 0)
    def _():
        first = f == 0
        last = f == n_chunks - 1

        @pl.when(first & (it == 0))
        def _():
            for_units(nu, lambda k: x_copy(0, k).start())

        @pl.when(first)
        def _():
            for_units(nu, init_unit)

        wg = wg_ref[0].astype(BF16)
        wu = wu_ref[0].astype(BF16)
        wd = wd_ref[0].astype(BF16)
        bg = bg_ref[0]
        bu = bu_ref[0]

        @pl.when(first)
        def _():
            for_units(nu, lambda k: x_copy(it, k).wait())

        def block(k, n):
            xb = xbuf[rows_of(k, n), :].astype(BF16)
            g = _dot(xb, wg) + bg
            u = _dot(xb, wu) + bu
            g = jnp.minimum(g, SWIGLU_LIMIT)
            u = jnp.clip(u, -SWIGLU_LIMIT, SWIGLU_LIMIT)
            act = (u + 1.0) * (g * jax.nn.sigmoid(SWIGLU_ALPHA * g))
            yacc[rows_of(k, n), :] += _dot(act.astype(BF16), wd)

        def quad(t, c):
            block(4 * t, 2)
            block(4 * t + 2, 2)
            return c

        n_quads = lax.shift_right_logical(nu, 2)
        lax.fori_loop(0, n_quads, quad, 0)
        done = n_quads * 4

        @pl.when((nu & 2) != 0)
        def _():
            block(done, 2)

        @pl.when((nu & 1) != 0)
        def _():
            block(done + (nu & 2), 1)

        @pl.when(last)
        def _():
            for_units(nu, lambda k: y_copy(k).start())
            nxt = jnp.minimum(it + 1, n_items - 1)
            n_next = jnp.where(it + 1 < n_items, inu_ref[nxt], 0)
            for_units(n_next, lambda k: x_copy(nxt, k).start())
            for_units(nu, lambda k: y_copy(k).wait())

    @pl.when((it == n_items - 1) & (f == n_chunks - 1))
    def _():
        first = used_ref[0]
        last = y_ref.shape[0] // rb
        dst_of = lambda b: pl.ds(pl.multiple_of(b * rb, rb), rb)
        yacc[pl.ds(0, rb), :] = jnp.zeros((rb, yacc.shape[1]), yacc.dtype)

        def start(b, c):
            pltpu.make_async_copy(yacc.at[pl.ds(0, rb)], y_ref.at[dst_of(b)], sem_out).start()
            return c

        def wait(b, c):
            pltpu.make_async_copy(yacc.at[pl.ds(0, rb)], y_ref.at[dst_of(b)], sem_out).wait()
            return c

        lax.fori_loop(first, last, start, 0)
        lax.fori_loop(first, last, wait, 0)


def _moe_experts(item_exp, item_row, item_nblk, used_blocks, xs, w_gate_up, b_gate_up, w_down, b_down):
    n_rows = xs.shape[0]
    n_exp, d_model, two_f = w_gate_up.shape
    d_ff = two_f // 2
    fc = FF_CHUNK
    n_chunks = d_ff // fc
    n_items = item_exp.shape[0]
    kern = functools.partial(_moe_kernel, n_chunks=n_chunks)

    def fsel(it, f, nbk):
        return jnp.where(nbk[it] > 0, f, n_chunks - 1)

    return pl.pallas_call(
        kern,
        grid_spec=pltpu.PrefetchScalarGridSpec(
            num_scalar_prefetch=4,
            grid=(n_items, n_chunks),
            in_specs=[
                pl.BlockSpec(memory_space=pl.ANY),
                pl.BlockSpec((1, d_model, fc), lambda it, f, ex, ro, nbk, us: (ex[it], 0, fsel(it, f, nbk))),
                pl.BlockSpec((1, d_model, fc),
                             lambda it, f, ex, ro, nbk, us: (ex[it], 0, n_chunks + fsel(it, f, nbk))),
                pl.BlockSpec((1, fc, d_model), lambda it, f, ex, ro, nbk, us: (ex[it], fsel(it, f, nbk), 0)),
                pl.BlockSpec((1, 1, fc), lambda it, f, ex, ro, nbk, us: (ex[it], 0, fsel(it, f, nbk))),
                pl.BlockSpec((1, 1, fc),
                             lambda it, f, ex, ro, nbk, us: (ex[it], 0, n_chunks + fsel(it, f, nbk))),
                pl.BlockSpec((1, 1, d_model), lambda it, f, ex, ro, nbk, us: (ex[it], 0, 0)),
            ],
            out_specs=pl.BlockSpec(memory_space=pl.ANY),
            scratch_shapes=[pltpu.VMEM((ITEM_ROWS, d_model), F32),
                            pltpu.VMEM((ITEM_ROWS, d_model), F32),
                            pltpu.SemaphoreType.DMA(()),
                            pltpu.SemaphoreType.DMA(())]),
        out_shape=jax.ShapeDtypeStruct((n_rows, d_model), F32),
        compiler_params=_cparams(("arbitrary", "arbitrary")),
        name="moe_experts",
    )(item_exp, item_row, item_nblk, used_blocks, xs, w_gate_up, w_gate_up, w_down,
      b_gate_up.reshape(n_exp, 1, two_f), b_gate_up.reshape(n_exp, 1, two_f),
      b_down.reshape(n_exp, 1, d_model))


def _combine_kernel(dest_ref, y_ref, x1_ref, gate_ref, gm_ref, g_ref, b_ref, o_ref, ybuf, sem, *, top_k):
    i = pl.program_id(0)
    n_steps = pl.num_programs(0)
    tm = x1_ref.shape[0]
    slot = i % 2

    def gather(step, slot_):
        n_tokens = dest_ref.shape[0] // top_k
        base = step * tm
        for t in range(tm):
            for k in range(top_k):
                d = dest_ref[base + (k * n_tokens + t)]
                pltpu.make_async_copy(y_ref.at[pl.ds(d, 1)], ybuf.at[slot_, pl.ds(k * tm + t, 1)],
                                      sem.at[slot_]).start()

    @pl.when(i == 0)
    def _():
        gather(0, 0)

    @pl.when(i + 1 < n_steps)
    def _():
        gather(i + 1, 1 - slot)

    pltpu.make_async_copy(y_ref.at[pl.ds(0, top_k * tm)], ybuf.at[slot], sem.at[slot]).wait()
    gates = gate_ref[...]
    ffn = gates[:, 0:1] * ybuf[slot, pl.ds(0, tm), :]
    for k in range(1, top_k):
        ffn = ffn + gates[:, k:k + 1] * ybuf[slot, pl.ds(k * tm, tm), :]
    o_ref[...] = _layer_norm(DEEPNORM_ALPHA * x1_ref[...] + gm_ref[0] * ffn, g_ref[...], b_ref[...])


def _combine(dest_flat, y, x1, gates, g_m, ln_g, ln_b, seq):
    t, d_model = x1.shape
    tm = 128
    kern = functools.partial(_combine_kernel, top_k=TOP_K)
    return pl.pallas_call(
        kern,
        grid_spec=pltpu.PrefetchScalarGridSpec(
            num_scalar_prefetch=1,
            grid=(t // tm,),
            in_specs=[pl.BlockSpec(memory_space=pl.ANY),
                      pl.BlockSpec((tm, d_model), lambda i, d: (i, 0)),
                      pl.BlockSpec((tm, LANES), lambda i, d: (i, 0)),
                      pl.BlockSpec((1, 1, d_model), lambda i, d: (i * tm // seq, 0, 0)),
                      pl.BlockSpec((1, d_model), lambda i, d: (0, 0)),
                      pl.BlockSpec((1, d_model), lambda i, d: (0, 0))],
            out_specs=pl.BlockSpec((tm, d_model), lambda i, d: (i, 0)),
            scratch_shapes=[pltpu.VMEM((2, TOP_K * tm, d_model), F32),
                            pltpu.SemaphoreType.DMA((2,))]),
        out_shape=jax.ShapeDtypeStruct((t, d_model), F32),
        compiler_params=_cparams(("arbitrary",)),
        name="combine_ln2",
    )(dest_flat, y, x1, gates, g_m, ln_g.reshape(1, -1), ln_b.reshape(1, -1))


def _routing_plan(counts, idx, rank, n_tokens):
    n_exp = counts.shape[0]
    padded = (counts + ROW_BLOCK - 1) // ROW_BLOCK * ROW_BLOCK
    pad_end = jnp.cumsum(padded)
    pad_start = pad_end - padded
    dest = (pad_start[idx] + rank).reshape(-1).astype(I32)

    max_rows = (n_tokens * TOP_K + n_exp * (ROW_BLOCK - 1) + ROW_BLOCK - 1) // ROW_BLOCK * ROW_BLOCK
    max_items = n_exp + max_rows // ITEM_ROWS
    items_per = (padded + ITEM_ROWS - 1) // ITEM_ROWS
    items_end = jnp.cumsum(items_per)
    total = items_end[-1]
    it = jnp.arange(max_items, dtype=I32)
    last = jnp.maximum(total - 1, 0)
    it_c = jnp.minimum(it, last)
    exp_of = jnp.sum((items_end[None, :] <= it_c[:, None]).astype(I32), axis=1)
    exp_of = jnp.minimum(exp_of, n_exp - 1)
    local = it_c - (items_end[exp_of] - items_per[exp_of])
    row0 = pad_start[exp_of] + local * ITEM_ROWS
    rows = jnp.minimum(padded[exp_of] - local * ITEM_ROWS, ITEM_ROWS)
    nblk = jnp.where(it < total, rows // ROW_BLOCK, 0)
    used_blocks = (pad_end[-1:] // ROW_BLOCK).astype(I32)
    fill = ((pad_start + counts).astype(I32), pad_end.astype(I32))
    return dest, fill, (exp_of.astype(I32), row0.astype(I32), nblk.astype(I32), used_blocks), max_rows


def kernel(x, c, w_ada, b_ada, w_in, b_forget, w_out, ln1_g, ln1_b, w_router, b_router,
           w_gate_up, b_gate_up, w_down, b_down, ln2_g, ln2_b):
    nb, seq, d_model = x.shape
    n_heads = d_model // HEAD_DIM
    n_dil = n_heads // 2
    n_fox = n_heads - n_dil
    dil_width, fox_width = n_dil * HEAD_DIM, n_fox * HEAD_DIM
    t = nb * seq
    layer = 0

    mod = _ada_mod(c, w_ada[layer], b_ada[layer])
    sh_a, sc_a, g_a, sh_m, sc_m, g_m = [m.reshape(nb, 1, d_model) for m in jnp.split(mod, 6, axis=-1)]

    x2 = x.reshape(t, d_model)
    qkv_dil, qkv_fox, f_logit = _in_proj(x2, sc_a, sh_a, w_in, layer, seq, dil_width, fox_width, n_fox)

    slopes = 2.0 ** (-8.0 * jnp.arange(1, n_dil + 1, dtype=F32) / n_dil)
    o_dil = _dilated_attention(qkv_dil.reshape(nb, seq, 3 * dil_width), slopes, nb, seq, n_dil)

    fox_blk = 256
    cum = _forget_cumsum(f_logit, b_forget[layer], nb, seq).reshape(
        nb, n_fox, seq // fox_blk, 1, fox_blk)
    o_fox = _fox_attention(qkv_fox.reshape(nb, seq, 3 * fox_width), cum, nb, seq, n_fox, fox_blk)

    x1, hp, idx, gates, rank, cnt = _post_attn(
        o_dil.reshape(t, dil_width), o_fox.reshape(t, fox_width), x2, w_out[layer].astype(BF16),
        g_a, sc_m, sh_m, ln1_g[layer], ln1_b[layer], w_router[layer], b_router[layer], seq)

    counts = cnt[0, :N_EXPERTS].astype(I32)
    dest, fill, items, max_rows = _routing_plan(counts, idx[:TOP_K], rank[:TOP_K], t)
    xs = _scatter_rows(dest, *fill, items[3], hp, max_rows)
    y = _moe_experts(*items, xs, w_gate_up[layer], b_gate_up[layer], w_down[layer], b_down[layer])
    out = _combine(dest, y, x1, gates, g_m, ln2_g[layer], ln2_b[layer], seq)
    return out.reshape(nb, seq, d_model)
```

```python
import functools
import math

import jax
import jax.numpy as jnp
from jax import lax
from jax.experimental import pallas as pl
from jax.experimental.pallas import tpu as pltpu

F32 = jnp.float32
BF16 = jnp.bfloat16
I32 = jnp.int32
U32 = jnp.uint32

HEAD_DIM = 128
Q_BLOCK = 128
DIL_PATTERNS = ((128, 1), (512, 4), (2048, 16))
N_EXPERTS = 32
TOP_K = 4
SWIGLU_LIMIT = 7.0
SWIGLU_ALPHA = 1.702
LN_EPS = 1e-5
DEPTH = 1
DEEPNORM_ALPHA = (2.0 * DEPTH) ** 0.25

LOG2E = math.log2(math.e)
DIL_GROUP = 8
DEINT = 4
FOX_HEADS_PER_STEP = 4

LANES = 128
VMEM_LIMIT = 56 * 1024 * 1024

ROW_BLOCK = 128
ITEM_ROWS = 1536
FF_CHUNK = 256


def _cparams(sem):
    return pltpu.CompilerParams(dimension_semantics=sem, vmem_limit_bytes=VMEM_LIMIT)


def _dot(a, b):
    return jnp.dot(a, b, preferred_element_type=F32)


def _dot_nt(a, b):
    return lax.dot_general(a, b, (((1,), (1,)), ((), ())), preferred_element_type=F32)


def _split_bf16(x):
    hi = x.astype(BF16)
    lo = (x - hi.astype(F32)).astype(BF16)
    return hi, lo


def _split_weight(w):
    return jnp.concatenate(_split_bf16(w), axis=1)


def _dot_split(x_hi, x_lo, w_cat):
    r = _dot(x_hi, w_cat) + _dot(x_lo, w_cat)
    return r[:, :LANES] + r[:, LANES:]


def _layer_norm(y, g, b):
    mu = jnp.mean(y, axis=-1, keepdims=True)
    yc = y - mu
    var = jnp.mean(yc * yc, axis=-1, keepdims=True)
    return yc * lax.rsqrt(var + LN_EPS) * g + b


def _ada_kernel(ct_ref, w_ref, b_ref, o_ref):
    ct = ct_ref[...]
    s = ct * jax.nn.sigmoid(ct)
    d_model, nb = ct.shape
    tn = w_ref.shape[1]
    sb = [jnp.broadcast_to(s[:, b:b + 1], (d_model, LANES)) for b in range(nb)]
    for j in range(tn // LANES):
        cols = slice(j * LANES, (j + 1) * LANES)
        w = w_ref[:, cols]
        for b in range(nb):
            acc = jnp.sum(w * sb[b], axis=0, keepdims=True)
            o_ref[b:b + 1, cols] = acc + b_ref[:, cols]


def _ada_mod(c, w, b):
    nb, d_model = c.shape
    n = w.shape[1]
    tn = 1536 if n % 1536 == 0 else LANES
    return pl.pallas_call(
        _ada_kernel,
        grid=(n // tn,),
        in_specs=[pl.BlockSpec((d_model, nb), lambda j: (0, 0)),
                  pl.BlockSpec((d_model, tn), lambda j: (0, j)),
                  pl.BlockSpec((1, tn), lambda j: (0, j))],
        out_specs=pl.BlockSpec((nb, tn), lambda j: (0, j)),
        out_shape=jax.ShapeDtypeStruct((nb, n), F32),
        compiler_params=_cparams(("arbitrary",)),
        name="ada_mod",
    )(c.T, w, b.reshape(1, n))


def _inproj_kernel(x_ref, sc_ref, sh_ref, w_ref, wf_ref,
                   odil_ref, ofox_ref, of_ref, h_scr, *, n_dil_tiles, q_tiles, scale):
    j = pl.program_id(1)

    @pl.when(j == 0)
    def _():
        h = x_ref[...] * (1.0 + sc_ref[0]) + sh_ref[0]
        h_hi, h_lo = _split_bf16(h)
        h_scr[...] = h_hi
        of_ref[...] = _dot_split(h_hi, h_lo, wf_ref[...])

    acc = _dot(h_scr[...], w_ref[...])
    is_q =(j < q_tiles) | ((j >= n_dil_tiles) & (j < n_dil_tiles + q_tiles))
    acc = acc * jnp.where(is_q, scale, 1.0).astype(F32)

    @pl.when(j < n_dil_tiles)
    def _():
        odil_ref[...] = acc

    @pl.when(j >= n_dil_tiles)
    def _():
        ofox_ref[...] = acc.astype(BF16)


def _in_proj(x2, sc, sh, w_in, layer, seq, dil_width, fox_width, n_fox_heads):
    t, d_model = x2.shape
    tm, tn = min(1024, seq), min(1024, dil_width)
    n_dil_tiles = 3 * dil_width // tn
    n_fox_tiles = 3 * fox_width // tn
    q_tiles = dil_width // tn
    assert dil_width == fox_width and dil_width % tn == 0 and seq % tm == 0
    main = 3 * dil_width + 3 * fox_width
    wf = _split_weight(jnp.pad(w_in[layer, :, main:], ((0, 0), (0, LANES - n_fox_heads))))
    kern = functools.partial(_inproj_kernel, n_dil_tiles=n_dil_tiles, q_tiles=q_tiles,
                             scale=LOG2E / math.sqrt(HEAD_DIM))
    bidx = lambda i, j: (i * tm // seq, 0, 0)
    return pl.pallas_call(
        kern,
        grid=(t // tm, n_dil_tiles + n_fox_tiles),
        in_specs=[pl.BlockSpec((tm, d_model), lambda i, j: (i, 0)),
                  pl.BlockSpec((1, 1, d_model), bidx),
                  pl.BlockSpec((1, 1, d_model), bidx),
                  pl.BlockSpec((None, d_model, tn), lambda i, j: (layer, 0, j)),
                  pl.BlockSpec((d_model, 2 * LANES), lambda i, j: (0, 0))],
        out_specs=[pl.BlockSpec((tm, tn), lambda i, j: (i, jnp.minimum(j, n_dil_tiles - 1))),
                   pl.BlockSpec((tm, tn), lambda i, j: (i, jnp.maximum(j - n_dil_tiles, 0))),
                   pl.BlockSpec((tm, LANES), lambda i, j: (i, 0))],
        out_shape=[jax.ShapeDtypeStruct((t, 3 * dil_width), F32),
                   jax.ShapeDtypeStruct((t, 3 * fox_width), BF16),
                   jax.ShapeDtypeStruct((t, LANES), F32)],
        scratch_shapes=[pltpu.VMEM((tm, d_model), BF16)],
        compiler_params=_cparams(("arbitrary", "arbitrary")),
        name="in_proj",
    )(x2, sc, sh, w_in.astype(BF16), wf)


def _cum_kernel(f_ref, b_ref, o_ref):
    hf = o_ref.shape[1]
    z = f_ref[...].T[:hf, :] + b_ref[...]
    c = jnp.minimum(z, 0.0) - jnp.log1p(jnp.exp(-jnp.abs(z)))
    seq = z.shape[1]
    lane = lax.broadcasted_iota(I32, z.shape, 1)
    shift = 1
    while shift < seq:
        c = c + jnp.where(lane >= shift, pltpu.roll(c, shift, axis=1), 0.0)
        shift *= 2
    o_ref[0] = c * LOG2E


def _forget_cumsum(f_logit, b_forget, nb, seq):
    hf = b_forget.shape[0]
    return pl.pallas_call(
        _cum_kernel,
        grid=(nb,),
        in_specs=[pl.BlockSpec((seq, LANES), lambda b: (b, 0)),
                  pl.BlockSpec((hf, 1), lambda b: (0, 0))],
        out_specs=pl.BlockSpec((1, hf, seq), lambda b: (b, 0, 0)),
        out_shape=jax.ShapeDtypeStruct((nb, hf, seq), F32),
        compiler_params=_cparams(("arbitrary",)),
        name="forget_cumsum",
    )(f_logit, b_forget.reshape(hf, 1))


def _dil_kernel(slope_ref, q_ref, k_ref, v_ref, o_ref, q4_ref, k4_ref, v4_ref, acc_ref, l_ref, m_ref,
                bias_ref, *, patterns, seq):
    slope = slope_ref[pl.program_id(1)]
    qb = Q_BLOCK
    row = lax.broadcasted_iota(I32, (qb, 2 * qb), 0)
    col = lax.broadcasted_iota(I32, (qb, 2 * qb), 1)
    delta = row + qb - col
    is_prev = col < qb
    neg_inf = F32(-jnp.inf)
    ones = jnp.ones((2 * qb, HEAD_DIM), BF16)

    sub = seq // DEINT
    chunk = min(256, sub)
    for r4 in range(DEINT):
        for c0 in range(0, sub, chunk):
            src = pl.ds(r4 + DEINT * c0, chunk, stride=DEINT)
            dst = pl.ds(r4 * sub + c0, chunk)
            q4_ref[dst, :] = q_ref[0, src, :]
            k4_ref[dst, :] = k_ref[0, src, :]
            v4_ref[dst, :] = v_ref[0, src, :]

    for pi, (window, dil) in enumerate(sorted(patterns, key=lambda wd: -wd[1])):
        reach = window // dil
        nblk = seq // dil // qb
        n_blocks = dil * nblk
        assert reach <= qb and nblk * qb * dil == seq and n_blocks % DIL_GROUP == 0
        sd = slope * (float(dil) * LOG2E)
        bias_ref[...] = jnp.where((delta >= 0) & (delta <= reach), -sd * delta.astype(F32), neg_inf)

        def block(idx, dil=dil, nblk=nblk, pi=pi):
            r = idx // nblk
            n = idx - r * nblk
            start = r + n * (qb * dil)
            cur = pl.ds(start, qb, stride=dil) if dil > 1 else pl.ds(pl.multiple_of(start, qb), qb)
            if dil % DEINT == 0:
                step = dil // DEINT
                first = (r % DEINT) * sub + r // DEINT
                lstart = first + n * (qb * step)
                lprev = jnp.maximum(lstart - qb * step, first)
                if step > 1:
                    src_cur, src_prev = pl.ds(lstart, qb, stride=step), pl.ds(lprev, qb, stride=step)
                else:
                    src_cur = pl.ds(pl.multiple_of(lstart, qb), qb)
                    src_prev = pl.ds(pl.multiple_of(lprev, qb), qb)
                q = q4_ref[src_cur, :].astype(BF16)
                kk = jnp.concatenate([k4_ref[src_prev, :], k4_ref[src_cur, :]], axis=0).astype(BF16)
                vv = jnp.concatenate([v4_ref[src_prev, :], v4_ref[src_cur, :]], axis=0).astype(BF16)
            else:
                pstart = jnp.maximum(start - qb * dil, r)
                prev = (pl.ds(pstart, qb, stride=dil) if dil > 1
                        else pl.ds(pl.multiple_of(pstart, qb), qb))
                q = q_ref[0, cur, :].astype(BF16)
                kk = jnp.concatenate([k_ref[0, prev, :], k_ref[0, cur, :]], axis=0).astype(BF16)
                vv = jnp.concatenate([v_ref[0, prev, :], v_ref[0, cur, :]], axis=0).astype(BF16)
            s = _dot_nt(q, kk) + bias_ref[...]
            s = jnp.where(is_prev & (n == 0), neg_inf, s)
            m_blk = jnp.max(s, axis=1, keepdims=True)
            p = jnp.exp2(s - m_blk)
            pv = _dot(p.astype(BF16), jnp.concatenate([vv, ones], axis=1))
            o_blk, l_blk = pv[:, :HEAD_DIM], pv[:, HEAD_DIM:]
            if pi == 0:
                m_ref[cur, :] = jnp.broadcast_to(m_blk, (qb, LANES))
                acc_ref[cur, :] = o_blk
                l_ref[cur, :] = l_blk
            else:
                m_old = m_ref[cur, :]
                m_new = jnp.maximum(m_old, m_blk)
                a_old = jnp.exp2(m_old - m_new)
                a_blk = jnp.exp2(m_blk - m_new)
                m_ref[cur, :] = m_new
                acc_ref[cur, :] = acc_ref[cur, :] * a_old + o_blk * a_blk
                l_ref[cur, :] = l_ref[cur, :] * a_old + l_blk * a_blk

        def body(g, carry, block=block):
            for u in range(DIL_GROUP):
                block(g * DIL_GROUP + u)
            return carry

        lax.fori_loop(0, n_blocks // DIL_GROUP, body, 0)

    chunk = 512
    for c0 in range(0, seq, chunk):
        rows = slice(c0, c0 + chunk)
        o_ref[0, rows, :] = (acc_ref[rows, :] / l_ref[rows, :]).astype(o_ref.dtype)


def _dilated_attention(qkv, slopes, nb, seq, n_heads):
    dh = HEAD_DIM
    kern = functools.partial(_dil_kernel, patterns=DIL_PATTERNS, seq=seq)
    blk = lambda off: pl.BlockSpec((1, seq, dh), lambda b, h, s: (b, 0, off + h))
    return pl.pallas_call(
        kern,
        grid_spec=pltpu.PrefetchScalarGridSpec(
            num_scalar_prefetch=1,
            grid=(nb, n_heads),
            in_specs=[blk(0), blk(n_heads), blk(2 * n_heads)],
            out_specs=pl.BlockSpec((1, seq, dh), lambda b, h, s: (b, 0, h)),
            scratch_shapes=[pltpu.VMEM((seq, dh), F32),
                            pltpu.VMEM((seq, dh), F32),
                            pltpu.VMEM((seq, dh), F32),
                            pltpu.VMEM((seq, dh), F32),
                            pltpu.VMEM((seq, LANES), F32),
                            pltpu.VMEM((seq, LANES), F32),
                            pltpu.VMEM((Q_BLOCK, 2 * Q_BLOCK), F32)]),
        out_shape=jax.ShapeDtypeStruct((nb, seq, n_heads * dh), BF16),
        compiler_params=_cparams(("arbitrary", "arbitrary")),
        name="dilated_attn",
    )(slopes, qkv, qkv, qkv)


def _fox_kernel(q_ref, k_ref, v_ref, c_ref, o_ref, m_scr, acc_scr, *, blk, hb):
    i = pl.program_id(2)
    dh = HEAD_DIM
    row = lax.broadcasted_iota(I32, (blk, blk), 0)
    col = lax.broadcasted_iota(I32, (blk, blk), 1)
    m_scr[...] = jnp.full(m_scr.shape, -jnp.inf, F32)
    acc_scr[...] = jnp.zeros(acc_scr.shape, F32)
    ones = jnp.ones((blk, dh), BF16)
    n_slab = blk // LANES

    def step(j, masked):
        rows = pl.ds(pl.multiple_of(j * blk, blk), blk)
        for h in range(hb):
            cols = slice(h * dh, (h + 1) * dh)
            s = _dot_nt(q_ref[0, :, cols], k_ref[0, rows, cols]) - c_ref[0, h, j]
            if masked:
                s = jnp.where(col <= row, s, -jnp.inf)
            m_old = m_scr[h]
            m_new = jnp.maximum(m_old, jnp.max(s, axis=1, keepdims=True))
            alpha = jnp.exp2(m_old - m_new)
            p = jnp.concatenate(
                [jnp.exp2(s[:, t * LANES:(t + 1) * LANES] - m_new) for t in range(n_slab)], axis=1)
            pv = _dot(p.astype(BF16), jnp.concatenate([v_ref[0, rows, cols], ones], axis=1))
            acc_scr[h] = jnp.concatenate([alpha, alpha], axis=1) * acc_scr[h] + pv
            m_scr[h] = m_new

    def body(j, carry):
        step(j, False)
        return carry

    lax.fori_loop(0, i, body, 0)
    step(i, True)
    for h in range(hb):
        acc = acc_scr[h]
        o_ref[0, :, h * dh:(h + 1) * dh] = (acc[:, :dh] / acc[:, dh:]).astype(o_ref.dtype)


def _fox_attention(qkv, cum, nb, seq, n_heads, blk):
    dh = HEAD_DIM
    hb = min(FOX_HEADS_PER_STEP, n_heads)
    assert n_heads % hb == 0
    ng = n_heads // hb
    w = hb * dh
    kern = functools.partial(_fox_kernel, blk=blk, hb=hb)
    return pl.pallas_call(
        kern,
        grid=(nb, ng, seq // blk),
        in_specs=[pl.BlockSpec((1, blk, w), lambda b, g, i: (b, i, g)),
                  pl.BlockSpec((1, seq, w), lambda b, g, i: (b, 0, ng + g)),
                  pl.BlockSpec((1, seq, w), lambda b, g, i: (b, 0, 2 * ng + g)),
                  pl.BlockSpec((1, hb, seq // blk, 1, blk), lambda b, g, i: (b, g, 0, 0, 0))],
        out_specs=pl.BlockSpec((1, blk, w), lambda b, g, i: (b, i, g)),
        out_shape=jax.ShapeDtypeStruct((nb, seq, n_heads * dh), BF16),
        scratch_shapes=[pltpu.VMEM((hb, blk, LANES), F32),
                        pltpu.VMEM((hb, blk, 2 * dh), F32)],
        compiler_params=_cparams(("arbitrary", "arbitrary", "arbitrary")),
        name="fox_attn",
    )(qkv, qkv, qkv, cum)


def _post_attn_kernel(od_ref, of_ref, x_ref, w_ref, ga_ref, scm_ref, shm_ref, g_ref, b_ref,
                      wr_ref, br_ref,
                      x1_ref, hp_ref, idx_ref, gate_ref, rank_ref, cnt_ref, carry_scr,
                      *, n_experts, top_k, dil_width, sub):
    i = pl.program_id(0)

    @pl.when(i == 0)
    def _():
        carry_scr[...] = jnp.zeros_like(carry_scr)

    for r0 in range(0, x_ref.shape[0], sub):
        _post_attn_rows(slice(r0, r0 + sub), od_ref, of_ref, x_ref, w_ref, ga_ref, scm_ref, shm_ref,
                        g_ref, b_ref, wr_ref, br_ref, x1_ref, hp_ref, idx_ref, gate_ref, rank_ref,
                        carry_scr, n_experts=n_experts, top_k=top_k, dil_width=dil_width)
    cnt_ref[...] = jnp.broadcast_to(carry_scr[...], cnt_ref.shape)


def _post_attn_rows(rows, od_ref, of_ref, x_ref, w_ref, ga_ref, scm_ref, shm_ref, g_ref, b_ref,
                    wr_ref, br_ref, x1_ref, hp_ref, idx_ref, gate_ref, rank_ref, carry_scr,
                    *, n_experts, top_k, dil_width):
    attn = (_dot(od_ref[rows, :], w_ref[:dil_width, :]) + _dot(of_ref[rows, :], w_ref[dil_width:, :]))
    x1 = _layer_norm(DEEPNORM_ALPHA * x_ref[rows, :] + ga_ref[0] * attn, g_ref[...], b_ref[...])
    x1_ref[rows, :] = x1
    h = x1 * (1.0 + scm_ref[0]) + shm_ref[0]
    h_hi, h_lo = _split_bf16(h)
    hp_ref[rows, :] = h

    logits = _dot_split(h_hi, h_lo, wr_ref[...]) + br_ref[...]
    tm = logits.shape[0]
    lane = lax.broadcasted_iota(I32, (tm, LANES), 1)
    work = jnp.where(lane < n_experts, logits, -jnp.inf)
    vals, idxs = [], []
    for _ in range(top_k):
        mx = jnp.max(work, axis=1, keepdims=True)
        ix = jnp.min(jnp.where(work == mx, lane, LANES), axis=1, keepdims=True)
        vals.append(mx)
        idxs.append(ix)
        work = jnp.where(lane == ix, -jnp.inf, work)
    exps = [jnp.exp(v - vals[0]) for v in vals]
    denom = exps[0]
    for e in exps[1:]:
        denom = denom + e

    onehot = jnp.zeros((tm, LANES), F32)
    for ix in idxs:
        onehot = onehot + (lane == ix).astype(F32)
    r_i = lax.broadcasted_iota(I32, (tm, tm), 0)
    c_i = lax.broadcasted_iota(I32, (tm, tm), 1)
    tri = (c_i < r_i).astype(BF16)
    before = _dot(tri, onehot.astype(BF16)) + carry_scr[...]

    idx_out = jnp.zeros((tm, LANES), I32)
    gate_out = jnp.zeros((tm, LANES), F32)
    rank_out = jnp.zeros((tm, LANES), I32)
    for k in range(top_k):
        rank_k = jnp.sum(jnp.where(lane == idxs[k], before, 0.0), axis=1, keepdims=True)
        idx_out = jnp.where(lane == k, idxs[k], idx_out)
        gate_out = jnp.where(lane == k, exps[k] / denom, gate_out)
        rank_out = jnp.where(lane == k, rank_k.astype(I32), rank_out)
    gate_ref[rows, :] = gate_out
    idx_ref[:, rows] = idx_out.T[:idx_ref.shape[0], :]
    rank_ref[:, rows] = rank_out.T[:rank_ref.shape[0], :]
    carry_scr[...] = carry_scr[...] + jnp.sum(onehot, axis=0, keepdims=True)


def _post_attn(o_dil, o_fox, x2, w_out_bf, g_a, sc_m, sh_m, ln_g, ln_b, w_router, b_router, seq):
    t, d_model = x2.shape
    dil_width = o_dil.shape[1]
    n_experts = w_router.shape[1]
    tm, sub = 256, 256
    wr = _split_weight(jnp.pad(w_router, ((0, 0), (0, LANES - n_experts))))
    br = jnp.pad(b_router, (0, LANES - n_experts)).reshape(1, LANES)
    kern = functools.partial(_post_attn_kernel, n_experts=n_experts, top_k=TOP_K, dil_width=dil_width,
                             sub=sub)
    row = lambda w: pl.BlockSpec((tm, w), lambda i: (i, 0))
    col = pl.BlockSpec((8, tm), lambda i: (0, i))
    const = lambda r, w: pl.BlockSpec((r, w), lambda i: (0, 0))
    bvec = pl.BlockSpec((1, 1, d_model), lambda i: (i * tm // seq, 0, 0))
    return pl.pallas_call(
        kern,
        grid=(t // tm,),
        in_specs=[row(dil_width), row(o_fox.shape[1]), row(d_model), const(d_model, d_model),
                  bvec, bvec, bvec, const(1, d_model), const(1, d_model),
                  const(d_model, 2 * LANES), const(1, LANES)],
        out_specs=[row(d_model), row(d_model), col, row(LANES), col,
                   pl.BlockSpec((8, LANES), lambda i: (0, 0))],
        out_shape=[jax.ShapeDtypeStruct((t, d_model), F32),
                   jax.ShapeDtypeStruct((t, d_model), F32),
                   jax.ShapeDtypeStruct((8, t), I32),
                   jax.ShapeDtypeStruct((t, LANES), F32),
                   jax.ShapeDtypeStruct((8, t), I32),
                   jax.ShapeDtypeStruct((8, LANES), F32)],
        scratch_shapes=[pltpu.VMEM((1, LANES), F32)],
        compiler_params=_cparams(("arbitrary",)),
        name="post_attn",
    )(o_dil, o_fox, x2, w_out_bf, g_a, sc_m, sh_m, ln_g.reshape(1, -1), ln_b.reshape(1, -1),
      wr, br)


def _scatter_kernel(dest_ref, fill_lo_ref, fill_hi_ref, used_ref, hp_ref, xs_ref, zbuf, sem, zsem,
                    *, top_k):
    i = pl.program_id(0)
    tm = hp_ref.shape[0]
    rb = ROW_BLOCK

    @pl.when(i == 0)
    def _():
        zbuf[...] = jnp.zeros(zbuf.shape, zbuf.dtype)

        def tails(wait):
            def row(r, c):
                copy = pltpu.make_async_copy(zbuf.at[pl.ds(0, 1)], xs_ref.at[pl.ds(r, 1)], zsem.at[0])
                copy.wait() if wait else copy.start()
                return c

            def expert(e, c):
                return lax.fori_loop(fill_lo_ref[e], fill_hi_ref[e], row, c)

            lax.fori_loop(0, fill_lo_ref.shape[0], expert, 0)

        def rest(wait):
            def blk(b, c):
                copy = pltpu.make_async_copy(zbuf, xs_ref.at[pl.ds(pl.multiple_of(b * rb, rb), rb)],
                                             zsem.at[1])
                copy.wait() if wait else copy.start()
                return c

            lax.fori_loop(used_ref[0], xs_ref.shape[0] // rb, blk, 0)

        tails(False)
        rest(False)
        tails(True)
        rest(True)

    n_tokens = dest_ref.shape[0] // top_k
    base = i * tm
    for t in range(tm):
        for k in range(top_k):
            d = dest_ref[base + (k * n_tokens + t)]
            pltpu.make_async_copy(hp_ref.at[pl.ds(t, 1)], xs_ref.at[pl.ds(d, 1)], sem).start()

    def drain(t, carry):
        for k in range(top_k):
            pltpu.make_async_copy(hp_ref.at[pl.ds(0, 1)], xs_ref.at[pl.ds(0, 1)], sem).wait()
        return carry

    lax.fori_loop(0, tm, drain, 0, unroll=8)


def _scatter_rows(dest_flat, fill_lo, fill_hi, used_blocks, hp, n_rows):
    t, w = hp.shape
    tm = 256
    kern = functools.partial(_scatter_kernel, top_k=TOP_K)
    return pl.pallas_call(
        kern,
        grid_spec=pltpu.PrefetchScalarGridSpec(
            num_scalar_prefetch=4,
            grid=(t // tm,),
            in_specs=[pl.BlockSpec((tm, w), lambda i, *_: (i, 0))],
            out_specs=pl.BlockSpec(memory_space=pl.ANY),
            scratch_shapes=[pltpu.VMEM((ROW_BLOCK, w), F32),
                            pltpu.SemaphoreType.DMA(()),
                            pltpu.SemaphoreType.DMA((2,))]),
        out_shape=jax.ShapeDtypeStruct((n_rows, w), F32),
        compiler_params=_cparams(("arbitrary",)),
        name="scatter_rows",
    )(dest_flat, fill_lo, fill_hi, used_blocks, hp)


def _moe_kernel(iexp_ref, irow_ref, inu_ref, used_ref, xs_ref, wg_ref, wu_ref, wd_ref, bg_ref, bu_ref,
                bd_ref, y_ref, xbuf, yacc, sem_in, sem_out, *, n_chunks):
    del iexp_ref
    it = pl.program_id(0)
    f = pl.program_id(1)
    n_items = pl.num_programs(0)
    nu = inu_ref[it]
    rb = ROW_BLOCK
    rows_of = lambda k, n=1: pl.ds(pl.multiple_of(k * rb, rb), n * rb)

    def x_copy(item, k):
        src = pl.ds(pl.multiple_of(irow_ref[item] + k * rb, rb), rb)
        return pltpu.make_async_copy(xs_ref.at[src], xbuf.at[rows_of(k)], sem_in)

    def y_copy(k):
        dst = pl.ds(pl.multiple_of(irow_ref[it] + k * rb, rb), rb)
        return pltpu.make_async_copy(yacc.at[rows_of(k)], y_ref.at[dst], sem_out)

    def for_units(n, fn):
        def body(k, c):
            fn(k)
            return c
        lax.fori_loop(0, n, body, 0)

    def init_unit(k):
        yacc[rows_of(k), :] = jnp.broadcast_to(bd_ref[0], (rb, yacc.shape[1]))

    @pl.when(nu > 0)
    def _():
        first = f == 0
        last = f == n_chunks - 1

        @pl.when(first & (it == 0))
        def _():
            for_units(nu, lambda k: x_copy(0, k).start())

        @pl.when(first)
        def _():
            for_units(nu, init_unit)

        wg = wg_ref[0].astype(BF16)
        wu = wu_ref[0].astype(BF16)
        wd = wd_ref[0].astype(BF16)
        bg = bg_ref[0]
        bu = bu_ref[0]

        @pl.when(first)
        def _():
            for_units(nu, lambda k: x_copy(it, k).wait())

        def block(k, n):
            xb = xbuf[rows_of(k, n), :].astype(BF16)
            g = _dot(xb, wg) + bg
            u = _dot(xb, wu) + bu
            g = jnp.minimum(g, SWIGLU_LIMIT)
            u = jnp.clip(u, -SWIGLU_LIMIT, SWIGLU_LIMIT)
            act = (u + 1.0) * (g * jax.nn.sigmoid(SWIGLU_ALPHA * g))
            yacc[rows_of(k, n), :] += _dot(act.astype(BF16), wd)

        def quad(t, c):
            block(4 * t, 2)
            block(4 * t + 2, 2)
            return c

        n_quads = lax.shift_right_logical(nu, 2)
        lax.fori_loop(0, n_quads, quad, 0)
        done = n_quads * 4

        @pl.when((nu & 2) != 0)
        def _():
            block(done, 2)

        @pl.when((nu & 1) != 0)
        def _():
            block(done + (nu & 2), 1)

        @pl.when(last)
        def _():
            for_units(nu, lambda k: y_copy(k).start())
            nxt = jnp.minimum(it + 1, n_items - 1)
            n_next = jnp.where(it + 1 < n_items, inu_ref[nxt], 0)
            for_units(n_next, lambda k: x_copy(nxt, k).start())
            for_units(nu, lambda k: y_copy(k).wait())

    @pl.when((it == n_items - 1) & (f == n_chunks - 1))
    def _():
        first = used_ref[0]
        last = y_ref.shape[0] // rb
        dst_of = lambda b: pl.ds(pl.multiple_of(b * rb, rb), rb)
        yacc[pl.ds(0, rb), :] = jnp.zeros((rb, yacc.shape[1]), yacc.dtype)

        def start(b, c):
            pltpu.make_async_copy(yacc.at[pl.ds(0, rb)], y_ref.at[dst_of(b)], sem_out).start()
            return c

        def wait(b, c):
            pltpu.make_async_copy(yacc.at[pl.ds(0, rb)], y_ref.at[dst_of(b)], sem_out).wait()
            return c

        lax.fori_loop(first, last, start, 0)
        lax.fori_loop(first, last, wait, 0)


def _moe_experts(item_exp, item_row, item_nblk, used_blocks, xs, w_gate_up, b_gate_up, w_down, b_down):
    n_rows = xs.shape[0]
    n_exp, d_model, two_f = w_gate_up.shape
    d_ff = two_f // 2
    fc = FF_CHUNK
    n_chunks = d_ff // fc
    n_items = item_exp.shape[0]
    kern = functools.partial(_moe_kernel, n_chunks=n_chunks)

    def fsel(it, f, nbk):
        return jnp.where(nbk[it] > 0, f, n_chunks - 1)

    return pl.pallas_call(
        kern,
        grid_spec=pltpu.PrefetchScalarGridSpec(
            num_scalar_prefetch=4,
            grid=(n_items, n_chunks),
            in_specs=[
                pl.BlockSpec(memory_space=pl.ANY),
                pl.BlockSpec((1, d_model, fc), lambda it, f, ex, ro, nbk, us: (ex[it], 0, fsel(it, f, nbk))),
                pl.BlockSpec((1, d_model, fc),
                             lambda it, f, ex, ro, nbk, us: (ex[it], 0, n_chunks + fsel(it, f, nbk))),
                pl.BlockSpec((1, fc, d_model), lambda it, f, ex, ro, nbk, us: (ex[it], fsel(it, f, nbk), 0)),
                pl.BlockSpec((1, 1, fc), lambda it, f, ex, ro, nbk, us: (ex[it], 0, fsel(it, f, nbk))),
                pl.BlockSpec((1, 1, fc),
                             lambda it, f, ex, ro, nbk, us: (ex[it], 0, n_chunks + fsel(it, f, nbk))),
                pl.BlockSpec((1, 1, d_model), lambda it, f, ex, ro, nbk, us: (ex[it], 0, 0)),
            ],
            out_specs=pl.BlockSpec(memory_space=pl.ANY),
            scratch_shapes=[pltpu.VMEM((ITEM_ROWS, d_model), F32),
                            pltpu.VMEM((ITEM_ROWS, d_model), F32),
                            pltpu.SemaphoreType.DMA(()),
                            pltpu.SemaphoreType.DMA(())]),
        out_shape=jax.ShapeDtypeStruct((n_rows, d_model), F32),
        compiler_params=_cparams(("arbitrary", "arbitrary")),
        name="moe_experts",
    )(item_exp, item_row, item_nblk, used_blocks, xs, w_gate_up, w_gate_up, w_down,
      b_gate_up.reshape(n_exp, 1, two_f), b_gate_up.reshape(n_exp, 1, two_f),
      b_down.reshape(n_exp, 1, d_model))


def _combine_kernel(dest_ref, y_ref, x1_ref, gate_ref, gm_ref, g_ref, b_ref, o_ref, ybuf, sem, *, top_k):
    i = pl.program_id(0)
    n_steps = pl.num_programs(0)
    tm = x1_ref.shape[0]
    slot = i % 2

    def gather(step, slot_):
        n_tokens = dest_ref.shape[0] // top_k
        base = step * tm
        for t in range(tm):
            for k in range(top_k):
                d = dest_ref[base + (k * n_tokens + t)]
                pltpu.make_async_copy(y_ref.at[pl.ds(d, 1)], ybuf.at[slot_, pl.ds(k * tm + t, 1)],
                                      sem.at[slot_]).start()

    @pl.when(i == 0)
    def _():
        gather(0, 0)

    @pl.when(i + 1 < n_steps)
    def _():
        gather(i + 1, 1 - slot)

    pltpu.make_async_copy(y_ref.at[pl.ds(0, top_k * tm)], ybuf.at[slot], sem.at[slot]).wait()
    gates = gate_ref[...]
    ffn = gates[:, 0:1] * ybuf[slot, pl.ds(0, tm), :]
    for k in range(1, top_k):
        ffn = ffn + gates[:, k:k + 1] * ybuf[slot, pl.ds(k * tm, tm), :]
    o_ref[...] = _layer_norm(DEEPNORM_ALPHA * x1_ref[...] + gm_ref[0] * ffn, g_ref[...], b_ref[...])


def _combine(dest_flat, y, x1, gates, g_m, ln_g, ln_b, seq):
    t, d_model = x1.shape
    tm = 128
    kern = functools.partial(_combine_kernel, top_k=TOP_K)
    return pl.pallas_call(
        kern,
        grid_spec=pltpu.PrefetchScalarGridSpec(
            num_scalar_prefetch=1,
            grid=(t // tm,),
            in_specs=[pl.BlockSpec(memory_space=pl.ANY),
                      pl.BlockSpec((tm, d_model), lambda i, d: (i, 0)),
                      pl.BlockSpec((tm, LANES), lambda i, d: (i, 0)),
                      pl.BlockSpec((1, 1, d_model), lambda i, d: (i * tm // seq, 0, 0)),
                      pl.BlockSpec((1, d_model), lambda i, d: (0, 0)),
                      pl.BlockSpec((1, d_model), lambda i, d: (0, 0))],
            out_specs=pl.BlockSpec((tm, d_model), lambda i, d: (i, 0)),
            scratch_shapes=[pltpu.VMEM((2, TOP_K * tm, d_model), F32),
                            pltpu.SemaphoreType.DMA((2,))]),
        out_shape=jax.ShapeDtypeStruct((t, d_model), F32),
        compiler_params=_cparams(("arbitrary",)),
        name="combine_ln2",
    )(dest_flat, y, x1, gates, g_m, ln_g.reshape(1, -1), ln_b.reshape(1, -1))


def _routing_plan(counts, idx, rank, n_tokens):
    n_exp = counts.shape[0]

    def take(table, index):
        hit = index[..., None] == jnp.arange(n_exp, dtype=I32)
        return jnp.sum(jnp.where(hit, table, 0), axis=-1)

    padded = (counts + ROW_BLOCK - 1) // ROW_BLOCK * ROW_BLOCK
    pad_end = jnp.cumsum(padded)
    pad_start = pad_end - padded
    dest = (take(pad_start, idx) + rank).reshape(-1).astype(I32)

    max_rows = (n_tokens * TOP_K + n_exp * (ROW_BLOCK - 1) + ROW_BLOCK - 1) // ROW_BLOCK * ROW_BLOCK
    max_items = n_exp + max_rows // ITEM_ROWS
    items_per = (padded + ITEM_ROWS - 1) // ITEM_ROWS
    items_end = jnp.cumsum(items_per)
    total = items_end[-1]
    it = jnp.arange(max_items, dtype=I32)
    last = jnp.maximum(total - 1, 0)
    it_c = jnp.minimum(it, last)
    exp_of = jnp.sum((items_end[None, :] <= it_c[:, None]).astype(I32), axis=1)
    exp_of = jnp.minimum(exp_of, n_exp - 1)
    local = it_c - take(items_end - items_per, exp_of)
    row0 = take(pad_start, exp_of) + local * ITEM_ROWS
    rows = jnp.minimum(take(padded, exp_of) - local * ITEM_ROWS, ITEM_ROWS)
    nblk = jnp.where(it < total, rows // ROW_BLOCK, 0)
    used_blocks = (pad_end[-1:] // ROW_BLOCK).astype(I32)
    fill = ((pad_start + counts).astype(I32), pad_end.astype(I32))
    return dest, fill, (exp_of.astype(I32), row0.astype(I32), nblk.astype(I32), used_blocks), max_rows


def kernel(x, c, w_ada, b_ada, w_in, b_forget, w_out, ln1_g, ln1_b, w_router, b_router,
           w_gate_up, b_gate_up, w_down, b_down, ln2_g, ln2_b):
    nb, seq, d_model = x.shape
    n_heads = d_model // HEAD_DIM
    n_dil = n_heads // 2
    n_fox = n_heads - n_dil
    dil_width, fox_width = n_dil * HEAD_DIM, n_fox * HEAD_DIM
    t = nb * seq
    layer = 0

    mod = _ada_mod(c, w_ada[layer], b_ada[layer])
    sh_a, sc_a, g_a, sh_m, sc_m, g_m = [m.reshape(nb, 1, d_model) for m in jnp.split(mod, 6, axis=-1)]

    x2 = x.reshape(t, d_model)
    qkv_dil, qkv_fox, f_logit = _in_proj(x2, sc_a, sh_a, w_in, layer, seq, dil_width, fox_width, n_fox)

    slopes = 2.0 ** (-8.0 * jnp.arange(1, n_dil + 1, dtype=F32) / n_dil)
    o_dil = _dilated_attention(qkv_dil.reshape(nb, seq, 3 * dil_width), slopes, nb, seq, n_dil)

    fox_blk = 512
    cum = _forget_cumsum(f_logit, b_forget[layer], nb, seq).reshape(
        nb, n_fox, seq // fox_blk, 1, fox_blk)
    o_fox = _fox_attention(qkv_fox.reshape(nb, seq, 3 * fox_width), cum, nb, seq, n_fox, fox_blk)

    x1, hp, idx, gates, rank, cnt = _post_attn(
        o_dil.reshape(t, dil_width), o_fox.reshape(t, fox_width), x2, w_out[layer].astype(BF16),
        g_a, sc_m, sh_m, ln1_g[layer], ln1_b[layer], w_router[layer], b_router[layer], seq)

    counts = cnt[0, :N_EXPERTS].astype(I32)
    dest, fill, items, max_rows = _routing_plan(counts, idx[:TOP_K], rank[:TOP_K], t)
    xs = _scatter_rows(dest, *fill, items[3], hp, max_rows)
    y = _moe_experts(*items, xs, w_gate_up[layer], b_gate_up[layer], w_down[layer], b_down[layer])
    out = _combine(dest, y, x1, gates, g_m, ln2_g[layer], ln2_b[layer], seq)
    return out.reshape(nb, seq, d_model)
```

```python
import functools
import math

import jax
import jax.numpy as jnp
from jax import lax
from jax.experimental import pallas as pl
from jax.experimental.pallas import tpu as pltpu

F32 = jnp.float32
BF16 = jnp.bfloat16
I32 = jnp.int32
U32 = jnp.uint32

HEAD_DIM = 128
Q_BLOCK = 128
DIL_PATTERNS = ((128, 1), (512, 4), (2048, 16))
N_EXPERTS = 32
TOP_K = 4
SWIGLU_LIMIT = 7.0
SWIGLU_ALPHA = 1.702
LN_EPS = 1e-5
DEPTH = 1
DEEPNORM_ALPHA = (2.0 * DEPTH) ** 0.25

LOG2E = math.log2(math.e)
DIL_GROUP = 8
DEINT = 4
FOX_HEADS_PER_STEP = 4

LANES = 128
VMEM_LIMIT = 56 * 1024 * 1024

ROW_BLOCK = 128
ITEM_ROWS = 1280
FF_CHUNK = 512


def _cparams(sem):
    return pltpu.CompilerParams(dimension_semantics=sem, vmem_limit_bytes=VMEM_LIMIT)


def _dot(a, b):
    return jnp.dot(a, b, preferred_element_type=F32)


def _dot_nt(a, b):
    return lax.dot_general(a, b, (((1,), (1,)), ((), ())), preferred_element_type=F32)


def _split_bf16(x):
    hi = x.astype(BF16)
    lo = (x - hi.astype(F32)).astype(BF16)
    return hi, lo


def _split_weight(w):
    return jnp.concatenate(_split_bf16(w), axis=1)


def _dot_split(x_hi, x_lo, w_cat):
    r = _dot(x_hi, w_cat) + _dot(x_lo, w_cat)
    return r[:, :LANES] + r[:, LANES:]


def _layer_norm(y, g, b):
    mu = jnp.mean(y, axis=-1, keepdims=True)
    yc = y - mu
    var = jnp.mean(yc * yc, axis=-1, keepdims=True)
    return yc * lax.rsqrt(var + LN_EPS) * g + b


def _ada_kernel(ct_ref, w_ref, b_ref, o_ref):
    ct = ct_ref[...]
    s = ct * jax.nn.sigmoid(ct)
    d_model, nb = ct.shape
    tn = w_ref.shape[1]
    sb = [jnp.broadcast_to(s[:, b:b + 1], (d_model, LANES)) for b in range(nb)]
    for j in range(tn // LANES):
        cols = slice(j * LANES, (j + 1) * LANES)
        w = w_ref[:, cols]
        for b in range(nb):
            acc = jnp.sum(w * sb[b], axis=0, keepdims=True)
            o_ref[b:b + 1, cols] = acc + b_ref[:, cols]


def _ada_mod(c, w, b):
    nb, d_model = c.shape
    n = w.shape[1]
    tn = 1536 if n % 1536 == 0 else LANES
    return pl.pallas_call(
        _ada_kernel,
        grid=(n // tn,),
        in_specs=[pl.BlockSpec((d_model, nb), lambda j: (0, 0)),
                  pl.BlockSpec((d_model, tn), lambda j: (0, j)),
                  pl.BlockSpec((1, tn), lambda j: (0, j))],
        out_specs=pl.BlockSpec((nb, tn), lambda j: (0, j)),
        out_shape=jax.ShapeDtypeStruct((nb, n), F32),
        compiler_params=_cparams(("arbitrary",)),
        name="ada_mod",
    )(c.T, w, b.reshape(1, n))


def _inproj_kernel(x_ref, sc_ref, sh_ref, w_ref, wf_ref,
                   odil_ref, ofox_ref, of_ref, h_scr, *, n_dil_tiles, q_tiles, scale):
    j = pl.program_id(1)

    @pl.when(j == 0)
    def _():
        h = x_ref[...] * (1.0 + sc_ref[0]) + sh_ref[0]
        h_hi, h_lo = _split_bf16(h)
        h_scr[...] = h_hi
        of_ref[...] = _dot_split(h_hi, h_lo, wf_ref[...])

    acc = _dot(h_scr[...], w_ref[...])
    is_q =(j < q_tiles) | ((j >= n_dil_tiles) & (j < n_dil_tiles + q_tiles))
    acc = acc * jnp.where(is_q, scale, 1.0).astype(F32)

    @pl.when(j < n_dil_tiles)
    def _():
        odil_ref[...] = acc

    @pl.when(j >= n_dil_tiles)
    def _():
        ofox_ref[...] = acc.astype(BF16)


def _in_proj(x2, sc, sh, w_in, layer, seq, dil_width, fox_width, n_fox_heads):
    t, d_model = x2.shape
    tm, tn = min(1024, seq), min(1024, dil_width)
    n_dil_tiles = 3 * dil_width // tn
    n_fox_tiles = 3 * fox_width // tn
    q_tiles = dil_width // tn
    assert dil_width == fox_width and dil_width % tn == 0 and seq % tm == 0
    main = 3 * dil_width + 3 * fox_width
    wf = _split_weight(jnp.pad(w_in[layer, :, main:], ((0, 0), (0, LANES - n_fox_heads))))
    kern = functools.partial(_inproj_kernel, n_dil_tiles=n_dil_tiles, q_tiles=q_tiles,
                             scale=LOG2E / math.sqrt(HEAD_DIM))
    bidx = lambda i, j: (i * tm // seq, 0, 0)
    return pl.pallas_call(
        kern,
        grid=(t // tm, n_dil_tiles + n_fox_tiles),
        in_specs=[pl.BlockSpec((tm, d_model), lambda i, j: (i, 0)),
                  pl.BlockSpec((1, 1, d_model), bidx),
                  pl.BlockSpec((1, 1, d_model), bidx),
                  pl.BlockSpec((None, d_model, tn), lambda i, j: (layer, 0, j)),
                  pl.BlockSpec((d_model, 2 * LANES), lambda i, j: (0, 0))],
        out_specs=[pl.BlockSpec((tm, tn), lambda i, j: (i, jnp.minimum(j, n_dil_tiles - 1))),
                   pl.BlockSpec((tm, tn), lambda i, j: (i, jnp.maximum(j - n_dil_tiles, 0))),
                   pl.BlockSpec((tm, LANES), lambda i, j: (i, 0))],
        out_shape=[jax.ShapeDtypeStruct((t, 3 * dil_width), F32),
                   jax.ShapeDtypeStruct((t, 3 * fox_width), BF16),
                   jax.ShapeDtypeStruct((t, LANES), F32)],
        scratch_shapes=[pltpu.VMEM((tm, d_model), BF16)],
        compiler_params=_cparams(("arbitrary", "arbitrary")),
        name="in_proj",
    )(x2, sc, sh, w_in.astype(BF16), wf)


def _cum_kernel(f_ref, b_ref, o_ref):
    hf = o_ref.shape[1]
    z = f_ref[...].T[:hf, :] + b_ref[...]
    c = jnp.minimum(z, 0.0) - jnp.log1p(jnp.exp(-jnp.abs(z)))
    seq = z.shape[1]
    lane = lax.broadcasted_iota(I32, z.shape, 1)
    shift = 1
    while shift < seq:
        c = c + jnp.where(lane >= shift, pltpu.roll(c, shift, axis=1), 0.0)
        shift *= 2
    o_ref[0] = c * LOG2E


def _forget_cumsum(f_logit, b_forget, nb, seq):
    hf = b_forget.shape[0]
    return pl.pallas_call(
        _cum_kernel,
        grid=(nb,),
        in_specs=[pl.BlockSpec((seq, LANES), lambda b: (b, 0)),
                  pl.BlockSpec((hf, 1), lambda b: (0, 0))],
        out_specs=pl.BlockSpec((1, hf, seq), lambda b: (b, 0, 0)),
        out_shape=jax.ShapeDtypeStruct((nb, hf, seq), F32),
        compiler_params=_cparams(("arbitrary",)),
        name="forget_cumsum",
    )(f_logit, b_forget.reshape(hf, 1))


def _dil_kernel(slope_ref, q_ref, k_ref, v_ref, o_ref, q4_ref, k4_ref, v4_ref, acc_ref, l_ref, m_ref,
                bias_ref, *, patterns, seq):
    slope = slope_ref[pl.program_id(1)]
    qb = Q_BLOCK
    row = lax.broadcasted_iota(I32, (qb, 2 * qb), 0)
    col = lax.broadcasted_iota(I32, (qb, 2 * qb), 1)
    delta = row + qb - col
    is_prev = col < qb
    neg_inf = F32(-jnp.inf)
    ones = jnp.ones((2 * qb, HEAD_DIM), BF16)

    sub = seq // DEINT
    chunk = min(256, sub)
    for r4 in range(DEINT):
        for c0 in range(0, sub, chunk):
            src = pl.ds(r4 + DEINT * c0, chunk, stride=DEINT)
            dst = pl.ds(r4 * sub + c0, chunk)
            q4_ref[dst, :] = q_ref[0, src, :]
            k4_ref[dst, :] = k_ref[0, src, :]
            v4_ref[dst, :] = v_ref[0, src, :]

    for pi, (window, dil) in enumerate(sorted(patterns, key=lambda wd: -wd[1])):
        reach = window // dil
        nblk = seq // dil // qb
        n_blocks = dil * nblk
        assert reach <= qb and nblk * qb * dil == seq and n_blocks % DIL_GROUP == 0
        sd = slope * (float(dil) * LOG2E)
        bias_ref[...] = jnp.where((delta >= 0) & (delta <= reach), -sd * delta.astype(F32), neg_inf)

        def block(idx, dil=dil, nblk=nblk, pi=pi):
            r = idx // nblk
            n = idx - r * nblk
            start = r + n * (qb * dil)
            cur = pl.ds(start, qb, stride=dil) if dil > 1 else pl.ds(pl.multiple_of(start, qb), qb)
            if dil % DEINT == 0:
                step = dil // DEINT
                first = (r % DEINT) * sub + r // DEINT
                lstart = first + n * (qb * step)
                lprev = jnp.maximum(lstart - qb * step, first)
                if step > 1:
                    src_cur, src_prev = pl.ds(lstart, qb, stride=step), pl.ds(lprev, qb, stride=step)
                else:
                    src_cur = pl.ds(pl.multiple_of(lstart, qb), qb)
                    src_prev = pl.ds(pl.multiple_of(lprev, qb), qb)
                q = q4_ref[src_cur, :].astype(BF16)
                kk = jnp.concatenate([k4_ref[src_prev, :], k4_ref[src_cur, :]], axis=0).astype(BF16)
                vv = jnp.concatenate([v4_ref[src_prev, :], v4_ref[src_cur, :]], axis=0).astype(BF16)
            else:
                pstart = jnp.maximum(start - qb * dil, r)
                prev = (pl.ds(pstart, qb, stride=dil) if dil > 1
                        else pl.ds(pl.multiple_of(pstart, qb), qb))
                q = q_ref[0, cur, :].astype(BF16)
                kk = jnp.concatenate([k_ref[0, prev, :], k_ref[0, cur, :]], axis=0).astype(BF16)
                vv = jnp.concatenate([v_ref[0, prev, :], v_ref[0, cur, :]], axis=0).astype(BF16)
            s = _dot_nt(q, kk) + bias_ref[...]
            s = jnp.where(is_prev & (n == 0), neg_inf, s)
            m_blk = jnp.max(s, axis=1, keepdims=True)
            p = jnp.exp2(s - m_blk)
            pv = _dot(p.astype(BF16), jnp.concatenate([vv, ones], axis=1))
            o_blk, l_blk = pv[:, :HEAD_DIM], pv[:, HEAD_DIM:]
            if pi == 0:
                m_ref[cur, :] = jnp.broadcast_to(m_blk, (qb, LANES))
                acc_ref[cur, :] = o_blk
                l_ref[cur, :] = l_blk
            else:
                m_old = m_ref[cur, :]
                m_new = jnp.maximum(m_old, m_blk)
                a_old = jnp.exp2(m_old - m_new)
                a_blk = jnp.exp2(m_blk - m_new)
                m_ref[cur, :] = m_new
                acc_ref[cur, :] = acc_ref[cur, :] * a_old + o_blk * a_blk
                l_ref[cur, :] = l_ref[cur, :] * a_old + l_blk * a_blk

        def body(g, carry, block=block):
            for u in range(DIL_GROUP):
                block(g * DIL_GROUP + u)
            return carry

        lax.fori_loop(0, n_blocks // DIL_GROUP, body, 0)

    chunk = 512
    for c0 in range(0, seq, chunk):
        rows = slice(c0, c0 + chunk)
        o_ref[0, rows, :] = (acc_ref[rows, :] / l_ref[rows, :]).astype(o_ref.dtype)


def _dilated_attention(qkv, slopes, nb, seq, n_heads):
    dh = HEAD_DIM
    kern = functools.partial(_dil_kernel, patterns=DIL_PATTERNS, seq=seq)
    blk = lambda off: pl.BlockSpec((1, seq, dh), lambda b, h, s: (b, 0, off + h))
    return pl.pallas_call(
        kern,
        grid_spec=pltpu.PrefetchScalarGridSpec(
            num_scalar_prefetch=1,
            grid=(nb, n_heads),
            in_specs=[blk(0), blk(n_heads), blk(2 * n_heads)],
            out_specs=pl.BlockSpec((1, seq, dh), lambda b, h, s: (b, 0, h)),
            scratch_shapes=[pltpu.VMEM((seq, dh), F32),
                            pltpu.VMEM((seq, dh), F32),
                            pltpu.VMEM((seq, dh), F32),
                            pltpu.VMEM((seq, dh), F32),
                            pltpu.VMEM((seq, LANES), F32),
                            pltpu.VMEM((seq, LANES), F32),
                            pltpu.VMEM((Q_BLOCK, 2 * Q_BLOCK), F32)]),
        out_shape=jax.ShapeDtypeStruct((nb, seq, n_heads * dh), BF16),
        compiler_params=_cparams(("arbitrary", "arbitrary")),
        name="dilated_attn",
    )(slopes, qkv, qkv, qkv)


def _fox_kernel(q_ref, k_ref, v_ref, c_ref, o_ref, m_scr, acc_scr, *, blk, hb):
    i = pl.program_id(2)
    dh = HEAD_DIM
    row = lax.broadcasted_iota(I32, (blk, blk), 0)
    col = lax.broadcasted_iota(I32, (blk, blk), 1)
    m_scr[...] = jnp.full(m_scr.shape, -jnp.inf, F32)
    acc_scr[...] = jnp.zeros(acc_scr.shape, F32)
    ones = jnp.ones((blk, dh), BF16)
    n_slab = blk // LANES

    def step(j, masked):
        rows = pl.ds(pl.multiple_of(j * blk, blk), blk)
        for h in range(hb):
            cols = slice(h * dh, (h + 1) * dh)
            s = _dot_nt(q_ref[0, :, cols], k_ref[0, rows, cols]) - c_ref[0, h, j]
            if masked:
                s = jnp.where(col <= row, s, -jnp.inf)
            m_old = m_scr[h]
            m_new = jnp.maximum(m_old, jnp.max(s, axis=1, keepdims=True))
            alpha = jnp.exp2(m_old - m_new)
            p = jnp.concatenate(
                [jnp.exp2(s[:, t * LANES:(t + 1) * LANES] - m_new) for t in range(n_slab)], axis=1)
            pv = _dot(p.astype(BF16), jnp.concatenate([v_ref[0, rows, cols], ones], axis=1))
            acc_scr[h] = jnp.concatenate([alpha, alpha], axis=1) * acc_scr[h] + pv
            m_scr[h] = m_new

    def body(j, carry):
        step(j, False)
        return carry

    lax.fori_loop(0, i, body, 0)
    step(i, True)
    for h in range(hb):
        acc = acc_scr[h]
        o_ref[0, :, h * dh:(h + 1) * dh] = (acc[:, :dh] / acc[:, dh:]).astype(o_ref.dtype)


def _fox_attention(qkv, cum, nb, seq, n_heads, blk):
    dh = HEAD_DIM
    hb = min(FOX_HEADS_PER_STEP, n_heads)
    assert n_heads % hb == 0
    ng = n_heads // hb
    w = hb * dh
    kern = functools.partial(_fox_kernel, blk=blk, hb=hb)
    return pl.pallas_call(
        kern,
        grid=(nb, ng, seq // blk),
        in_specs=[pl.BlockSpec((1, blk, w), lambda b, g, i: (b, i, g)),
                  pl.BlockSpec((1, seq, w), lambda b, g, i: (b, 0, ng + g)),
                  pl.BlockSpec((1, seq, w), lambda b, g, i: (b, 0, 2 * ng + g)),
                  pl.BlockSpec((1, hb, seq // blk, 1, blk), lambda b, g, i: (b, g, 0, 0, 0))],
        out_specs=pl.BlockSpec((1, blk, w), lambda b, g, i: (b, i, g)),
        out_shape=jax.ShapeDtypeStruct((nb, seq, n_heads * dh), BF16),
        scratch_shapes=[pltpu.VMEM((hb, blk, LANES), F32),
                        pltpu.VMEM((hb, blk, 2 * dh), F32)],
        compiler_params=_cparams(("arbitrary", "arbitrary", "arbitrary")),
        name="fox_attn",
    )(qkv, qkv, qkv, cum)


def _post_attn_kernel(od_ref, of_ref, x_ref, w_ref, ga_ref, scm_ref, shm_ref, g_ref, b_ref,
                      wr_ref, br_ref,
                      x1_ref, hp_ref, idx_ref, gate_ref, rank_ref, cnt_ref, carry_scr,
                      *, n_experts, top_k, dil_width, sub):
    i = pl.program_id(0)

    @pl.when(i == 0)
    def _():
        carry_scr[...] = jnp.zeros_like(carry_scr)

    for r0 in range(0, x_ref.shape[0], sub):
        _post_attn_rows(slice(r0, r0 + sub), od_ref, of_ref, x_ref, w_ref, ga_ref, scm_ref, shm_ref,
                        g_ref, b_ref, wr_ref, br_ref, x1_ref, hp_ref, idx_ref, gate_ref, rank_ref,
                        carry_scr, n_experts=n_experts, top_k=top_k, dil_width=dil_width)
    cnt_ref[...] = jnp.broadcast_to(carry_scr[...], cnt_ref.shape)


def _post_attn_rows(rows, od_ref, of_ref, x_ref, w_ref, ga_ref, scm_ref, shm_ref, g_ref, b_ref,
                    wr_ref, br_ref, x1_ref, hp_ref, idx_ref, gate_ref, rank_ref, carry_scr,
                    *, n_experts, top_k, dil_width):
    attn = (_dot(od_ref[rows, :], w_ref[:dil_width, :]) + _dot(of_ref[rows, :], w_ref[dil_width:, :]))
    x1 = _layer_norm(DEEPNORM_ALPHA * x_ref[rows, :] + ga_ref[0] * attn, g_ref[...], b_ref[...])
    x1_ref[rows, :] = x1
    h = x1 * (1.0 + scm_ref[0]) + shm_ref[0]
    h_hi, h_lo = _split_bf16(h)
    hp_ref[rows, :] = h

    logits = _dot_split(h_hi, h_lo, wr_ref[...]) + br_ref[...]
    tm = logits.shape[0]
    lane = lax.broadcasted_iota(I32, (tm, LANES), 1)
    work = jnp.where(lane < n_experts, logits, -jnp.inf)
    vals, idxs = [], []
    for _ in range(top_k):
        mx = jnp.max(work, axis=1, keepdims=True)
        ix = jnp.min(jnp.where(work == mx, lane, LANES), axis=1, keepdims=True)
        vals.append(mx)
        idxs.append(ix)
        work = jnp.where(lane == ix, -jnp.inf, work)
    exps = [jnp.exp(v - vals[0]) for v in vals]
    denom = exps[0]
    for e in exps[1:]:
        denom = denom + e

    onehot = jnp.zeros((tm, LANES), F32)
    for ix in idxs:
        onehot = onehot + (lane == ix).astype(F32)
    r_i = lax.broadcasted_iota(I32, (tm, tm), 0)
    c_i = lax.broadcasted_iota(I32, (tm, tm), 1)
    tri = (c_i < r_i).astype(BF16)
    before = _dot(tri, onehot.astype(BF16)) + carry_scr[...]

    idx_out = jnp.zeros((tm, LANES), I32)
    gate_out = jnp.zeros((tm, LANES), F32)
    rank_out = jnp.zeros((tm, LANES), I32)
    for k in range(top_k):
        rank_k = jnp.sum(jnp.where(lane == idxs[k], before, 0.0), axis=1, keepdims=True)
        idx_out = jnp.where(lane == k, idxs[k], idx_out)
        gate_out = jnp.where(lane == k, exps[k] / denom, gate_out)
        rank_out = jnp.where(lane == k, rank_k.astype(I32), rank_out)
    gate_ref[rows, :] = gate_out
    idx_ref[:, rows] = idx_out.T[:idx_ref.shape[0], :]
    rank_ref[:, rows] = rank_out.T[:rank_ref.shape[0], :]
    carry_scr[...] = carry_scr[...] + jnp.sum(onehot, axis=0, keepdims=True)


def _post_attn(o_dil, o_fox, x2, w_out_bf, g_a, sc_m, sh_m, ln_g, ln_b, w_router, b_router, seq):
    t, d_model = x2.shape
    dil_width = o_dil.shape[1]
    n_experts = w_router.shape[1]
    tm, sub = 256, 256
    wr = _split_weight(jnp.pad(w_router, ((0, 0), (0, LANES - n_experts))))
    br = jnp.pad(b_router, (0, LANES - n_experts)).reshape(1, LANES)
    kern = functools.partial(_post_attn_kernel, n_experts=n_experts, top_k=TOP_K, dil_width=dil_width,
                             sub=sub)
    row = lambda w: pl.BlockSpec((tm, w), lambda i: (i, 0))
    col = pl.BlockSpec((8, tm), lambda i: (0, i))
    const = lambda r, w: pl.BlockSpec((r, w), lambda i: (0, 0))
    bvec = pl.BlockSpec((1, 1, d_model), lambda i: (i * tm // seq, 0, 0))
    return pl.pallas_call(
        kern,
        grid=(t // tm,),
        in_specs=[row(dil_width), row(o_fox.shape[1]), row(d_model), const(d_model, d_model),
                  bvec, bvec, bvec, const(1, d_model), const(1, d_model),
                  const(d_model, 2 * LANES), const(1, LANES)],
        out_specs=[row(d_model), row(d_model), col, row(LANES), col,
                   pl.BlockSpec((8, LANES), lambda i: (0, 0))],
        out_shape=[jax.ShapeDtypeStruct((t, d_model), F32),
                   jax.ShapeDtypeStruct((t, d_model), F32),
                   jax.ShapeDtypeStruct((8, t), I32),
                   jax.ShapeDtypeStruct((t, LANES), F32),
                   jax.ShapeDtypeStruct((8, t), I32),
                   jax.ShapeDtypeStruct((8, LANES), F32)],
        scratch_shapes=[pltpu.VMEM((1, LANES), F32)],
        compiler_params=_cparams(("arbitrary",)),
        name="post_attn",
    )(o_dil, o_fox, x2, w_out_bf, g_a, sc_m, sh_m, ln_g.reshape(1, -1), ln_b.reshape(1, -1),
      wr, br)


def _scatter_kernel(dest_ref, fill_lo_ref, fill_hi_ref, used_ref, hp_ref, xs_ref, zbuf, sem, zsem,
                    *, top_k):
    i = pl.program_id(0)
    tm = hp_ref.shape[0]
    rb = ROW_BLOCK

    @pl.when(i == 0)
    def _():
        zbuf[...] = jnp.zeros(zbuf.shape, zbuf.dtype)

        def tails(wait):
            def row(r, c):
                copy = pltpu.make_async_copy(zbuf.at[pl.ds(0, 1)], xs_ref.at[pl.ds(r, 1)], zsem.at[0])
                copy.wait() if wait else copy.start()
                return c

            def expert(e, c):
                return lax.fori_loop(fill_lo_ref[e], fill_hi_ref[e], row, c)

            lax.fori_loop(0, fill_lo_ref.shape[0], expert, 0)

        def rest(wait):
            def blk(b, c):
                copy = pltpu.make_async_copy(zbuf, xs_ref.at[pl.ds(pl.multiple_of(b * rb, rb), rb)],
                                             zsem.at[1])
                copy.wait() if wait else copy.start()
                return c

            lax.fori_loop(used_ref[0], xs_ref.shape[0] // rb, blk, 0)

        tails(False)
        rest(False)
        tails(True)
        rest(True)

    n_tokens = dest_ref.shape[0] // top_k
    base = i * tm
    for t in range(tm):
        for k in range(top_k):
            d = dest_ref[base + (k * n_tokens + t)]
            pltpu.make_async_copy(hp_ref.at[pl.ds(t, 1)], xs_ref.at[pl.ds(d, 1)], sem).start(
                priority=(t * top_k + k) % 2)

    def drain(t, carry):
        for k in range(top_k):
            pltpu.make_async_copy(hp_ref.at[pl.ds(0, 1)], xs_ref.at[pl.ds(0, 1)], sem).wait()
        return carry

    lax.fori_loop(0, tm, drain, 0, unroll=8)


def _scatter_rows(dest_flat, fill_lo, fill_hi, used_blocks, hp, n_rows):
    t, w = hp.shape
    tm = 256
    kern = functools.partial(_scatter_kernel, top_k=TOP_K)
    return pl.pallas_call(
        kern,
        grid_spec=pltpu.PrefetchScalarGridSpec(
            num_scalar_prefetch=4,
            grid=(t // tm,),
            in_specs=[pl.BlockSpec((tm, w), lambda i, *_: (i, 0))],
            out_specs=pl.BlockSpec(memory_space=pl.ANY),
            scratch_shapes=[pltpu.VMEM((ROW_BLOCK, w), F32),
                            pltpu.SemaphoreType.DMA(()),
                            pltpu.SemaphoreType.DMA((2,))]),
        out_shape=jax.ShapeDtypeStruct((n_rows, w), F32),
        compiler_params=_cparams(("arbitrary",)),
        name="scatter_rows",
    )(dest_flat, fill_lo, fill_hi, used_blocks, hp)


def _moe_kernel(iexp_ref, irow_ref, inu_ref, used_ref, xs_ref, wg_ref, wu_ref, wd_ref, bg_ref, bu_ref,
                bd_ref, y_ref, xbuf, yacc, sem_in, sem_out, *, n_chunks):
    del iexp_ref
    it = pl.program_id(0)
    f = pl.program_id(1)
    n_items = pl.num_programs(0)
    nu = inu_ref[it]
    rb = ROW_BLOCK
    rows_of = lambda k, n=1: pl.ds(pl.multiple_of(k * rb, rb), n * rb)

    def x_copy(item, k):
        src = pl.ds(pl.multiple_of(irow_ref[item] + k * rb, rb), rb)
        return pltpu.make_async_copy(xs_ref.at[src], xbuf.at[rows_of(k)], sem_in)

    def y_copy(k):
        dst = pl.ds(pl.multiple_of(irow_ref[it] + k * rb, rb), rb)
        return pltpu.make_async_copy(yacc.at[rows_of(k)], y_ref.at[dst], sem_out)

    def for_units(n, fn):
        def body(k, c):
            fn(k)
            return c
        lax.fori_loop(0, n, body, 0)

    def init_unit(k):
        yacc[rows_of(k), :] = jnp.broadcast_to(bd_ref[0], (rb, yacc.shape[1]))

    @pl.when(nu > 0)
    def _():
        first = f == 0
        last = f == n_chunks - 1

        @pl.when(first & (it == 0))
        def _():
            for_units(nu, lambda k: x_copy(0, k).start())

        @pl.when(first)
        def _():
            for_units(nu, init_unit)

        wg = wg_ref[0].astype(BF16)
        wu = wu_ref[0].astype(BF16)
        wd = wd_ref[0].astype(BF16)
        bg = bg_ref[0]
        bu = bu_ref[0]

        @pl.when(first)
        def _():
            for_units(nu, lambda k: x_copy(it, k).wait())

        def block(k, n):
            xb = xbuf[rows_of(k, n), :].astype(BF16)
            g = _dot(xb, wg) + bg
            u = _dot(xb, wu) + bu
            g = jnp.minimum(g, SWIGLU_LIMIT)
            u = jnp.clip(u, -SWIGLU_LIMIT, SWIGLU_LIMIT)
            act = (u + 1.0) * (g * jax.nn.sigmoid(SWIGLU_ALPHA * g))
            yacc[rows_of(k, n), :] += _dot(act.astype(BF16), wd)

        def quad(t, c):
            block(4 * t, 2)
            block(4 * t + 2, 2)
            return c

        n_quads = lax.shift_right_logical(nu, 2)
        lax.fori_loop(0, n_quads, quad, 0)
        done = n_quads * 4

        @pl.when((nu & 2) != 0)
        def _():
            block(done, 2)

        @pl.when((nu & 1) != 0)
        def _():
            block(done + (nu & 2), 1)

        @pl.when(last)
        def _():
            for_units(nu, lambda k: y_copy(k).start())
            nxt = jnp.minimum(it + 1, n_items - 1)
            n_next = jnp.where(it + 1 < n_items, inu_ref[nxt], 0)
            for_units(n_next, lambda k: x_copy(nxt, k).start())
            for_units(nu, lambda k: y_copy(k).wait())

    @pl.when((it == n_items - 1) & (f == n_chunks - 1))
    def _():
        first = used_ref[0]
        last = y_ref.shape[0] // rb
        dst_of = lambda b: pl.ds(pl.multiple_of(b * rb, rb), rb)
        yacc[pl.ds(0, rb), :] = jnp.zeros((rb, yacc.shape[1]), yacc.dtype)

        def start(b, c):
            pltpu.make_async_copy(yacc.at[pl.ds(0, rb)], y_ref.at[dst_of(b)], sem_out).start()
            return c

        def wait(b, c):
            pltpu.make_async_copy(yacc.at[pl.ds(0, rb)], y_ref.at[dst_of(b)], sem_out).wait()
            return c

        lax.fori_loop(first, last, start, 0)
        lax.fori_loop(first, last, wait, 0)


def _moe_experts(item_exp, item_row, item_nblk, used_blocks, xs, w_gate_up, b_gate_up, w_down, b_down):
    n_rows = xs.shape[0]
    n_exp, d_model, two_f = w_gate_up.shape
    d_ff = two_f // 2
    fc = FF_CHUNK
    n_chunks = d_ff // fc
    n_items = item_exp.shape[0]
    kern = functools.partial(_moe_kernel, n_chunks=n_chunks)

    def fsel(it, f, nbk):
        return jnp.where(nbk[it] > 0, f, n_chunks - 1)

    return pl.pallas_call(
        kern,
        grid_spec=pltpu.PrefetchScalarGridSpec(
            num_scalar_prefetch=4,
            grid=(n_items, n_chunks),
            in_specs=[
                pl.BlockSpec(memory_space=pl.ANY),
                pl.BlockSpec((1, d_model, fc), lambda it, f, ex, ro, nbk, us: (ex[it], 0, fsel(it, f, nbk))),
                pl.BlockSpec((1, d_model, fc),
                             lambda it, f, ex, ro, nbk, us: (ex[it], 0, n_chunks + fsel(it, f, nbk))),
                pl.BlockSpec((1, fc, d_model), lambda it, f, ex, ro, nbk, us: (ex[it], fsel(it, f, nbk), 0)),
                pl.BlockSpec((1, 1, fc), lambda it, f, ex, ro, nbk, us: (ex[it], 0, fsel(it, f, nbk))),
                pl.BlockSpec((1, 1, fc),
                             lambda it, f, ex, ro, nbk, us: (ex[it], 0, n_chunks + fsel(it, f, nbk))),
                pl.BlockSpec((1, 1, d_model), lambda it, f, ex, ro, nbk, us: (ex[it], 0, 0)),
            ],
            out_specs=pl.BlockSpec(memory_space=pl.ANY),
            scratch_shapes=[pltpu.VMEM((ITEM_ROWS, d_model), F32),
                            pltpu.VMEM((ITEM_ROWS, d_model), F32),
                            pltpu.SemaphoreType.DMA(()),
                            pltpu.SemaphoreType.DMA(())]),
        out_shape=jax.ShapeDtypeStruct((n_rows, d_model), F32),
        compiler_params=_cparams(("arbitrary", "arbitrary")),
        name="moe_experts",
    )(item_exp, item_row, item_nblk, used_blocks, xs, w_gate_up, w_gate_up, w_down,
      b_gate_up.reshape(n_exp, 1, two_f), b_gate_up.reshape(n_exp, 1, two_f),
      b_down.reshape(n_exp, 1, d_model))


def _combine_kernel(dest_ref, y_ref, x1_ref, gate_ref, gm_ref, g_ref, b_ref, o_ref, ybuf, sem, *, top_k):
    i = pl.program_id(0)
    n_steps = pl.num_programs(0)
    tm = x1_ref.shape[0]
    slot = i % 2

    def gather(step, slot_):
        n_tokens = dest_ref.shape[0] // top_k
        base = step * tm
        for t in range(tm):
            for k in range(top_k):
                d = dest_ref[base + (k * n_tokens + t)]
                pltpu.make_async_copy(y_ref.at[pl.ds(d, 1)], ybuf.at[slot_, pl.ds(k * tm + t, 1)],
                                      sem.at[slot_]).start(priority=(t * top_k + k) % 2)

    @pl.when(i == 0)
    def _():
        gather(0, 0)

    @pl.when(i + 1 < n_steps)
    def _():
        gather(i + 1, 1 - slot)

    pltpu.make_async_copy(y_ref.at[pl.ds(0, top_k * tm)], ybuf.at[slot], sem.at[slot]).wait()
    gates = gate_ref[...]
    ffn = gates[:, 0:1] * ybuf[slot, pl.ds(0, tm), :]
    for k in range(1, top_k):
        ffn = ffn + gates[:, k:k + 1] * ybuf[slot, pl.ds(k * tm, tm), :]
    o_ref[...] = _layer_norm(DEEPNORM_ALPHA * x1_ref[...] + gm_ref[0] * ffn, g_ref[...], b_ref[...])


def _combine(dest_flat, y, x1, gates, g_m, ln_g, ln_b, seq):
    t, d_model = x1.shape
    tm = 128
    kern = functools.partial(_combine_kernel, top_k=TOP_K)
    return pl.pallas_call(
        kern,
        grid_spec=pltpu.PrefetchScalarGridSpec(
            num_scalar_prefetch=1,
            grid=(t // tm,),
            in_specs=[pl.BlockSpec(memory_space=pl.ANY),
                      pl.BlockSpec((tm, d_model), lambda i, d: (i, 0)),
                      pl.BlockSpec((tm, LANES), lambda i, d: (i, 0)),
                      pl.BlockSpec((1, 1, d_model), lambda i, d: (i * tm // seq, 0, 0)),
                      pl.BlockSpec((1, d_model), lambda i, d: (0, 0)),
                      pl.BlockSpec((1, d_model), lambda i, d: (0, 0))],
            out_specs=pl.BlockSpec((tm, d_model), lambda i, d: (i, 0)),
            scratch_shapes=[pltpu.VMEM((2, TOP_K * tm, d_model), F32),
                            pltpu.SemaphoreType.DMA((2,))]),
        out_shape=jax.ShapeDtypeStruct((t, d_model), F32),
        compiler_params=_cparams(("arbitrary",)),
        name="combine_ln2",
    )(dest_flat, y, x1, gates, g_m, ln_g.reshape(1, -1), ln_b.reshape(1, -1))


def _routing_plan(counts, idx, rank, n_tokens):
    n_exp = counts.shape[0]

    def take(table, index):
        hit = index[..., None] == jnp.arange(n_exp, dtype=I32)
        return jnp.sum(jnp.where(hit, table, 0), axis=-1)

    padded = (counts + ROW_BLOCK - 1) // ROW_BLOCK * ROW_BLOCK
    pad_end = jnp.cumsum(padded)
    pad_start = pad_end - padded
    dest = (take(pad_start, idx) + rank).reshape(-1).astype(I32)

    max_rows = (n_tokens * TOP_K + n_exp * (ROW_BLOCK - 1) + ROW_BLOCK - 1) // ROW_BLOCK * ROW_BLOCK
    max_items = n_exp + max_rows // ITEM_ROWS
    items_per = (padded + ITEM_ROWS - 1) // ITEM_ROWS
    items_end = jnp.cumsum(items_per)
    total = items_end[-1]
    it = jnp.arange(max_items, dtype=I32)
    last = jnp.maximum(total - 1, 0)
    it_c = jnp.minimum(it, last)
    exp_of = jnp.sum((items_end[None, :] <= it_c[:, None]).astype(I32), axis=1)
    exp_of = jnp.minimum(exp_of, n_exp - 1)
    local = it_c - take(items_end - items_per, exp_of)
    row0 = take(pad_start, exp_of) + local * ITEM_ROWS
    rows = jnp.minimum(take(padded, exp_of) - local * ITEM_ROWS, ITEM_ROWS)
    nblk = jnp.where(it < total, rows // ROW_BLOCK, 0)
    used_blocks = (pad_end[-1:] // ROW_BLOCK).astype(I32)
    fill = ((pad_start + counts).astype(I32), pad_end.astype(I32))
    return dest, fill, (exp_of.astype(I32), row0.astype(I32), nblk.astype(I32), used_blocks), max_rows


def kernel(x, c, w_ada, b_ada, w_in, b_forget, w_out, ln1_g, ln1_b, w_router, b_router,
           w_gate_up, b_gate_up, w_down, b_down, ln2_g, ln2_b):
    nb, seq, d_model = x.shape
    n_heads = d_model // HEAD_DIM
    n_dil = n_heads // 2
    n_fox = n_heads - n_dil
    dil_width, fox_width = n_dil * HEAD_DIM, n_fox * HEAD_DIM
    t = nb * seq
    layer = 0

    mod = _ada_mod(c, w_ada[layer], b_ada[layer])
    sh_a, sc_a, g_a, sh_m, sc_m, g_m = [m.reshape(nb, 1, d_model) for m in jnp.split(mod, 6, axis=-1)]

    x2 = x.reshape(t, d_model)
    qkv_dil, qkv_fox, f_logit = _in_proj(x2, sc_a, sh_a, w_in, layer, seq, dil_width, fox_width, n_fox)

    slopes = 2.0 ** (-8.0 * jnp.arange(1, n_dil + 1, dtype=F32) / n_dil)
    o_dil = _dilated_attention(qkv_dil.reshape(nb, seq, 3 * dil_width), slopes, nb, seq, n_dil)

    fox_blk = 512
    cum = _forget_cumsum(f_logit, b_forget[layer], nb, seq).reshape(
        nb, n_fox, seq // fox_blk, 1, fox_blk)
    o_fox = _fox_attention(qkv_fox.reshape(nb, seq, 3 * fox_width), cum, nb, seq, n_fox, fox_blk)

    x1, hp, idx, gates, rank, cnt = _post_attn(
        o_dil.reshape(t, dil_width), o_fox.reshape(t, fox_width), x2, w_out[layer].astype(BF16),
        g_a, sc_m, sh_m, ln1_g[layer], ln1_b[layer], w_router[layer], b_router[layer], seq)

    counts = cnt[0, :N_EXPERTS].astype(I32)
    dest, fill, items, max_rows = _routing_plan(counts, idx[:TOP_K], rank[:TOP_K], t)
    xs = _scatter_rows(dest, *fill, items[3], hp, max_rows)
    y = _moe_experts(*items, xs, w_gate_up[layer], b_gate_up[layer], w_down[layer], b_down[layer])
    out = _combine(dest, y, x1, gates, g_m, ln2_g[layer], ln2_b[layer], seq)
    return out.reshape(nb, seq, d_model)
```

```python
import functools
import math

import jax
import jax.numpy as jnp
from jax import lax
from jax.experimental import pallas as pl
from jax.experimental.pallas import tpu as pltpu

F32 = jnp.float32
BF16 = jnp.bfloat16
I32 = jnp.int32
U32 = jnp.uint32

HEAD_DIM = 128
Q_BLOCK = 128
DIL_PATTERNS = ((128, 1), (512, 4), (2048, 16))
N_EXPERTS = 32
TOP_K = 4
SWIGLU_LIMIT = 7.0
SWIGLU_ALPHA = 1.702
LN_EPS = 1e-5
DEPTH = 1
DEEPNORM_ALPHA = (2.0 * DEPTH) ** 0.25

LOG2E = math.log2(math.e)
DIL_GROUP = 8
DEINT = 4
FOX_HEADS_PER_STEP = 4

LANES = 128
VMEM_LIMIT = 56 * 1024 * 1024

ROW_BLOCK = 128
ITEM_ROWS = 1280
FF_CHUNK = 512


def _cparams(sem):
    return pltpu.CompilerParams(dimension_semantics=sem, vmem_limit_bytes=VMEM_LIMIT)


def _dot(a, b):
    return jnp.dot(a, b, preferred_element_type=F32)


def _dot_nt(a, b):
    return lax.dot_general(a, b, (((1,), (1,)), ((), ())), preferred_element_type=F32)


def _split_bf16(x):
    hi = x.astype(BF16)
    lo = (x - hi.astype(F32)).astype(BF16)
    return hi, lo


def _split_weight(w):
    return jnp.concatenate(_split_bf16(w), axis=1)


def _dot_split(x_hi, x_lo, w_cat):
    r = _dot(x_hi, w_cat) + _dot(x_lo, w_cat)
    return r[:, :LANES] + r[:, LANES:]


def _layer_norm(y, g, b):
    mu = jnp.mean(y, axis=-1, keepdims=True)
    yc = y - mu
    var = jnp.mean(yc * yc, axis=-1, keepdims=True)
    return yc * lax.rsqrt(var + LN_EPS) * g + b


def _ada_kernel(ct_ref, w_ref, b_ref, o_ref):
    ct = ct_ref[...]
    s = ct * jax.nn.sigmoid(ct)
    d_model, nb = ct.shape
    tn = w_ref.shape[1]
    sb = [jnp.broadcast_to(s[:, b:b + 1], (d_model, LANES)) for b in range(nb)]
    for j in range(tn // LANES):
        cols = slice(j * LANES, (j + 1) * LANES)
        w = w_ref[:, cols]
        for b in range(nb):
            acc = jnp.sum(w * sb[b], axis=0, keepdims=True)
            o_ref[b:b + 1, cols] = acc + b_ref[:, cols]


def _ada_mod(c, w, b):
    nb, d_model = c.shape
    n = w.shape[1]
    tn = 1536 if n % 1536 == 0 else LANES
    return pl.pallas_call(
        _ada_kernel,
        grid=(n // tn,),
        in_specs=[pl.BlockSpec((d_model, nb), lambda j: (0, 0)),
                  pl.BlockSpec((d_model, tn), lambda j: (0, j)),
                  pl.BlockSpec((1, tn), lambda j: (0, j))],
        out_specs=pl.BlockSpec((nb, tn), lambda j: (0, j)),
        out_shape=jax.ShapeDtypeStruct((nb, n), F32),
        compiler_params=_cparams(("arbitrary",)),
        name="ada_mod",
    )(c.T, w, b.reshape(1, n))


def _inproj_kernel(x_ref, sc_ref, sh_ref, w_ref, wf_ref,
                   odil_ref, ofox_ref, of_ref, h_scr, *, n_dil_tiles, q_tiles, scale):
    j = pl.program_id(1)

    @pl.when(j == 0)
    def _():
        h = x_ref[...] * (1.0 + sc_ref[0]) + sh_ref[0]
        h_hi, h_lo = _split_bf16(h)
        h_scr[...] = h_hi
        of_ref[...] = _dot_split(h_hi, h_lo, wf_ref[...])

    acc = _dot(h_scr[...], w_ref[...])
    is_q =(j < q_tiles) | ((j >= n_dil_tiles) & (j < n_dil_tiles + q_tiles))
    acc = acc * jnp.where(is_q, scale, 1.0).astype(F32)

    @pl.when(j < n_dil_tiles)
    def _():
        odil_ref[...] = acc

    @pl.when(j >= n_dil_tiles)
    def _():
        ofox_ref[...] = acc.astype(BF16)


def _in_proj(x2, sc, sh, w_in, layer, seq, dil_width, fox_width, n_fox_heads):
    t, d_model = x2.shape
    tm, tn = min(1024, seq), min(1024, dil_width)
    n_dil_tiles = 3 * dil_width // tn
    n_fox_tiles = 3 * fox_width // tn
    q_tiles = dil_width // tn
    assert dil_width == fox_width and dil_width % tn == 0 and seq % tm == 0
    main = 3 * dil_width + 3 * fox_width
    wf = _split_weight(jnp.pad(w_in[layer, :, main:], ((0, 0), (0, LANES - n_fox_heads))))
    kern = functools.partial(_inproj_kernel, n_dil_tiles=n_dil_tiles, q_tiles=q_tiles,
                             scale=LOG2E / math.sqrt(HEAD_DIM))
    bidx = lambda i, j: (i * tm // seq, 0, 0)
    return pl.pallas_call(
        kern,
        grid=(t // tm, n_dil_tiles + n_fox_tiles),
        in_specs=[pl.BlockSpec((tm, d_model), lambda i, j: (i, 0)),
                  pl.BlockSpec((1, 1, d_model), bidx),
                  pl.BlockSpec((1, 1, d_model), bidx),
                  pl.BlockSpec((None, d_model, tn), lambda i, j: (layer, 0, j)),
                  pl.BlockSpec((d_model, 2 * LANES), lambda i, j: (0, 0))],
        out_specs=[pl.BlockSpec((tm, tn), lambda i, j: (i, jnp.minimum(j, n_dil_tiles - 1))),
                   pl.BlockSpec((tm, tn), lambda i, j: (i, jnp.maximum(j - n_dil_tiles, 0))),
                   pl.BlockSpec((tm, LANES), lambda i, j: (i, 0))],
        out_shape=[jax.ShapeDtypeStruct((t, 3 * dil_width), F32),
                   jax.ShapeDtypeStruct((t, 3 * fox_width), BF16),
                   jax.ShapeDtypeStruct((t, LANES), F32)],
        scratch_shapes=[pltpu.VMEM((tm, d_model), BF16)],
        compiler_params=_cparams(("arbitrary", "arbitrary")),
        name="in_proj",
    )(x2, sc, sh, w_in.astype(BF16), wf)


def _cum_kernel(f_ref, b_ref, o_ref):
    hf = o_ref.shape[1]
    z = f_ref[...].T[:hf, :] + b_ref[...]
    c = jnp.minimum(z, 0.0) - jnp.log1p(jnp.exp(-jnp.abs(z)))
    seq = z.shape[1]
    lane = lax.broadcasted_iota(I32, z.shape, 1)
    shift = 1
    while shift < seq:
        c = c + jnp.where(lane >= shift, pltpu.roll(c, shift, axis=1), 0.0)
        shift *= 2
    o_ref[0] = c * LOG2E


def _forget_cumsum(f_logit, b_forget, nb, seq):
    hf = b_forget.shape[0]
    return pl.pallas_call(
        _cum_kernel,
        grid=(nb,),
        in_specs=[pl.BlockSpec((seq, LANES), lambda b: (b, 0)),
                  pl.BlockSpec((hf, 1), lambda b: (0, 0))],
        out_specs=pl.BlockSpec((1, hf, seq), lambda b: (b, 0, 0)),
        out_shape=jax.ShapeDtypeStruct((nb, hf, seq), F32),
        compiler_params=_cparams(("arbitrary",)),
        name="forget_cumsum",
    )(f_logit, b_forget.reshape(hf, 1))


def _dil_kernel(slope_ref, q_ref, k_ref, v_ref, o_ref, q4_ref, k4_ref, v4_ref, acc_ref, l_ref, m_ref,
                bias_ref, *, patterns, seq):
    slope = slope_ref[pl.program_id(1)]
    qb = Q_BLOCK
    row = lax.broadcasted_iota(I32, (qb, 2 * qb), 0)
    col = lax.broadcasted_iota(I32, (qb, 2 * qb), 1)
    delta = row + qb - col
    is_prev = col < qb
    neg_inf = F32(-jnp.inf)
    ones = jnp.ones((2 * qb, HEAD_DIM), BF16)

    sub = seq // DEINT
    chunk = min(256, sub)
    for r4 in range(DEINT):
        for c0 in range(0, sub, chunk):
            src = pl.ds(r4 + DEINT * c0, chunk, stride=DEINT)
            dst = pl.ds(r4 * sub + c0, chunk)
            q4_ref[dst, :] = q_ref[0, src, :]
            k4_ref[dst, :] = k_ref[0, src, :]
            v4_ref[dst, :] = v_ref[0, src, :]

    for pi, (window, dil) in enumerate(sorted(patterns, key=lambda wd: -wd[1])):
        reach = window // dil
        nblk = seq // dil // qb
        n_blocks = dil * nblk
        assert reach <= qb and nblk * qb * dil == seq and n_blocks % DIL_GROUP == 0
        sd = slope * (float(dil) * LOG2E)
        bias_ref[...] = jnp.where((delta >= 0) & (delta <= reach), -sd * delta.astype(F32), neg_inf)

        def block(idx, dil=dil, nblk=nblk, pi=pi):
            r = idx // nblk
            n = idx - r * nblk
            start = r + n * (qb * dil)
            cur = pl.ds(start, qb, stride=dil) if dil > 1 else pl.ds(pl.multiple_of(start, qb), qb)
            if dil % DEINT == 0:
                step = dil // DEINT
                first = (r % DEINT) * sub + r // DEINT
                lstart = first + n * (qb * step)
                lprev = jnp.maximum(lstart - qb * step, first)
                if step > 1:
                    src_cur, src_prev = pl.ds(lstart, qb, stride=step), pl.ds(lprev, qb, stride=step)
                else:
                    src_cur = pl.ds(pl.multiple_of(lstart, qb), qb)
                    src_prev = pl.ds(pl.multiple_of(lprev, qb), qb)
                q = q4_ref[src_cur, :].astype(BF16)
                kk = jnp.concatenate([k4_ref[src_prev, :], k4_ref[src_cur, :]], axis=0).astype(BF16)
                vv = jnp.concatenate([v4_ref[src_prev, :], v4_ref[src_cur, :]], axis=0).astype(BF16)
            else:
                pstart = jnp.maximum(start - qb * dil, r)
                prev = (pl.ds(pstart, qb, stride=dil) if dil > 1
                        else pl.ds(pl.multiple_of(pstart, qb), qb))
                q = q_ref[0, cur, :].astype(BF16)
                kk = jnp.concatenate([k_ref[0, prev, :], k_ref[0, cur, :]], axis=0).astype(BF16)
                vv = jnp.concatenate([v_ref[0, prev, :], v_ref[0, cur, :]], axis=0).astype(BF16)
            s = _dot_nt(q, kk) + bias_ref[...]
            s = jnp.where(is_prev & (n == 0), neg_inf, s)
            m_blk = jnp.max(s, axis=1, keepdims=True)
            p = jnp.exp2(s - m_blk)
            pv = _dot(p.astype(BF16), jnp.concatenate([vv, ones], axis=1))
            o_blk, l_blk = pv[:, :HEAD_DIM], pv[:, HEAD_DIM:]
            if pi == 0:
                m_ref[cur, :] = jnp.broadcast_to(m_blk, (qb, LANES))
                acc_ref[cur, :] = o_blk
                l_ref[cur, :] = l_blk
            else:
                m_old = m_ref[cur, :]
                m_new = jnp.maximum(m_old, m_blk)
                a_old = jnp.exp2(m_old - m_new)
                a_blk = jnp.exp2(m_blk - m_new)
                m_ref[cur, :] = m_new
                acc_ref[cur, :] = acc_ref[cur, :] * a_old + o_blk * a_blk
                l_ref[cur, :] = l_ref[cur, :] * a_old + l_blk * a_blk

        def body(g, carry, block=block):
            for u in range(DIL_GROUP):
                block(g * DIL_GROUP + u)
            return carry

        lax.fori_loop(0, n_blocks // DIL_GROUP, body, 0)

    chunk = 512
    for c0 in range(0, seq, chunk):
        rows = slice(c0, c0 + chunk)
        o_ref[0, rows, :] = (acc_ref[rows, :] / l_ref[rows, :]).astype(o_ref.dtype)


def _dilated_attention(qkv, slopes, nb, seq, n_heads):
    dh = HEAD_DIM
    kern = functools.partial(_dil_kernel, patterns=DIL_PATTERNS, seq=seq)
    blk = lambda off: pl.BlockSpec((1, seq, dh), lambda b, h, s: (b, 0, off + h))
    return pl.pallas_call(
        kern,
        grid_spec=pltpu.PrefetchScalarGridSpec(
            num_scalar_prefetch=1,
            grid=(nb, n_heads),
            in_specs=[blk(0), blk(n_heads), blk(2 * n_heads)],
            out_specs=pl.BlockSpec((1, seq, dh), lambda b, h, s: (b, 0, h)),
            scratch_shapes=[pltpu.VMEM((seq, dh), F32),
                            pltpu.VMEM((seq, dh), F32),
                            pltpu.VMEM((seq, dh), F32),
                            pltpu.VMEM((seq, dh), F32),
                            pltpu.VMEM((seq, LANES), F32),
                            pltpu.VMEM((seq, LANES), F32),
                            pltpu.VMEM((Q_BLOCK, 2 * Q_BLOCK), F32)]),
        out_shape=jax.ShapeDtypeStruct((nb, seq, n_heads * dh), BF16),
        compiler_params=_cparams(("arbitrary", "arbitrary")),
        name="dilated_attn",
    )(slopes, qkv, qkv, qkv)


def _fox_kernel(q_ref, k_ref, v_ref, c_ref, o_ref, m_scr, acc_scr, *, blk, hb):
    i = pl.program_id(2)
    dh = HEAD_DIM
    row = lax.broadcasted_iota(I32, (blk, blk), 0)
    col = lax.broadcasted_iota(I32, (blk, blk), 1)
    m_scr[...] = jnp.full(m_scr.shape, -jnp.inf, F32)
    acc_scr[...] = jnp.zeros(acc_scr.shape, F32)
    ones = jnp.ones((blk, dh), BF16)
    n_slab = blk // LANES

    def step(j, masked):
        rows = pl.ds(pl.multiple_of(j * blk, blk), blk)
        for h in range(hb):
            cols = slice(h * dh, (h + 1) * dh)
            s = _dot_nt(q_ref[0, :, cols], k_ref[0, rows, cols]) - c_ref[0, h, j]
            if masked:
                s = jnp.where(col <= row, s, -jnp.inf)
            m_old = m_scr[h]
            m_new = jnp.maximum(m_old, jnp.max(s, axis=1, keepdims=True))
            alpha = jnp.exp2(m_old - m_new)
            p = jnp.concatenate(
                [jnp.exp2(s[:, t * LANES:(t + 1) * LANES] - m_new) for t in range(n_slab)], axis=1)
            pv = _dot(p.astype(BF16), jnp.concatenate([v_ref[0, rows, cols], ones], axis=1))
            acc_scr[h] = jnp.concatenate([alpha, alpha], axis=1) * acc_scr[h] + pv
            m_scr[h] = m_new

    def body(j, carry):
        step(j, False)
        return carry

    lax.fori_loop(0, i, body, 0)
    step(i, True)
    for h in range(hb):
        acc = acc_scr[h]
        o_ref[0, :, h * dh:(h + 1) * dh] = (acc[:, :dh] / acc[:, dh:]).astype(o_ref.dtype)


def _fox_attention(qkv, cum, nb, seq, n_heads, blk):
    dh = HEAD_DIM
    hb = min(FOX_HEADS_PER_STEP, n_heads)
    assert n_heads % hb == 0
    ng = n_heads // hb
    w = hb * dh
    kern = functools.partial(_fox_kernel, blk=blk, hb=hb)
    return pl.pallas_call(
        kern,
        grid=(nb, ng, seq // blk),
        in_specs=[pl.BlockSpec((1, blk, w), lambda b, g, i: (b, i, g)),
                  pl.BlockSpec((1, seq, w), lambda b, g, i: (b, 0, ng + g)),
                  pl.BlockSpec((1, seq, w), lambda b, g, i: (b, 0, 2 * ng + g)),
                  pl.BlockSpec((1, hb, seq // blk, 1, blk), lambda b, g, i: (b, g, 0, 0, 0))],
        out_specs=pl.BlockSpec((1, blk, w), lambda b, g, i: (b, i, g)),
        out_shape=jax.ShapeDtypeStruct((nb, seq, n_heads * dh), BF16),
        scratch_shapes=[pltpu.VMEM((hb, blk, LANES), F32),
                        pltpu.VMEM((hb, blk, 2 * dh), F32)],
        compiler_params=_cparams(("arbitrary", "arbitrary", "arbitrary")),
        name="fox_attn",
    )(qkv, qkv, qkv, cum)


def _post_attn_kernel(od_ref, of_ref, x_ref, w_ref, ga_ref, scm_ref, shm_ref, g_ref, b_ref,
                      wr_ref, br_ref,
                      x1_ref, hp_ref, idx_ref, gate_ref, rank_ref, cnt_ref, carry_scr,
                      *, n_experts, top_k, dil_width, sub):
    i = pl.program_id(0)

    @pl.when(i == 0)
    def _():
        carry_scr[...] = jnp.zeros_like(carry_scr)

    for r0 in range(0, x_ref.shape[0], sub):
        _post_attn_rows(slice(r0, r0 + sub), od_ref, of_ref, x_ref, w_ref, ga_ref, scm_ref, shm_ref,
                        g_ref, b_ref, wr_ref, br_ref, x1_ref, hp_ref, idx_ref, gate_ref, rank_ref,
                        carry_scr, n_experts=n_experts, top_k=top_k, dil_width=dil_width)
    cnt_ref[...] = jnp.broadcast_to(carry_scr[...], cnt_ref.shape)


def _post_attn_rows(rows, od_ref, of_ref, x_ref, w_ref, ga_ref, scm_ref, shm_ref, g_ref, b_ref,
                    wr_ref, br_ref, x1_ref, hp_ref, idx_ref, gate_ref, rank_ref, carry_scr,
                    *, n_experts, top_k, dil_width):
    attn = (_dot(od_ref[rows, :], w_ref[:dil_width, :]) + _dot(of_ref[rows, :], w_ref[dil_width:, :]))
    x1 = _layer_norm(DEEPNORM_ALPHA * x_ref[rows, :] + ga_ref[0] * attn, g_ref[...], b_ref[...])
    x1_ref[rows, :] = x1
    h = x1 * (1.0 + scm_ref[0]) + shm_ref[0]
    h_hi, h_lo = _split_bf16(h)
    hp_ref[rows, :] = h

    logits = _dot_split(h_hi, h_lo, wr_ref[...]) + br_ref[...]
    tm = logits.shape[0]
    lane = lax.broadcasted_iota(I32, (tm, LANES), 1)
    work = jnp.where(lane < n_experts, logits, -jnp.inf)
    vals, idxs = [], []
    for _ in range(top_k):
        mx = jnp.max(work, axis=1, keepdims=True)
        ix = jnp.min(jnp.where(work == mx, lane, LANES), axis=1, keepdims=True)
        vals.append(mx)
        idxs.append(ix)
        work = jnp.where(lane == ix, -jnp.inf, work)
    exps = [jnp.exp(v - vals[0]) for v in vals]
    denom = exps[0]
    for e in exps[1:]:
        denom = denom + e

    onehot = jnp.zeros((tm, LANES), F32)
    for ix in idxs:
        onehot = onehot + (lane == ix).astype(F32)
    r_i = lax.broadcasted_iota(I32, (tm, tm), 0)
    c_i = lax.broadcasted_iota(I32, (tm, tm), 1)
    tri = (c_i < r_i).astype(BF16)
    before = _dot(tri, onehot.astype(BF16)) + carry_scr[...]

    idx_out = jnp.zeros((tm, LANES), I32)
    gate_out = jnp.zeros((tm, LANES), F32)
    rank_out = jnp.zeros((tm, LANES), I32)
    for k in range(top_k):
        rank_k = jnp.sum(jnp.where(lane == idxs[k], before, 0.0), axis=1, keepdims=True)
        idx_out = jnp.where(lane == k, idxs[k], idx_out)
        gate_out = jnp.where(lane == k, exps[k] / denom, gate_out)
        rank_out = jnp.where(lane == k, rank_k.astype(I32), rank_out)
    gate_ref[rows, :] = gate_out
    idx_ref[:, rows] = idx_out.T[:idx_ref.shape[0], :]
    rank_ref[:, rows] = rank_out.T[:rank_ref.shape[0], :]
    carry_scr[...] = carry_scr[...] + jnp.sum(onehot, axis=0, keepdims=True)


def _post_attn(o_dil, o_fox, x2, w_out_bf, g_a, sc_m, sh_m, ln_g, ln_b, w_router, b_router, seq):
    t, d_model = x2.shape
    dil_width = o_dil.shape[1]
    n_experts = w_router.shape[1]
    tm, sub = 256, 256
    wr = _split_weight(jnp.pad(w_router, ((0, 0), (0, LANES - n_experts))))
    br = jnp.pad(b_router, (0, LANES - n_experts)).reshape(1, LANES)
    kern = functools.partial(_post_attn_kernel, n_experts=n_experts, top_k=TOP_K, dil_width=dil_width,
                             sub=sub)
    row = lambda w: pl.BlockSpec((tm, w), lambda i: (i, 0))
    col = pl.BlockSpec((8, tm), lambda i: (0, i))
    const = lambda r, w: pl.BlockSpec((r, w), lambda i: (0, 0))
    bvec = pl.BlockSpec((1, 1, d_model), lambda i: (i * tm // seq, 0, 0))
    return pl.pallas_call(
        kern,
        grid=(t // tm,),
        in_specs=[row(dil_width), row(o_fox.shape[1]), row(d_model), const(d_model, d_model),
                  bvec, bvec, bvec, const(1, d_model), const(1, d_model),
                  const(d_model, 2 * LANES), const(1, LANES)],
        out_specs=[row(d_model), row(d_model), col, row(LANES), col,
                   pl.BlockSpec((8, LANES), lambda i: (0, 0))],
        out_shape=[jax.ShapeDtypeStruct((t, d_model), F32),
                   jax.ShapeDtypeStruct((t, d_model), F32),
                   jax.ShapeDtypeStruct((8, t), I32),
                   jax.ShapeDtypeStruct((t, LANES), F32),
                   jax.ShapeDtypeStruct((8, t), I32),
                   jax.ShapeDtypeStruct((8, LANES), F32)],
        scratch_shapes=[pltpu.VMEM((1, LANES), F32)],
        compiler_params=_cparams(("arbitrary",)),
        name="post_attn",
    )(o_dil, o_fox, x2, w_out_bf, g_a, sc_m, sh_m, ln_g.reshape(1, -1), ln_b.reshape(1, -1),
      wr, br)


def _scatter_kernel(dest_ref, fill_lo_ref, fill_hi_ref, used_ref, hp_ref, xs_ref, zbuf, sem, zsem,
                    *, top_k):
    i = pl.program_id(0)
    tm = hp_ref.shape[0]
    rb = ROW_BLOCK

    @pl.when(i == 0)
    def _():
        zbuf[...] = jnp.zeros(zbuf.shape, zbuf.dtype)

        def tails(wait):
            def row(r, c):
                copy = pltpu.make_async_copy(zbuf.at[pl.ds(0, 1)], xs_ref.at[pl.ds(r, 1)], zsem.at[0])
                copy.wait() if wait else copy.start()
                return c

            def expert(e, c):
                return lax.fori_loop(fill_lo_ref[e], fill_hi_ref[e], row, c)

            lax.fori_loop(0, fill_lo_ref.shape[0], expert, 0)

        def rest(wait):
            def blk(b, c):
                copy = pltpu.make_async_copy(zbuf, xs_ref.at[pl.ds(pl.multiple_of(b * rb, rb), rb)],
                                             zsem.at[1])
                copy.wait() if wait else copy.start()
                return c

            lax.fori_loop(used_ref[0], xs_ref.shape[0] // rb, blk, 0)

        tails(False)
        rest(False)
        tails(True)
        rest(True)

    n_tokens = dest_ref.shape[0] // top_k
    base = i * tm
    for t in range(tm):
        for k in range(top_k):
            d = dest_ref[base + (k * n_tokens + t)]
            pltpu.make_async_copy(hp_ref.at[pl.ds(t, 1)], xs_ref.at[pl.ds(d, 1)], sem).start(
                priority=(t * top_k + k) % 2)

    for k in range(top_k):
        pltpu.make_async_copy(hp_ref, xs_ref.at[pl.ds(0, tm)], sem).wait()


def _scatter_rows(dest_flat, fill_lo, fill_hi, used_blocks, hp, n_rows):
    t, w = hp.shape
    tm = 256
    kern = functools.partial(_scatter_kernel, top_k=TOP_K)
    return pl.pallas_call(
        kern,
        grid_spec=pltpu.PrefetchScalarGridSpec(
            num_scalar_prefetch=4,
            grid=(t // tm,),
            in_specs=[pl.BlockSpec((tm, w), lambda i, *_: (i, 0))],
            out_specs=pl.BlockSpec(memory_space=pl.ANY),
            scratch_shapes=[pltpu.VMEM((ROW_BLOCK, w), F32),
                            pltpu.SemaphoreType.DMA(()),
                            pltpu.SemaphoreType.DMA((2,))]),
        out_shape=jax.ShapeDtypeStruct((n_rows, w), F32),
        compiler_params=_cparams(("arbitrary",)),
        name="scatter_rows",
    )(dest_flat, fill_lo, fill_hi, used_blocks, hp)


def _moe_kernel(iexp_ref, irow_ref, inu_ref, used_ref, xs_ref, wg_ref, wu_ref, wd_ref, bg_ref, bu_ref,
                bd_ref, y_ref, xbuf, yacc, sem_in, sem_out, *, n_chunks):
    del iexp_ref
    it = pl.program_id(0)
    f = pl.program_id(1)
    n_items = pl.num_programs(0)
    nu = inu_ref[it]
    rb = ROW_BLOCK
    rows_of = lambda k, n=1: pl.ds(pl.multiple_of(k * rb, rb), n * rb)

    def x_copy(item, k):
        src = pl.ds(pl.multiple_of(irow_ref[item] + k * rb, rb), rb)
        return pltpu.make_async_copy(xs_ref.at[src], xbuf.at[rows_of(k)], sem_in)

    def y_copy(k):
        dst = pl.ds(pl.multiple_of(irow_ref[it] + k * rb, rb), rb)
        return pltpu.make_async_copy(yacc.at[rows_of(k)], y_ref.at[dst], sem_out)

    def for_units(n, fn):
        def body(k, c):
            fn(k)
            return c
        lax.fori_loop(0, n, body, 0)

    def init_unit(k):
        yacc[rows_of(k), :] = jnp.broadcast_to(bd_ref[0], (rb, yacc.shape[1]))

    first = f == 0
    last = f == n_chunks - 1
    n_prev = inu_ref[jnp.maximum(it - 1, 0)]

    def drain_previous():
        @pl.when(first & (it > 0))
        def _():
            for_units(n_prev, lambda k: y_copy(k).wait())

    @pl.when(nu == 0)
    def _():
        drain_previous()

    @pl.when(nu > 0)
    def _():
        @pl.when(first & (it == 0))
        def _():
            for_units(nu, lambda k: x_copy(0, k).start())

        wg = wg_ref[0].astype(BF16)
        wu = wu_ref[0].astype(BF16)
        wd = wd_ref[0].astype(BF16)
        bg = bg_ref[0]
        bu = bu_ref[0]

        drain_previous()

        @pl.when(first)
        def _():
            for_units(nu, init_unit)
            for_units(nu, lambda k: x_copy(it, k).wait())

        def block(k, n):
            xb = xbuf[rows_of(k, n), :].astype(BF16)
            g = _dot(xb, wg) + bg
            u = _dot(xb, wu) + bu
            g = jnp.minimum(g, SWIGLU_LIMIT)
            u = jnp.clip(u, -SWIGLU_LIMIT, SWIGLU_LIMIT)
            act = (u + 1.0) * (g * jax.nn.sigmoid(SWIGLU_ALPHA * g))
            yacc[rows_of(k, n), :] += _dot(act.astype(BF16), wd)

        def quad(t, c):
            block(4 * t, 2)
            block(4 * t + 2, 2)
            return c

        n_quads = lax.shift_right_logical(nu, 2)
        lax.fori_loop(0, n_quads, quad, 0)
        done = n_quads * 4

        @pl.when((nu & 2) != 0)
        def _():
            block(done, 2)

        @pl.when((nu & 1) != 0)
        def _():
            block(done + (nu & 2), 1)

        @pl.when(last)
        def _():
            for_units(nu, lambda k: y_copy(k).start())
            nxt = jnp.minimum(it + 1, n_items - 1)
            n_next = jnp.where(it + 1 < n_items, inu_ref[nxt], 0)
            for_units(n_next, lambda k: x_copy(nxt, k).start())

            @pl.when(it == n_items - 1)
            def _():
                for_units(nu, lambda k: y_copy(k).wait())

    @pl.when((it == n_items - 1) & last)
    def _():
        lo = used_ref[0]
        hi = y_ref.shape[0] // rb
        dst_of = lambda b: pl.ds(pl.multiple_of(b * rb, rb), rb)
        yacc[pl.ds(0, rb), :] = jnp.zeros((rb, yacc.shape[1]), yacc.dtype)

        def start(b, c):
            pltpu.make_async_copy(yacc.at[pl.ds(0, rb)], y_ref.at[dst_of(b)], sem_out).start()
            return c

        def wait(b, c):
            pltpu.make_async_copy(yacc.at[pl.ds(0, rb)], y_ref.at[dst_of(b)], sem_out).wait()
            return c

        lax.fori_loop(lo, hi, start, 0)
        lax.fori_loop(lo, hi, wait, 0)


def _moe_experts(item_exp, item_row, item_nblk, used_blocks, xs, w_gate_up, b_gate_up, w_down, b_down):
    n_rows = xs.shape[0]
    n_exp, d_model, two_f = w_gate_up.shape
    d_ff = two_f // 2
    fc = FF_CHUNK
    n_chunks = d_ff // fc
    n_items = item_exp.shape[0]
    kern = functools.partial(_moe_kernel, n_chunks=n_chunks)

    def fsel(it, f, nbk):
        return jnp.where(nbk[it] > 0, f, n_chunks - 1)

    return pl.pallas_call(
        kern,
        grid_spec=pltpu.PrefetchScalarGridSpec(
            num_scalar_prefetch=4,
            grid=(n_items, n_chunks),
            in_specs=[
                pl.BlockSpec(memory_space=pl.ANY),
                pl.BlockSpec((1, d_model, fc), lambda it, f, ex, ro, nbk, us: (ex[it], 0, fsel(it, f, nbk))),
                pl.BlockSpec((1, d_model, fc),
                             lambda it, f, ex, ro, nbk, us: (ex[it], 0, n_chunks + fsel(it, f, nbk))),
                pl.BlockSpec((1, fc, d_model), lambda it, f, ex, ro, nbk, us: (ex[it], fsel(it, f, nbk), 0)),
                pl.BlockSpec((1, 1, fc), lambda it, f, ex, ro, nbk, us: (ex[it], 0, fsel(it, f, nbk))),
                pl.BlockSpec((1, 1, fc),
                             lambda it, f, ex, ro, nbk, us: (ex[it], 0, n_chunks + fsel(it, f, nbk))),
                pl.BlockSpec((1, 1, d_model), lambda it, f, ex, ro, nbk, us: (ex[it], 0, 0)),
            ],
            out_specs=pl.BlockSpec(memory_space=pl.ANY),
            scratch_shapes=[pltpu.VMEM((ITEM_ROWS, d_model), F32),
                            pltpu.VMEM((ITEM_ROWS, d_model), F32),
                            pltpu.SemaphoreType.DMA(()),
                            pltpu.SemaphoreType.DMA(())]),
        out_shape=jax.ShapeDtypeStruct((n_rows, d_model), F32),
        compiler_params=_cparams(("arbitrary", "arbitrary")),
        name="moe_experts",
    )(item_exp, item_row, item_nblk, used_blocks, xs, w_gate_up, w_gate_up, w_down,
      b_gate_up.reshape(n_exp, 1, two_f), b_gate_up.reshape(n_exp, 1, two_f),
      b_down.reshape(n_exp, 1, d_model))


def _combine_kernel(dest_ref, y_ref, x1_ref, gate_ref, gm_ref, g_ref, b_ref, o_ref, ybuf, sem, *, top_k):
    i = pl.program_id(0)
    n_steps = pl.num_programs(0)
    tm = x1_ref.shape[0]
    slot = i % 2

    def gather(step, slot_):
        n_tokens = dest_ref.shape[0] // top_k
        base = step * tm
        for t in range(tm):
            for k in range(top_k):
                d = dest_ref[base + (k * n_tokens + t)]
                pltpu.make_async_copy(y_ref.at[pl.ds(d, 1)], ybuf.at[slot_, pl.ds(k * tm + t, 1)],
                                      sem.at[slot_]).start(priority=(t * top_k + k) % 2)

    @pl.when(i == 0)
    def _():
        gather(0, 0)

    for nxt in range(2):
        @pl.when((i + 1 < n_steps) & (slot == 1 - nxt))
        def _(nxt=nxt):
            gather(i + 1, nxt)

    pltpu.make_async_copy(y_ref.at[pl.ds(0, top_k * tm)], ybuf.at[slot], sem.at[slot]).wait()
    gates = gate_ref[...]
    ffn = gates[:, 0:1] * ybuf[slot, pl.ds(0, tm), :]
    for k in range(1, top_k):
        ffn = ffn + gates[:, k:k + 1] * ybuf[slot, pl.ds(k * tm, tm), :]
    o_ref[...] = _layer_norm(DEEPNORM_ALPHA * x1_ref[...] + gm_ref[0] * ffn, g_ref[...], b_ref[...])


def _combine(dest_flat, y, x1, gates, g_m, ln_g, ln_b, seq):
    t, d_model = x1.shape
    tm = 128
    kern = functools.partial(_combine_kernel, top_k=TOP_K)
    return pl.pallas_call(
        kern,
        grid_spec=pltpu.PrefetchScalarGridSpec(
            num_scalar_prefetch=1,
            grid=(t // tm,),
            in_specs=[pl.BlockSpec(memory_space=pl.ANY),
                      pl.BlockSpec((tm, d_model), lambda i, d: (i, 0)),
                      pl.BlockSpec((tm, LANES), lambda i, d: (i, 0)),
                      pl.BlockSpec((1, 1, d_model), lambda i, d: (i * tm // seq, 0, 0)),
                      pl.BlockSpec((1, d_model), lambda i, d: (0, 0)),
                      pl.BlockSpec((1, d_model), lambda i, d: (0, 0))],
            out_specs=pl.BlockSpec((tm, d_model), lambda i, d: (i, 0)),
            scratch_shapes=[pltpu.VMEM((2, TOP_K * tm, d_model), F32),
                            pltpu.SemaphoreType.DMA((2,))]),
        out_shape=jax.ShapeDtypeStruct((t, d_model), F32),
        compiler_params=_cparams(("arbitrary",)),
        name="combine_ln2",
    )(dest_flat, y, x1, gates, g_m, ln_g.reshape(1, -1), ln_b.reshape(1, -1))


def _routing_plan(counts, idx, rank, n_tokens):
    n_exp = counts.shape[0]

    def take(table, index):
        hit = index[..., None] == jnp.arange(n_exp, dtype=I32)
        return jnp.sum(jnp.where(hit, table, 0), axis=-1)

    padded = (counts + ROW_BLOCK - 1) // ROW_BLOCK * ROW_BLOCK
    pad_end = jnp.cumsum(padded)
    pad_start = pad_end - padded
    dest = (take(pad_start, idx) + rank).reshape(-1).astype(I32)

    max_rows = (n_tokens * TOP_K + n_exp * (ROW_BLOCK - 1) + ROW_BLOCK - 1) // ROW_BLOCK * ROW_BLOCK
    max_items = n_exp + max_rows // ITEM_ROWS
    items_per = (padded + ITEM_ROWS - 1) // ITEM_ROWS
    items_end = jnp.cumsum(items_per)
    total = items_end[-1]
    it = jnp.arange(max_items, dtype=I32)
    last = jnp.maximum(total - 1, 0)
    it_c = jnp.minimum(it, last)
    exp_of = jnp.sum((items_end[None, :] <= it_c[:, None]).astype(I32), axis=1)
    exp_of = jnp.minimum(exp_of, n_exp - 1)
    local = it_c - take(items_end - items_per, exp_of)
    row0 = take(pad_start, exp_of) + local * ITEM_ROWS
    rows = jnp.minimum(take(padded, exp_of) - local * ITEM_ROWS, ITEM_ROWS)
    nblk = jnp.where(it < total, rows // ROW_BLOCK, 0)
    used_blocks = (pad_end[-1:] // ROW_BLOCK).astype(I32)
    fill = ((pad_start + counts).astype(I32), pad_end.astype(I32))
    return dest, fill, (exp_of.astype(I32), row0.astype(I32), nblk.astype(I32), used_blocks), max_rows


def kernel(x, c, w_ada, b_ada, w_in, b_forget, w_out, ln1_g, ln1_b, w_router, b_router,
           w_gate_up, b_gate_up, w_down, b_down, ln2_g, ln2_b):
    nb, seq, d_model = x.shape
    n_heads = d_model // HEAD_DIM
    n_dil = n_heads // 2
    n_fox = n_heads - n_dil
    dil_width, fox_width = n_dil * HEAD_DIM, n_fox * HEAD_DIM
    t = nb * seq
    layer = 0

    mod = _ada_mod(c, w_ada[layer], b_ada[layer])
    sh_a, sc_a, g_a, sh_m, sc_m, g_m = [m.reshape(nb, 1, d_model) for m in jnp.split(mod, 6, axis=-1)]

    x2 = x.reshape(t, d_model)
    qkv_dil, qkv_fox, f_logit = _in_proj(x2, sc_a, sh_a, w_in, layer, seq, dil_width, fox_width, n_fox)

    slopes = 2.0 ** (-8.0 * jnp.arange(1, n_dil + 1, dtype=F32) / n_dil)
    o_dil = _dilated_attention(qkv_dil.reshape(nb, seq, 3 * dil_width), slopes, nb, seq, n_dil)

    fox_blk = 512
    cum = _forget_cumsum(f_logit, b_forget[layer], nb, seq).reshape(
        nb, n_fox, seq // fox_blk, 1, fox_blk)
    o_fox = _fox_attention(qkv_fox.reshape(nb, seq, 3 * fox_width), cum, nb, seq, n_fox, fox_blk)

    x1, hp, idx, gates, rank, cnt = _post_attn(
        o_dil.reshape(t, dil_width), o_fox.reshape(t, fox_width), x2, w_out[layer].astype(BF16),
        g_a, sc_m, sh_m, ln1_g[layer], ln1_b[layer], w_router[layer], b_router[layer], seq)

    counts = cnt[0, :N_EXPERTS].astype(I32)
    dest, fill, items, max_rows = _routing_plan(counts, idx[:TOP_K], rank[:TOP_K], t)
    xs = _scatter_rows(dest, *fill, items[3], hp, max_rows)
    y = _moe_experts(*items, xs, w_gate_up[layer], b_gate_up[layer], w_down[layer], b_down[layer])
    out = _combine(dest, y, x1, gates, g_m, ln2_g[layer], ln2_b[layer], seq)
    return out.reshape(nb, seq, d_model)
```

```python
import functools
import math

import jax
import jax.numpy as jnp
from jax import lax
from jax.experimental import pallas as pl
from jax.experimental.pallas import tpu as pltpu

F32 = jnp.float32
BF16 = jnp.bfloat16
I32 = jnp.int32
U32 = jnp.uint32

HEAD_DIM = 128
Q_BLOCK = 128
DIL_PATTERNS = ((128, 1), (512, 4), (2048, 16))
N_EXPERTS = 32
TOP_K = 4
SWIGLU_LIMIT = 7.0
SWIGLU_ALPHA = 1.702
LN_EPS = 1e-5
DEPTH = 1
DEEPNORM_ALPHA = (2.0 * DEPTH) ** 0.25

LOG2E = math.log2(math.e)
DIL_GROUP = 8
DEINT = 4
FOX_HEADS_PER_STEP = 4

LANES = 128
VMEM_LIMIT = 56 * 1024 * 1024

ROW_BLOCK = 128
ITEM_ROWS = 1280
FF_CHUNK = 512


def _cparams(sem):
    return pltpu.CompilerParams(dimension_semantics=sem, vmem_limit_bytes=VMEM_LIMIT)


def _dot(a, b):
    return jnp.dot(a, b, preferred_element_type=F32)


def _dot_nt(a, b):
    return lax.dot_general(a, b, (((1,), (1,)), ((), ())), preferred_element_type=F32)


def _split_bf16(x):
    hi = x.astype(BF16)
    lo = (x - hi.astype(F32)).astype(BF16)
    return hi, lo


def _split_weight(w):
    return jnp.concatenate(_split_bf16(w), axis=1)


def _dot_split(x_hi, x_lo, w_cat):
    r = _dot(x_hi, w_cat) + _dot(x_lo, w_cat)
    return r[:, :LANES] + r[:, LANES:]


def _layer_norm(y, g, b):
    mu = jnp.mean(y, axis=-1, keepdims=True)
    yc = y - mu
    var = jnp.mean(yc * yc, axis=-1, keepdims=True)
    return yc * lax.rsqrt(var + LN_EPS) * g + b


def _ada_kernel(ct_ref, w_ref, b_ref, o_ref):
    ct = ct_ref[...]
    s = ct * jax.nn.sigmoid(ct)
    d_model, nb = ct.shape
    tn = w_ref.shape[1]
    sb = [jnp.broadcast_to(s[:, b:b + 1], (d_model, LANES)) for b in range(nb)]
    for j in range(tn // LANES):
        cols = slice(j * LANES, (j + 1) * LANES)
        w = w_ref[:, cols]
        for b in range(nb):
            acc = jnp.sum(w * sb[b], axis=0, keepdims=True)
            o_ref[b:b + 1, cols] = acc + b_ref[:, cols]


def _ada_mod(c, w, b):
    nb, d_model = c.shape
    n = w.shape[1]
    tn = 1536 if n % 1536 == 0 else LANES
    return pl.pallas_call(
        _ada_kernel,
        grid=(n // tn,),
        in_specs=[pl.BlockSpec((d_model, nb), lambda j: (0, 0)),
                  pl.BlockSpec((d_model, tn), lambda j: (0, j)),
                  pl.BlockSpec((1, tn), lambda j: (0, j))],
        out_specs=pl.BlockSpec((nb, tn), lambda j: (0, j)),
        out_shape=jax.ShapeDtypeStruct((nb, n), F32),
        compiler_params=_cparams(("arbitrary",)),
        name="ada_mod",
    )(c.T, w, b.reshape(1, n))


def _inproj_kernel(x_ref, sc_ref, sh_ref, w_ref, wf_ref,
                   odil_ref, ofox_ref, of_ref, h_scr, *, n_dil_tiles, q_tiles, scale):
    j = pl.program_id(1)

    @pl.when(j == 0)
    def _():
        h = x_ref[...] * (1.0 + sc_ref[0]) + sh_ref[0]
        h_hi, h_lo = _split_bf16(h)
        h_scr[...] = h_hi
        of_ref[...] = _dot_split(h_hi, h_lo, wf_ref[...])

    acc = _dot(h_scr[...], w_ref[...])
    is_q =(j < q_tiles) | ((j >= n_dil_tiles) & (j < n_dil_tiles + q_tiles))
    acc = acc * jnp.where(is_q, scale, 1.0).astype(F32)

    @pl.when(j < n_dil_tiles)
    def _():
        odil_ref[...] = acc

    @pl.when(j >= n_dil_tiles)
    def _():
        ofox_ref[...] = acc.astype(BF16)


def _in_proj(x2, sc, sh, w_in, layer, seq, dil_width, fox_width, n_fox_heads):
    t, d_model = x2.shape
    tm, tn = min(1024, seq), min(1024, dil_width)
    n_dil_tiles = 3 * dil_width // tn
    n_fox_tiles = 3 * fox_width // tn
    q_tiles = dil_width // tn
    assert dil_width == fox_width and dil_width % tn == 0 and seq % tm == 0
    main = 3 * dil_width + 3 * fox_width
    wf = _split_weight(jnp.pad(w_in[layer, :, main:], ((0, 0), (0, LANES - n_fox_heads))))
    kern = functools.partial(_inproj_kernel, n_dil_tiles=n_dil_tiles, q_tiles=q_tiles,
                             scale=LOG2E / math.sqrt(HEAD_DIM))
    bidx = lambda i, j: (i * tm // seq, 0, 0)
    return pl.pallas_call(
        kern,
        grid=(t // tm, n_dil_tiles + n_fox_tiles),
        in_specs=[pl.BlockSpec((tm, d_model), lambda i, j: (i, 0)),
                  pl.BlockSpec((1, 1, d_model), bidx),
                  pl.BlockSpec((1, 1, d_model), bidx),
                  pl.BlockSpec((None, d_model, tn), lambda i, j: (layer, 0, j)),
                  pl.BlockSpec((d_model, 2 * LANES), lambda i, j: (0, 0))],
        out_specs=[pl.BlockSpec((tm, tn), lambda i, j: (i, jnp.minimum(j, n_dil_tiles - 1))),
                   pl.BlockSpec((tm, tn), lambda i, j: (i, jnp.maximum(j - n_dil_tiles, 0))),
                   pl.BlockSpec((tm, LANES), lambda i, j: (i, 0))],
        out_shape=[jax.ShapeDtypeStruct((t, 3 * dil_width), F32),
                   jax.ShapeDtypeStruct((t, 3 * fox_width), BF16),
                   jax.ShapeDtypeStruct((t, LANES), F32)],
        scratch_shapes=[pltpu.VMEM((tm, d_model), BF16)],
        compiler_params=_cparams(("arbitrary", "arbitrary")),
        name="in_proj",
    )(x2, sc, sh, w_in.astype(BF16), wf)


def _cum_kernel(f_ref, b_ref, o_ref):
    hf = o_ref.shape[1]
    z = f_ref[...].T[:hf, :] + b_ref[...]
    c = jnp.minimum(z, 0.0) - jnp.log1p(jnp.exp(-jnp.abs(z)))
    seq = z.shape[1]
    lane = lax.broadcasted_iota(I32, z.shape, 1)
    shift = 1
    while shift < seq:
        c = c + jnp.where(lane >= shift, pltpu.roll(c, shift, axis=1), 0.0)
        shift *= 2
    o_ref[0] = c * LOG2E


def _forget_cumsum(f_logit, b_forget, nb, seq):
    hf = b_forget.shape[0]
    return pl.pallas_call(
        _cum_kernel,
        grid=(nb,),
        in_specs=[pl.BlockSpec((seq, LANES), lambda b: (b, 0)),
                  pl.BlockSpec((hf, 1), lambda b: (0, 0))],
        out_specs=pl.BlockSpec((1, hf, seq), lambda b: (b, 0, 0)),
        out_shape=jax.ShapeDtypeStruct((nb, hf, seq), F32),
        compiler_params=_cparams(("arbitrary",)),
        name="forget_cumsum",
    )(f_logit, b_forget.reshape(hf, 1))


def _dil_kernel(slope_ref, q_ref, k_ref, v_ref, o_ref, q4_ref, k4_ref, v4_ref, acc_ref, l_ref, m_ref,
                bias_ref, *, patterns, seq):
    slope = slope_ref[pl.program_id(1)]
    qb = Q_BLOCK
    row = lax.broadcasted_iota(I32, (qb, 2 * qb), 0)
    col = lax.broadcasted_iota(I32, (qb, 2 * qb), 1)
    delta = row + qb - col
    is_prev = col < qb
    neg_inf = F32(-jnp.inf)
    ones = jnp.ones((2 * qb, HEAD_DIM), BF16)

    sub = seq // DEINT
    chunk = min(256, sub)
    for r4 in range(DEINT):
        for c0 in range(0, sub, chunk):
            src = pl.ds(r4 + DEINT * c0, chunk, stride=DEINT)
            dst = pl.ds(r4 * sub + c0, chunk)
            q4_ref[dst, :] = q_ref[0, src, :]
            k4_ref[dst, :] = k_ref[0, src, :]
            v4_ref[dst, :] = v_ref[0, src, :]

    for pi, (window, dil) in enumerate(sorted(patterns, key=lambda wd: -wd[1])):
        reach = window // dil
        nblk = seq // dil // qb
        n_blocks = dil * nblk
        assert reach <= qb and nblk * qb * dil == seq and n_blocks % DIL_GROUP == 0
        sd = slope * (float(dil) * LOG2E)
        bias_ref[...] = jnp.where((delta >= 0) & (delta <= reach), -sd * delta.astype(F32), neg_inf)

        def block(idx, dil=dil, nblk=nblk, pi=pi):
            r = idx // nblk
            n = idx - r * nblk
            start = r + n * (qb * dil)
            cur = pl.ds(start, qb, stride=dil) if dil > 1 else pl.ds(pl.multiple_of(start, qb), qb)
            if dil % DEINT == 0:
                step = dil // DEINT
                first = (r % DEINT) * sub + r // DEINT
                lstart = first + n * (qb * step)
                lprev = jnp.maximum(lstart - qb * step, first)
                if step > 1:
                    src_cur, src_prev = pl.ds(lstart, qb, stride=step), pl.ds(lprev, qb, stride=step)
                else:
                    src_cur = pl.ds(pl.multiple_of(lstart, qb), qb)
                    src_prev = pl.ds(pl.multiple_of(lprev, qb), qb)
                q = q4_ref[src_cur, :].astype(BF16)
                kk = jnp.concatenate([k4_ref[src_prev, :], k4_ref[src_cur, :]], axis=0).astype(BF16)
                vv = jnp.concatenate([v4_ref[src_prev, :], v4_ref[src_cur, :]], axis=0).astype(BF16)
            else:
                pstart = jnp.maximum(start - qb * dil, r)
                prev = (pl.ds(pstart, qb, stride=dil) if dil > 1
                        else pl.ds(pl.multiple_of(pstart, qb), qb))
                q = q_ref[0, cur, :].astype(BF16)
                kk = jnp.concatenate([k_ref[0, prev, :], k_ref[0, cur, :]], axis=0).astype(BF16)
                vv = jnp.concatenate([v_ref[0, prev, :], v_ref[0, cur, :]], axis=0).astype(BF16)
            s = _dot_nt(q, kk) + bias_ref[...]
            s = jnp.where(is_prev & (n == 0), neg_inf, s)
            m_blk = jnp.max(s, axis=1, keepdims=True)
            p = jnp.exp2(s - m_blk)
            pv = _dot(p.astype(BF16), jnp.concatenate([vv, ones], axis=1))
            o_blk, l_blk = pv[:, :HEAD_DIM], pv[:, HEAD_DIM:]
            if pi == 0:
                m_ref[cur, :] = jnp.broadcast_to(m_blk, (qb, LANES))
                acc_ref[cur, :] = o_blk
                l_ref[cur, :] = l_blk
            else:
                m_old = m_ref[cur, :]
                m_new = jnp.maximum(m_old, m_blk)
                a_old = jnp.exp2(m_old - m_new)
                a_blk = jnp.exp2(m_blk - m_new)
                m_ref[cur, :] = m_new
                acc_ref[cur, :] = acc_ref[cur, :] * a_old + o_blk * a_blk
                l_ref[cur, :] = l_ref[cur, :] * a_old + l_blk * a_blk

        def body(g, carry, block=block):
            for u in range(DIL_GROUP):
                block(g * DIL_GROUP + u)
            return carry

        lax.fori_loop(0, n_blocks // DIL_GROUP, body, 0)

    chunk = 512
    for c0 in range(0, seq, chunk):
        rows = slice(c0, c0 + chunk)
        o_ref[0, rows, :] = (acc_ref[rows, :] / l_ref[rows, :]).astype(o_ref.dtype)


def _dilated_attention(qkv, slopes, nb, seq, n_heads):
    dh = HEAD_DIM
    kern = functools.partial(_dil_kernel, patterns=DIL_PATTERNS, seq=seq)
    blk = lambda off: pl.BlockSpec((1, seq, dh), lambda b, h, s: (b, 0, off + h))
    return pl.pallas_call(
        kern,
        grid_spec=pltpu.PrefetchScalarGridSpec(
            num_scalar_prefetch=1,
            grid=(nb, n_heads),
            in_specs=[blk(0), blk(n_heads), blk(2 * n_heads)],
            out_specs=pl.BlockSpec((1, seq, dh), lambda b, h, s: (b, 0, h)),
            scratch_shapes=[pltpu.VMEM((seq, dh), F32),
                            pltpu.VMEM((seq, dh), F32),
                            pltpu.VMEM((seq, dh), F32),
                            pltpu.VMEM((seq, dh), F32),
                            pltpu.VMEM((seq, LANES), F32),
                            pltpu.VMEM((seq, LANES), F32),
                            pltpu.VMEM((Q_BLOCK, 2 * Q_BLOCK), F32)]),
        out_shape=jax.ShapeDtypeStruct((nb, seq, n_heads * dh), BF16),
        compiler_params=_cparams(("arbitrary", "arbitrary")),
        name="dilated_attn",
    )(slopes, qkv, qkv, qkv)


def _fox_kernel(q_ref, k_ref, v_ref, c_ref, o_ref, m_scr, acc_scr, *, blk, hb):
    i = pl.program_id(2)
    dh = HEAD_DIM
    row = lax.broadcasted_iota(I32, (blk, blk), 0)
    col = lax.broadcasted_iota(I32, (blk, blk), 1)
    m_scr[...] = jnp.full(m_scr.shape, -jnp.inf, F32)
    acc_scr[...] = jnp.zeros(acc_scr.shape, F32)
    ones = jnp.ones((blk, dh), BF16)
    n_slab = blk // LANES

    def step(j, masked):
        rows = pl.ds(pl.multiple_of(j * blk, blk), blk)
        for h in range(hb):
            cols = slice(h * dh, (h + 1) * dh)
            s = _dot_nt(q_ref[0, :, cols], k_ref[0, rows, cols]) - c_ref[0, h, j]
            if masked:
                s = jnp.where(col <= row, s, -jnp.inf)
            m_old = m_scr[h]
            m_new = jnp.maximum(m_old, jnp.max(s, axis=1, keepdims=True))
            alpha = jnp.exp2(m_old - m_new)
            p = jnp.concatenate(
                [jnp.exp2(s[:, t * LANES:(t + 1) * LANES] - m_new) for t in range(n_slab)], axis=1)
            pv = _dot(p.astype(BF16), jnp.concatenate([v_ref[0, rows, cols], ones], axis=1))
            acc_scr[h] = jnp.concatenate([alpha, alpha], axis=1) * acc_scr[h] + pv
            m_scr[h] = m_new

    def body(j, carry):
        step(j, False)
        return carry

    lax.fori_loop(0, i, body, 0)
    step(i, True)
    for h in range(hb):
        acc = acc_scr[h]
        o_ref[0, :, h * dh:(h + 1) * dh] = (acc[:, :dh] / acc[:, dh:]).astype(o_ref.dtype)


def _fox_attention(qkv, cum, nb, seq, n_heads, blk):
    dh = HEAD_DIM
    hb = min(FOX_HEADS_PER_STEP, n_heads)
    assert n_heads % hb == 0
    ng = n_heads // hb
    w = hb * dh
    kern = functools.partial(_fox_kernel, blk=blk, hb=hb)
    return pl.pallas_call(
        kern,
        grid=(nb, ng, seq // blk),
        in_specs=[pl.BlockSpec((1, blk, w), lambda b, g, i: (b, i, g)),
                  pl.BlockSpec((1, seq, w), lambda b, g, i: (b, 0, ng + g)),
                  pl.BlockSpec((1, seq, w), lambda b, g, i: (b, 0, 2 * ng + g)),
                  pl.BlockSpec((1, hb, seq // blk, 1, blk), lambda b, g, i: (b, g, 0, 0, 0))],
        out_specs=pl.BlockSpec((1, blk, w), lambda b, g, i: (b, i, g)),
        out_shape=jax.ShapeDtypeStruct((nb, seq, n_heads * dh), BF16),
        scratch_shapes=[pltpu.VMEM((hb, blk, LANES), F32),
                        pltpu.VMEM((hb, blk, 2 * dh), F32)],
        compiler_params=_cparams(("arbitrary", "arbitrary", "arbitrary")),
        name="fox_attn",
    )(qkv, qkv, qkv, cum)


def _post_attn_kernel(od_ref, of_ref, x_ref, w_ref, ga_ref, scm_ref, shm_ref, g_ref, b_ref,
                      wr_ref, br_ref,
                      x1_ref, hp_ref, idx_ref, gate_ref, rank_ref, cnt_ref, carry_scr,
                      *, n_experts, top_k, dil_width, sub):
    i = pl.program_id(0)

    @pl.when(i == 0)
    def _():
        carry_scr[...] = jnp.zeros_like(carry_scr)

    for r0 in range(0, x_ref.shape[0], sub):
        _post_attn_rows(slice(r0, r0 + sub), od_ref, of_ref, x_ref, w_ref, ga_ref, scm_ref, shm_ref,
                        g_ref, b_ref, wr_ref, br_ref, x1_ref, hp_ref, idx_ref, gate_ref, rank_ref,
                        carry_scr, n_experts=n_experts, top_k=top_k, dil_width=dil_width)
    cnt_ref[...] = jnp.broadcast_to(carry_scr[...], cnt_ref.shape)


def _post_attn_rows(rows, od_ref, of_ref, x_ref, w_ref, ga_ref, scm_ref, shm_ref, g_ref, b_ref,
                    wr_ref, br_ref, x1_ref, hp_ref, idx_ref, gate_ref, rank_ref, carry_scr,
                    *, n_experts, top_k, dil_width):
    attn = (_dot(od_ref[rows, :], w_ref[:dil_width, :]) + _dot(of_ref[rows, :], w_ref[dil_width:, :]))
    x1 = _layer_norm(DEEPNORM_ALPHA * x_ref[rows, :] + ga_ref[0] * attn, g_ref[...], b_ref[...])
    x1_ref[rows, :] = x1
    h = x1 * (1.0 + scm_ref[0]) + shm_ref[0]
    h_hi, h_lo = _split_bf16(h)
    hp_ref[rows, :] = h

    logits = _dot_split(h_hi, h_lo, wr_ref[...]) + br_ref[...]
    tm = logits.shape[0]
    lane = lax.broadcasted_iota(I32, (tm, LANES), 1)
    work = jnp.where(lane < n_experts, logits, -jnp.inf)
    vals, idxs = [], []
    for _ in range(top_k):
        mx = jnp.max(work, axis=1, keepdims=True)
        ix = jnp.min(jnp.where(work == mx, lane, LANES), axis=1, keepdims=True)
        vals.append(mx)
        idxs.append(ix)
        work = jnp.where(lane == ix, -jnp.inf, work)
    exps = [jnp.exp(v - vals[0]) for v in vals]
    denom = exps[0]
    for e in exps[1:]:
        denom = denom + e

    onehot = jnp.zeros((tm, LANES), F32)
    for ix in idxs:
        onehot = onehot + (lane == ix).astype(F32)
    r_i = lax.broadcasted_iota(I32, (tm, tm), 0)
    c_i = lax.broadcasted_iota(I32, (tm, tm), 1)
    tri = (c_i < r_i).astype(BF16)
    before = _dot(tri, onehot.astype(BF16)) + carry_scr[...]

    idx_out = jnp.zeros((tm, LANES), I32)
    gate_out = jnp.zeros((tm, LANES), F32)
    rank_out = jnp.zeros((tm, LANES), I32)
    for k in range(top_k):
        rank_k = jnp.sum(jnp.where(lane == idxs[k], before, 0.0), axis=1, keepdims=True)
        idx_out = jnp.where(lane == k, idxs[k], idx_out)
        gate_out = jnp.where(lane == k, exps[k] / denom, gate_out)
        rank_out = jnp.where(lane == k, rank_k.astype(I32), rank_out)
    gate_ref[rows, :] = gate_out
    idx_ref[:, rows] = idx_out.T[:idx_ref.shape[0], :]
    rank_ref[:, rows] = rank_out.T[:rank_ref.shape[0], :]
    carry_scr[...] = carry_scr[...] + jnp.sum(onehot, axis=0, keepdims=True)


def _post_attn(o_dil, o_fox, x2, w_out_bf, g_a, sc_m, sh_m, ln_g, ln_b, w_router, b_router, seq):
    t, d_model = x2.shape
    dil_width = o_dil.shape[1]
    n_experts = w_router.shape[1]
    tm, sub = 256, 256
    wr = _split_weight(jnp.pad(w_router, ((0, 0), (0, LANES - n_experts))))
    br = jnp.pad(b_router, (0, LANES - n_experts)).reshape(1, LANES)
    kern = functools.partial(_post_attn_kernel, n_experts=n_experts, top_k=TOP_K, dil_width=dil_width,
                             sub=sub)
    row = lambda w: pl.BlockSpec((tm, w), lambda i: (i, 0))
    col = pl.BlockSpec((8, tm), lambda i: (0, i))
    const = lambda r, w: pl.BlockSpec((r, w), lambda i: (0, 0))
    bvec = pl.BlockSpec((1, 1, d_model), lambda i: (i * tm // seq, 0, 0))
    return pl.pallas_call(
        kern,
        grid=(t // tm,),
        in_specs=[row(dil_width), row(o_fox.shape[1]), row(d_model), const(d_model, d_model),
                  bvec, bvec, bvec, const(1, d_model), const(1, d_model),
                  const(d_model, 2 * LANES), const(1, LANES)],
        out_specs=[row(d_model), row(d_model), col, row(LANES), col,
                   pl.BlockSpec((8, LANES), lambda i: (0, 0))],
        out_shape=[jax.ShapeDtypeStruct((t, d_model), F32),
                   jax.ShapeDtypeStruct((t, d_model), F32),
                   jax.ShapeDtypeStruct((8, t), I32),
                   jax.ShapeDtypeStruct((t, LANES), F32),
                   jax.ShapeDtypeStruct((8, t), I32),
                   jax.ShapeDtypeStruct((8, LANES), F32)],
        scratch_shapes=[pltpu.VMEM((1, LANES), F32)],
        compiler_params=_cparams(("arbitrary",)),
        name="post_attn",
    )(o_dil, o_fox, x2, w_out_bf, g_a, sc_m, sh_m, ln_g.reshape(1, -1), ln_b.reshape(1, -1),
      wr, br)


def _scatter_kernel(dest_ref, fill_lo_ref, fill_hi_ref, used_ref, hp_ref, xs_ref, zbuf, sem, zsem,
                    *, top_k):
    i = pl.program_id(0)
    tm = hp_ref.shape[0]
    rb = ROW_BLOCK

    @pl.when(i == 0)
    def _():
        zbuf[...] = jnp.zeros(zbuf.shape, zbuf.dtype)

        def tails(wait):
            def row(r, c):
                copy = pltpu.make_async_copy(zbuf.at[pl.ds(0, 1)], xs_ref.at[pl.ds(r, 1)], zsem.at[0])
                copy.wait() if wait else copy.start()
                return c

            def expert(e, c):
                return lax.fori_loop(fill_lo_ref[e], fill_hi_ref[e], row, c)

            lax.fori_loop(0, fill_lo_ref.shape[0], expert, 0)

        def rest(wait):
            def blk(b, c):
                copy = pltpu.make_async_copy(zbuf, xs_ref.at[pl.ds(pl.multiple_of(b * rb, rb), rb)],
                                             zsem.at[1])
                copy.wait() if wait else copy.start()
                return c

            lax.fori_loop(used_ref[0], xs_ref.shape[0] // rb, blk, 0)

        tails(False)
        rest(False)
        tails(True)
        rest(True)

    n_tokens = dest_ref.shape[0] // top_k
    base = i * tm
    for t in range(tm):
        for k in range(top_k):
            d = dest_ref[base + (k * n_tokens + t)]
            pltpu.make_async_copy(hp_ref.at[pl.ds(t, 1)], xs_ref.at[pl.ds(d, 1)], sem).start(
                priority=(t * top_k + k) % 2)

    for k in range(top_k):
        pltpu.make_async_copy(hp_ref, xs_ref.at[pl.ds(0, tm)], sem).wait()


def _scatter_rows(dest_flat, fill_lo, fill_hi, used_blocks, hp, n_rows):
    t, w = hp.shape
    tm = 256
    kern = functools.partial(_scatter_kernel, top_k=TOP_K)
    return pl.pallas_call(
        kern,
        grid_spec=pltpu.PrefetchScalarGridSpec(
            num_scalar_prefetch=4,
            grid=(t // tm,),
            in_specs=[pl.BlockSpec((tm, w), lambda i, *_: (i, 0))],
            out_specs=pl.BlockSpec(memory_space=pl.ANY),
            scratch_shapes=[pltpu.VMEM((ROW_BLOCK, w), F32),
                            pltpu.SemaphoreType.DMA(()),
                            pltpu.SemaphoreType.DMA((2,))]),
        out_shape=jax.ShapeDtypeStruct((n_rows, w), F32),
        compiler_params=_cparams(("arbitrary",)),
        name="scatter_rows",
    )(dest_flat, fill_lo, fill_hi, used_blocks, hp)


def _moe_kernel(iexp_ref, irow_ref, inu_ref, used_ref, xs_ref, wg_ref, wu_ref, wd_ref, bg_ref, bu_ref,
                bd_ref, y_ref, xbuf, yacc, sem_in, sem_out, *, n_chunks):
    del iexp_ref
    it = pl.program_id(0)
    f = pl.program_id(1)
    n_items = pl.num_programs(0)
    nu = inu_ref[it]
    rb = ROW_BLOCK
    rows_of = lambda k, n=1: pl.ds(pl.multiple_of(k * rb, rb), n * rb)

    nxt = jnp.minimum(it + 1, n_items - 1)
    n_next = jnp.where(it + 1 < n_items, inu_ref[nxt], 0)
    n_prev = jnp.where(it > 0, inu_ref[jnp.maximum(it - 1, 0)], 0)

    def x_copy(item, k, src_unit=None):
        src_unit = k if src_unit is None else src_unit
        src = pl.ds(pl.multiple_of(irow_ref[item] + src_unit * rb, rb), rb)
        return pltpu.make_async_copy(xs_ref.at[src], xbuf.at[rows_of(k)], sem_in)

    def y_copy(k):
        dst = pl.ds(pl.multiple_of(irow_ref[it] + k * rb, rb), rb)
        return pltpu.make_async_copy(yacc.at[rows_of(k)], y_ref.at[dst], sem_out)

    def for_units(n, fn):
        def body(k, c):
            fn(k)
            return c
        lax.fori_loop(0, n, body, 0)

    def init_unit(k):
        yacc[rows_of(k), :] = jnp.broadcast_to(bd_ref[0], (rb, yacc.shape[1]))

    first = f == 0
    last = f == n_chunks - 1

    n_arriving = jnp.where(it == 0, nu, jnp.where(n_prev > 0, jnp.maximum(n_prev, nu), 0))

    def wait_rows():
        @pl.when(first)
        def _():
            for_units(n_arriving, lambda k: x_copy(it, 0).wait())

    @pl.when(nu == 0)
    def _():
        wait_rows()

    @pl.when(nu > 0)
    def _():
        @pl.when(first & (it == 0))
        def _():
            for_units(nu, lambda k: x_copy(0, k).start())

        @pl.when(first)
        def _():
            for_units(nu, init_unit)

        wg = wg_ref[0].astype(BF16)
        wu = wu_ref[0].astype(BF16)
        wd = wd_ref[0].astype(BF16)
        bg = bg_ref[0]
        bu = bu_ref[0]

        wait_rows()

        def block(k, n, stream):
            xb = xbuf[rows_of(k, n), :].astype(BF16)
            g = _dot(xb, wg) + bg
            u = _dot(xb, wu) + bu
            g = jnp.minimum(g, SWIGLU_LIMIT)
            u = jnp.clip(u, -SWIGLU_LIMIT, SWIGLU_LIMIT)
            act = (u + 1.0) * (g * jax.nn.sigmoid(SWIGLU_ALPHA * g))
            yacc[rows_of(k, n), :] += _dot(act.astype(BF16), wd)
            if stream:
                for j in range(n):
                    y_copy(k + j).start()
                    x_copy(nxt, k + j, jnp.minimum(k + j, jnp.maximum(n_next, 1) - 1)).start()

        def run_blocks(stream):
            def quad(t, c):
                block(4 * t, 2, stream)
                block(4 * t + 2, 2, stream)
                return c

            n_quads = lax.shift_right_logical(nu, 2)
            lax.fori_loop(0, n_quads, quad, 0)
            done = n_quads * 4

            @pl.when((nu & 2) != 0)
            def _():
                block(done, 2, stream)

            @pl.when((nu & 1) != 0)
            def _():
                block(done + (nu & 2), 1, stream)

        @pl.when(jnp.logical_not(last))
        def _():
            run_blocks(False)

        @pl.when(last)
        def _():
            run_blocks(True)

            def rest(k, c):
                x_copy(nxt, k).start()
                return c
            lax.fori_loop(nu, n_next, rest, 0)
            for_units(nu, lambda k: y_copy(k).wait())

            @pl.when(it == n_items - 1)
            def _():
                for_units(nu, lambda k: x_copy(it, 0).wait())

    @pl.when((it == n_items - 1) & last)
    def _():
        lo = used_ref[0]
        hi = y_ref.shape[0] // rb
        dst_of = lambda b: pl.ds(pl.multiple_of(b * rb, rb), rb)
        yacc[pl.ds(0, rb), :] = jnp.zeros((rb, yacc.shape[1]), yacc.dtype)

        def start(b, c):
            pltpu.make_async_copy(yacc.at[pl.ds(0, rb)], y_ref.at[dst_of(b)], sem_out).start()
            return c

        def wait(b, c):
            pltpu.make_async_copy(yacc.at[pl.ds(0, rb)], y_ref.at[dst_of(b)], sem_out).wait()
            return c

        lax.fori_loop(lo, hi, start, 0)
        lax.fori_loop(lo, hi, wait, 0)


def _moe_experts(item_exp, item_row, item_nblk, used_blocks, xs, w_gate_up, b_gate_up, w_down, b_down):
    n_rows = xs.shape[0]
    n_exp, d_model, two_f = w_gate_up.shape
    d_ff = two_f // 2
    fc = FF_CHUNK
    n_chunks = d_ff // fc
    n_items = item_exp.shape[0]
    kern = functools.partial(_moe_kernel, n_chunks=n_chunks)

    def fsel(it, f, nbk):
        return jnp.where(nbk[it] > 0, f, n_chunks - 1)

    return pl.pallas_call(
        kern,
        grid_spec=pltpu.PrefetchScalarGridSpec(
            num_scalar_prefetch=4,
            grid=(n_items, n_chunks),
            in_specs=[
                pl.BlockSpec(memory_space=pl.ANY),
                pl.BlockSpec((1, d_model, fc), lambda it, f, ex, ro, nbk, us: (ex[it], 0, fsel(it, f, nbk))),
                pl.BlockSpec((1, d_model, fc),
                             lambda it, f, ex, ro, nbk, us: (ex[it], 0, n_chunks + fsel(it, f, nbk))),
                pl.BlockSpec((1, fc, d_model), lambda it, f, ex, ro, nbk, us: (ex[it], fsel(it, f, nbk), 0)),
                pl.BlockSpec((1, 1, fc), lambda it, f, ex, ro, nbk, us: (ex[it], 0, fsel(it, f, nbk))),
                pl.BlockSpec((1, 1, fc),
                             lambda it, f, ex, ro, nbk, us: (ex[it], 0, n_chunks + fsel(it, f, nbk))),
                pl.BlockSpec((1, 1, d_model), lambda it, f, ex, ro, nbk, us: (ex[it], 0, 0)),
            ],
            out_specs=pl.BlockSpec(memory_space=pl.ANY),
            scratch_shapes=[pltpu.VMEM((ITEM_ROWS, d_model), F32),
                            pltpu.VMEM((ITEM_ROWS, d_model), F32),
                            pltpu.SemaphoreType.DMA(()),
                            pltpu.SemaphoreType.DMA(())]),
        out_shape=jax.ShapeDtypeStruct((n_rows, d_model), F32),
        compiler_params=_cparams(("arbitrary", "arbitrary")),
        name="moe_experts",
    )(item_exp, item_row, item_nblk, used_blocks, xs, w_gate_up, w_gate_up, w_down,
      b_gate_up.reshape(n_exp, 1, two_f), b_gate_up.reshape(n_exp, 1, two_f),
      b_down.reshape(n_exp, 1, d_model))


def _combine_kernel(dest_ref, y_ref, x1_ref, gate_ref, gm_ref, g_ref, b_ref, o_ref, ybuf, sem, *, top_k):
    i = pl.program_id(0)
    n_steps = pl.num_programs(0)
    tm = x1_ref.shape[0]
    slot = i % 2

    def gather(step, slot_):
        n_tokens = dest_ref.shape[0] // top_k
        base = step * tm
        for t in range(tm):
            for k in range(top_k):
                d = dest_ref[base + (k * n_tokens + t)]
                pltpu.make_async_copy(y_ref.at[pl.ds(d, 1)], ybuf.at[slot_, pl.ds(k * tm + t, 1)],
                                      sem.at[slot_]).start(priority=(t * top_k + k) % 2)

    @pl.when(i == 0)
    def _():
        gather(0, 0)

    for nxt in range(2):
        @pl.when((i + 1 < n_steps) & (slot == 1 - nxt))
        def _(nxt=nxt):
            gather(i + 1, nxt)

    pltpu.make_async_copy(y_ref.at[pl.ds(0, top_k * tm)], ybuf.at[slot], sem.at[slot]).wait()
    gates = gate_ref[...]
    ffn = gates[:, 0:1] * ybuf[slot, pl.ds(0, tm), :]
    for k in range(1, top_k):
        ffn = ffn + gates[:, k:k + 1] * ybuf[slot, pl.ds(k * tm, tm), :]
    o_ref[...] = _layer_norm(DEEPNORM_ALPHA * x1_ref[...] + gm_ref[0] * ffn, g_ref[...], b_ref[...])


def _combine(dest_flat, y, x1, gates, g_m, ln_g, ln_b, seq):
    t, d_model = x1.shape
    tm = 128
    kern = functools.partial(_combine_kernel, top_k=TOP_K)
    return pl.pallas_call(
        kern,
        grid_spec=pltpu.PrefetchScalarGridSpec(
            num_scalar_prefetch=1,
            grid=(t // tm,),
            in_specs=[pl.BlockSpec(memory_space=pl.ANY),
                      pl.BlockSpec((tm, d_model), lambda i, d: (i, 0)),
                      pl.BlockSpec((tm, LANES), lambda i, d: (i, 0)),
                      pl.BlockSpec((1, 1, d_model), lambda i, d: (i * tm // seq, 0, 0)),
                      pl.BlockSpec((1, d_model), lambda i, d: (0, 0)),
                      pl.BlockSpec((1, d_model), lambda i, d: (0, 0))],
            out_specs=pl.BlockSpec((tm, d_model), lambda i, d: (i, 0)),
            scratch_shapes=[pltpu.VMEM((2, TOP_K * tm, d_model), F32),
                            pltpu.SemaphoreType.DMA((2,))]),
        out_shape=jax.ShapeDtypeStruct((t, d_model), F32),
        compiler_params=_cparams(("arbitrary",)),
        name="combine_ln2",
    )(dest_flat, y, x1, gates, g_m, ln_g.reshape(1, -1), ln_b.reshape(1, -1))


def _routing_plan(counts, idx, rank, n_tokens):
    n_exp = counts.shape[0]

    def take(table, index):
        hit = index[..., None] == jnp.arange(n_exp, dtype=I32)
        return jnp.sum(jnp.where(hit, table, 0), axis=-1)

    padded = (counts + ROW_BLOCK - 1) // ROW_BLOCK * ROW_BLOCK
    pad_end = jnp.cumsum(padded)
    pad_start = pad_end - padded
    dest = (take(pad_start, idx) + rank).reshape(-1).astype(I32)

    max_rows = (n_tokens * TOP_K + n_exp * (ROW_BLOCK - 1) + ROW_BLOCK - 1) // ROW_BLOCK * ROW_BLOCK
    max_items = n_exp + max_rows // ITEM_ROWS
    items_per = (padded + ITEM_ROWS - 1) // ITEM_ROWS
    items_end = jnp.cumsum(items_per)
    total = items_end[-1]
    it = jnp.arange(max_items, dtype=I32)
    last = jnp.maximum(total - 1, 0)
    it_c = jnp.minimum(it, last)
    exp_of = jnp.sum((items_end[None, :] <= it_c[:, None]).astype(I32), axis=1)
    exp_of = jnp.minimum(exp_of, n_exp - 1)
    local = it_c - take(items_end - items_per, exp_of)
    row0 = take(pad_start, exp_of) + local * ITEM_ROWS
    rows = jnp.minimum(take(padded, exp_of) - local * ITEM_ROWS, ITEM_ROWS)
    nblk = jnp.where(it < total, rows // ROW_BLOCK, 0)
    used_blocks = (pad_end[-1:] // ROW_BLOCK).astype(I32)
    fill = ((pad_start + counts).astype(I32), pad_end.astype(I32))
    return dest, fill, (exp_of.astype(I32), row0.astype(I32), nblk.astype(I32), used_blocks), max_rows


def kernel(x, c, w_ada, b_ada, w_in, b_forget, w_out, ln1_g, ln1_b, w_router, b_router,
           w_gate_up, b_gate_up, w_down, b_down, ln2_g, ln2_b):
    nb, seq, d_model = x.shape
    n_heads = d_model // HEAD_DIM
    n_dil = n_heads // 2
    n_fox = n_heads - n_dil
    dil_width, fox_width = n_dil * HEAD_DIM, n_fox * HEAD_DIM
    t = nb * seq
    layer = 0

    mod = _ada_mod(c, w_ada[layer], b_ada[layer])
    sh_a, sc_a, g_a, sh_m, sc_m, g_m = [m.reshape(nb, 1, d_model) for m in jnp.split(mod, 6, axis=-1)]

    x2 = x.reshape(t, d_model)
    qkv_dil, qkv_fox, f_logit = _in_proj(x2, sc_a, sh_a, w_in, layer, seq, dil_width, fox_width, n_fox)

    slopes = 2.0 ** (-8.0 * jnp.arange(1, n_dil + 1, dtype=F32) / n_dil)
    o_dil = _dilated_attention(qkv_dil.reshape(nb, seq, 3 * dil_width), slopes, nb, seq, n_dil)

    fox_blk = 512
    cum = _forget_cumsum(f_logit, b_forget[layer], nb, seq).reshape(
        nb, n_fox, seq // fox_blk, 1, fox_blk)
    o_fox = _fox_attention(qkv_fox.reshape(nb, seq, 3 * fox_width), cum, nb, seq, n_fox, fox_blk)

    x1, hp, idx, gates, rank, cnt = _post_attn(
        o_dil.reshape(t, dil_width), o_fox.reshape(t, fox_width), x2, w_out[layer].astype(BF16),
        g_a, sc_m, sh_m, ln1_g[layer], ln1_b[layer], w_router[layer], b_router[layer], seq)

    counts = cnt[0, :N_EXPERTS].astype(I32)
    dest, fill, items, max_rows = _routing_plan(counts, idx[:TOP_K], rank[:TOP_K], t)
    xs = _scatter_rows(dest, *fill, items[3], hp, max_rows)
    y = _moe_experts(*items, xs, w_gate_up[layer], b_gate_up[layer], w_down[layer], b_down[layer])
    out = _combine(dest, y, x1, gates, g_m, ln2_g[layer], ln2_b[layer], seq)
    return out.reshape(nb, seq, d_model)
```

```python
import functools
import math

import jax
import jax.numpy as jnp
from jax import lax
from jax.experimental import pallas as pl
from jax.experimental.pallas import tpu as pltpu

F32 = jnp.float32
BF16 = jnp.bfloat16
I32 = jnp.int32
U32 = jnp.uint32

HEAD_DIM = 128
Q_BLOCK = 128
DIL_PATTERNS = ((128, 1), (512, 4), (2048, 16))
N_EXPERTS = 32
TOP_K = 4
SWIGLU_LIMIT = 7.0
SWIGLU_ALPHA = 1.702
LN_EPS = 1e-5
DEPTH = 1
DEEPNORM_ALPHA = (2.0 * DEPTH) ** 0.25

LOG2E = math.log2(math.e)
DIL_GROUP = 16
DEINT = 4
FOX_HEADS_PER_STEP = 4

LANES = 128
VMEM_LIMIT = 56 * 1024 * 1024

ROW_BLOCK = 128
ITEM_ROWS = 1280
FF_CHUNK = 512


def _cparams(sem):
    return pltpu.CompilerParams(dimension_semantics=sem, vmem_limit_bytes=VMEM_LIMIT)


def _dot(a, b):
    return jnp.dot(a, b, preferred_element_type=F32)


def _dot_nt(a, b):
    return lax.dot_general(a, b, (((1,), (1,)), ((), ())), preferred_element_type=F32)


def _split_bf16(x):
    hi = x.astype(BF16)
    lo = (x - hi.astype(F32)).astype(BF16)
    return hi, lo


def _split_weight(w):
    return jnp.concatenate(_split_bf16(w), axis=1)


def _dot_split(x_hi, x_lo, w_cat):
    r = _dot(x_hi, w_cat) + _dot(x_lo, w_cat)
    return r[:, :LANES] + r[:, LANES:]


def _layer_norm(y, g, b):
    mu = jnp.mean(y, axis=-1, keepdims=True)
    yc = y - mu
    var = jnp.mean(yc * yc, axis=-1, keepdims=True)
    return yc * lax.rsqrt(var + LN_EPS) * g + b


def _ada_kernel(ct_ref, w_ref, b_ref, o_ref):
    ct = ct_ref[...]
    s = ct * jax.nn.sigmoid(ct)
    d_model, nb = ct.shape
    tn = w_ref.shape[1]
    sb = [jnp.broadcast_to(s[:, b:b + 1], (d_model, LANES)) for b in range(nb)]
    for j in range(tn // LANES):
        cols = slice(j * LANES, (j + 1) * LANES)
        w = w_ref[:, cols]
        for b in range(nb):
            acc = jnp.sum(w * sb[b], axis=0, keepdims=True)
            o_ref[b:b + 1, cols] = acc + b_ref[:, cols]


def _ada_mod(c, w, b):
    nb, d_model = c.shape
    n = w.shape[1]
    tn = 1536 if n % 1536 == 0 else LANES
    return pl.pallas_call(
        _ada_kernel,
        grid=(n // tn,),
        in_specs=[pl.BlockSpec((d_model, nb), lambda j: (0, 0)),
                  pl.BlockSpec((d_model, tn), lambda j: (0, j)),
                  pl.BlockSpec((1, tn), lambda j: (0, j))],
        out_specs=pl.BlockSpec((nb, tn), lambda j: (0, j)),
        out_shape=jax.ShapeDtypeStruct((nb, n), F32),
        compiler_params=_cparams(("arbitrary",)),
        name="ada_mod",
    )(c.T, w, b.reshape(1, n))


def _inproj_kernel(x_ref, sc_ref, sh_ref, w_ref, wf_ref,
                   odil_ref, ofox_ref, of_ref, h_scr, *, n_dil_tiles, q_tiles, scale):
    j = pl.program_id(1)

    @pl.when(j == 0)
    def _():
        h = x_ref[...] * (1.0 + sc_ref[0]) + sh_ref[0]
        h_hi, h_lo = _split_bf16(h)
        h_scr[...] = h_hi
        of_ref[...] = _dot_split(h_hi, h_lo, wf_ref[...])

    acc = _dot(h_scr[...], w_ref[...])
    is_q =(j < q_tiles) | ((j >= n_dil_tiles) & (j < n_dil_tiles + q_tiles))
    acc = acc * jnp.where(is_q, scale, 1.0).astype(F32)

    @pl.when(j < n_dil_tiles)
    def _():
        odil_ref[...] = acc

    @pl.when(j >= n_dil_tiles)
    def _():
        ofox_ref[...] = acc.astype(BF16)


def _in_proj(x2, sc, sh, w_in, layer, seq, dil_width, fox_width, n_fox_heads):
    t, d_model = x2.shape
    tm, tn = min(1024, seq), min(1024, dil_width)
    n_dil_tiles = 3 * dil_width // tn
    n_fox_tiles = 3 * fox_width // tn
    q_tiles = dil_width // tn
    assert dil_width == fox_width and dil_width % tn == 0 and seq % tm == 0
    main = 3 * dil_width + 3 * fox_width
    wf = _split_weight(jnp.pad(w_in[layer, :, main:], ((0, 0), (0, LANES - n_fox_heads))))
    kern = functools.partial(_inproj_kernel, n_dil_tiles=n_dil_tiles, q_tiles=q_tiles,
                             scale=LOG2E / math.sqrt(HEAD_DIM))
    bidx = lambda i, j: (i * tm // seq, 0, 0)
    return pl.pallas_call(
        kern,
        grid=(t // tm, n_dil_tiles + n_fox_tiles),
        in_specs=[pl.BlockSpec((tm, d_model), lambda i, j: (i, 0)),
                  pl.BlockSpec((1, 1, d_model), bidx),
                  pl.BlockSpec((1, 1, d_model), bidx),
                  pl.BlockSpec((None, d_model, tn), lambda i, j: (layer, 0, j)),
                  pl.BlockSpec((d_model, 2 * LANES), lambda i, j: (0, 0))],
        out_specs=[pl.BlockSpec((tm, tn), lambda i, j: (i, jnp.minimum(j, n_dil_tiles - 1))),
                   pl.BlockSpec((tm, tn), lambda i, j: (i, jnp.maximum(j - n_dil_tiles, 0))),
                   pl.BlockSpec((tm, LANES), lambda i, j: (i, 0))],
        out_shape=[jax.ShapeDtypeStruct((t, 3 * dil_width), F32),
                   jax.ShapeDtypeStruct((t, 3 * fox_width), BF16),
                   jax.ShapeDtypeStruct((t, LANES), F32)],
        scratch_shapes=[pltpu.VMEM((tm, d_model), BF16)],
        compiler_params=_cparams(("arbitrary", "arbitrary")),
        name="in_proj",
    )(x2, sc, sh, w_in.astype(BF16), wf)


def _cum_kernel(f_ref, b_ref, o_ref):
    hf = o_ref.shape[1]
    z = f_ref[...].T[:hf, :] + b_ref[...]
    c = jnp.minimum(z, 0.0) - jnp.log1p(jnp.exp(-jnp.abs(z)))
    seq = z.shape[1]
    lane = lax.broadcasted_iota(I32, z.shape, 1)
    shift = 1
    while shift < seq:
        c = c + jnp.where(lane >= shift, pltpu.roll(c, shift, axis=1), 0.0)
        shift *= 2
    o_ref[0] = c * LOG2E


def _forget_cumsum(f_logit, b_forget, nb, seq):
    hf = b_forget.shape[0]
    return pl.pallas_call(
        _cum_kernel,
        grid=(nb,),
        in_specs=[pl.BlockSpec((seq, LANES), lambda b: (b, 0)),
                  pl.BlockSpec((hf, 1), lambda b: (0, 0))],
        out_specs=pl.BlockSpec((1, hf, seq), lambda b: (b, 0, 0)),
        out_shape=jax.ShapeDtypeStruct((nb, hf, seq), F32),
        compiler_params=_cparams(("arbitrary",)),
        name="forget_cumsum",
    )(f_logit, b_forget.reshape(hf, 1))


def _dil_kernel(slope_ref, q_ref, k_ref, v_ref, o_ref, q4_ref, k4_ref, v4_ref, acc_ref, l_ref, m_ref,
                bias_ref, *, patterns, seq):
    slope = slope_ref[pl.program_id(1)]
    qb = Q_BLOCK
    row = lax.broadcasted_iota(I32, (qb, 2 * qb), 0)
    col = lax.broadcasted_iota(I32, (qb, 2 * qb), 1)
    delta = row + qb - col
    is_prev = col < qb
    neg_inf = F32(-jnp.inf)
    ones = jnp.ones((2 * qb, HEAD_DIM), BF16)

    sub = seq // DEINT
    chunk = min(256, sub)
    for r4 in range(DEINT):
        for c0 in range(0, sub, chunk):
            src = pl.ds(r4 + DEINT * c0, chunk, stride=DEINT)
            dst = pl.ds(r4 * sub + c0, chunk)
            q4_ref[dst, :] = q_ref[0, src, :]
            k4_ref[dst, :] = k_ref[0, src, :]
            v4_ref[dst, :] = v_ref[0, src, :]

    for pi, (window, dil) in enumerate(sorted(patterns, key=lambda wd: -wd[1])):
        reach = window // dil
        nblk = seq // dil // qb
        n_blocks = dil * nblk
        assert reach <= qb and nblk * qb * dil == seq and n_blocks % DIL_GROUP == 0
        sd = slope * (float(dil) * LOG2E)
        bias_ref[...] = jnp.where((delta >= 0) & (delta <= reach), -sd * delta.astype(F32), neg_inf)

        def block(idx, dil=dil, nblk=nblk, pi=pi):
            r = idx // nblk
            n = idx - r * nblk
            start = r + n * (qb * dil)
            cur = pl.ds(start, qb, stride=dil) if dil > 1 else pl.ds(pl.multiple_of(start, qb), qb)
            if dil % DEINT == 0:
                step = dil // DEINT
                first = (r % DEINT) * sub + r // DEINT
                lstart = first + n * (qb * step)
                lprev = jnp.maximum(lstart - qb * step, first)
                if step > 1:
                    src_cur, src_prev = pl.ds(lstart, qb, stride=step), pl.ds(lprev, qb, stride=step)
                else:
                    src_cur = pl.ds(pl.multiple_of(lstart, qb), qb)
                    src_prev = pl.ds(pl.multiple_of(lprev, qb), qb)
                q = q4_ref[src_cur, :].astype(BF16)
                kk = jnp.concatenate([k4_ref[src_prev, :], k4_ref[src_cur, :]], axis=0).astype(BF16)
                vv = jnp.concatenate([v4_ref[src_prev, :], v4_ref[src_cur, :]], axis=0).astype(BF16)
            else:
                pstart = jnp.maximum(start - qb * dil, r)
                prev = (pl.ds(pstart, qb, stride=dil) if dil > 1
                        else pl.ds(pl.multiple_of(pstart, qb), qb))
                q = q_ref[0, cur, :].astype(BF16)
                kk = jnp.concatenate([k_ref[0, prev, :], k_ref[0, cur, :]], axis=0).astype(BF16)
                vv = jnp.concatenate([v_ref[0, prev, :], v_ref[0, cur, :]], axis=0).astype(BF16)
            s = _dot_nt(q, kk) + bias_ref[...]
            s = jnp.where(is_prev & (n == 0), neg_inf, s)
            m_blk = jnp.max(s, axis=1, keepdims=True)
            p = jnp.exp2(s - m_blk)
            pv = _dot(p.astype(BF16), jnp.concatenate([vv, ones], axis=1))
            o_blk, l_blk = pv[:, :HEAD_DIM], pv[:, HEAD_DIM:]
            if pi == 0:
                m_ref[cur, :] = jnp.broadcast_to(m_blk, (qb, LANES))
                acc_ref[cur, :] = o_blk
                l_ref[cur, :] = l_blk
            else:
                m_old = m_ref[cur, :]
                m_new = jnp.maximum(m_old, m_blk)
                a_old = jnp.exp2(m_old - m_new)
                a_blk = jnp.exp2(m_blk - m_new)
                m_ref[cur, :] = m_new
                acc_ref[cur, :] = acc_ref[cur, :] * a_old + o_blk * a_blk
                l_ref[cur, :] = l_ref[cur, :] * a_old + l_blk * a_blk

        def body(g, carry, block=block):
            for u in range(DIL_GROUP):
                block(g * DIL_GROUP + u)
            return carry

        lax.fori_loop(0, n_blocks // DIL_GROUP, body, 0)

    chunk = 512
    for c0 in range(0, seq, chunk):
        rows = slice(c0, c0 + chunk)
        o_ref[0, rows, :] = (acc_ref[rows, :] / l_ref[rows, :]).astype(o_ref.dtype)


def _dilated_attention(qkv, slopes, nb, seq, n_heads):
    dh = HEAD_DIM
    kern = functools.partial(_dil_kernel, patterns=DIL_PATTERNS, seq=seq)
    blk = lambda off: pl.BlockSpec((1, seq, dh), lambda b, h, s: (b, 0, off + h))
    return pl.pallas_call(
        kern,
        grid_spec=pltpu.PrefetchScalarGridSpec(
            num_scalar_prefetch=1,
            grid=(nb, n_heads),
            in_specs=[blk(0), blk(n_heads), blk(2 * n_heads)],
            out_specs=pl.BlockSpec((1, seq, dh), lambda b, h, s: (b, 0, h)),
            scratch_shapes=[pltpu.VMEM((seq, dh), F32),
                            pltpu.VMEM((seq, dh), F32),
                            pltpu.VMEM((seq, dh), F32),
                            pltpu.VMEM((seq, dh), F32),
                            pltpu.VMEM((seq, LANES), F32),
                            pltpu.VMEM((seq, LANES), F32),
                            pltpu.VMEM((Q_BLOCK, 2 * Q_BLOCK), F32)]),
        out_shape=jax.ShapeDtypeStruct((nb, seq, n_heads * dh), BF16),
        compiler_params=_cparams(("arbitrary", "arbitrary")),
        name="dilated_attn",
    )(slopes, qkv, qkv, qkv)


def _fox_kernel(q_ref, k_ref, v_ref, c_ref, o_ref, m_scr, acc_scr, *, blk, hb):
    i = pl.program_id(2)
    dh = HEAD_DIM
    row = lax.broadcasted_iota(I32, (blk, blk), 0)
    col = lax.broadcasted_iota(I32, (blk, blk), 1)
    m_scr[...] = jnp.full(m_scr.shape, -jnp.inf, F32)
    acc_scr[...] = jnp.zeros(acc_scr.shape, F32)
    ones = jnp.ones((blk, dh), BF16)
    n_slab = blk // LANES

    def step(j, masked):
        rows = pl.ds(pl.multiple_of(j * blk, blk), blk)
        for h in range(hb):
            cols = slice(h * dh, (h + 1) * dh)
            s = _dot_nt(q_ref[0, :, cols], k_ref[0, rows, cols]) - c_ref[0, h, j]
            if masked:
                s = jnp.where(col <= row, s, -jnp.inf)
            m_old = m_scr[h]
            m_new = jnp.maximum(m_old, jnp.max(s, axis=1, keepdims=True))
            alpha = jnp.exp2(m_old - m_new)
            p = jnp.concatenate(
                [jnp.exp2(s[:, t * LANES:(t + 1) * LANES] - m_new) for t in range(n_slab)], axis=1)
            pv = _dot(p.astype(BF16), jnp.concatenate([v_ref[0, rows, cols], ones], axis=1))
            acc_scr[h] = jnp.concatenate([alpha, alpha], axis=1) * acc_scr[h] + pv
            m_scr[h] = m_new

    def body(j, carry):
        step(j, False)
        return carry

    lax.fori_loop(0, i, body, 0)
    step(i, True)
    for h in range(hb):
        acc = acc_scr[h]
        o_ref[0, :, h * dh:(h + 1) * dh] = (acc[:, :dh] / acc[:, dh:]).astype(o_ref.dtype)


def _fox_attention(qkv, cum, nb, seq, n_heads, blk):
    dh = HEAD_DIM
    hb = min(FOX_HEADS_PER_STEP, n_heads)
    assert n_heads % hb == 0
    ng = n_heads // hb
    w = hb * dh
    kern = functools.partial(_fox_kernel, blk=blk, hb=hb)
    return pl.pallas_call(
        kern,
        grid=(nb, ng, seq // blk),
        in_specs=[pl.BlockSpec((1, blk, w), lambda b, g, i: (b, i, g)),
                  pl.BlockSpec((1, seq, w), lambda b, g, i: (b, 0, ng + g)),
                  pl.BlockSpec((1, seq, w), lambda b, g, i: (b, 0, 2 * ng + g)),
                  pl.BlockSpec((1, hb, seq // blk, 1, blk), lambda b, g, i: (b, g, 0, 0, 0))],
        out_specs=pl.BlockSpec((1, blk, w), lambda b, g, i: (b, i, g)),
        out_shape=jax.ShapeDtypeStruct((nb, seq, n_heads * dh), BF16),
        scratch_shapes=[pltpu.VMEM((hb, blk, LANES), F32),
                        pltpu.VMEM((hb, blk, 2 * dh), F32)],
        compiler_params=_cparams(("arbitrary", "arbitrary", "arbitrary")),
        name="fox_attn",
    )(qkv, qkv, qkv, cum)


def _post_attn_kernel(od_ref, of_ref, x_ref, w_ref, ga_ref, scm_ref, shm_ref, g_ref, b_ref,
                      wr_ref, br_ref,
                      x1_ref, hp_ref, idx_ref, gate_ref, rank_ref, cnt_ref, carry_scr,
                      *, n_experts, top_k, dil_width, sub):
    i = pl.program_id(0)

    @pl.when(i == 0)
    def _():
        carry_scr[...] = jnp.zeros_like(carry_scr)

    for r0 in range(0, x_ref.shape[0], sub):
        _post_attn_rows(slice(r0, r0 + sub), od_ref, of_ref, x_ref, w_ref, ga_ref, scm_ref, shm_ref,
                        g_ref, b_ref, wr_ref, br_ref, x1_ref, hp_ref, idx_ref, gate_ref, rank_ref,
                        carry_scr, n_experts=n_experts, top_k=top_k, dil_width=dil_width)
    cnt_ref[...] = jnp.broadcast_to(carry_scr[...], cnt_ref.shape)


def _post_attn_rows(rows, od_ref, of_ref, x_ref, w_ref, ga_ref, scm_ref, shm_ref, g_ref, b_ref,
                    wr_ref, br_ref, x1_ref, hp_ref, idx_ref, gate_ref, rank_ref, carry_scr,
                    *, n_experts, top_k, dil_width):
    attn = (_dot(od_ref[rows, :], w_ref[:dil_width, :]) + _dot(of_ref[rows, :], w_ref[dil_width:, :]))
    x1 = _layer_norm(DEEPNORM_ALPHA * x_ref[rows, :] + ga_ref[0] * attn, g_ref[...], b_ref[...])
    x1_ref[rows, :] = x1
    h = x1 * (1.0 + scm_ref[0]) + shm_ref[0]
    h_hi, h_lo = _split_bf16(h)
    hp_ref[rows, :] = h

    logits = _dot_split(h_hi, h_lo, wr_ref[...]) + br_ref[...]
    tm = logits.shape[0]
    lane = lax.broadcasted_iota(I32, (tm, LANES), 1)
    work = jnp.where(lane < n_experts, logits, -jnp.inf)
    vals, idxs = [], []
    for _ in range(top_k):
        mx = jnp.max(work, axis=1, keepdims=True)
        ix = jnp.min(jnp.where(work == mx, lane, LANES), axis=1, keepdims=True)
        vals.append(mx)
        idxs.append(ix)
        work = jnp.where(lane == ix, -jnp.inf, work)
    exps = [jnp.exp(v - vals[0]) for v in vals]
    denom = exps[0]
    for e in exps[1:]:
        denom = denom + e

    onehot = jnp.zeros((tm, LANES), F32)
    for ix in idxs:
        onehot = onehot + (lane == ix).astype(F32)
    r_i = lax.broadcasted_iota(I32, (tm, tm), 0)
    c_i = lax.broadcasted_iota(I32, (tm, tm), 1)
    tri = (c_i < r_i).astype(BF16)
    before = _dot(tri, onehot.astype(BF16)) + carry_scr[...]

    idx_out = jnp.zeros((tm, LANES), I32)
    gate_out = jnp.zeros((tm, LANES), F32)
    rank_out = jnp.zeros((tm, LANES), I32)
    for k in range(top_k):
        rank_k = jnp.sum(jnp.where(lane == idxs[k], before, 0.0), axis=1, keepdims=True)
        idx_out = jnp.where(lane == k, idxs[k], idx_out)
        gate_out = jnp.where(lane == k, exps[k] / denom, gate_out)
        rank_out = jnp.where(lane == k, rank_k.astype(I32), rank_out)
    gate_ref[rows, :] = gate_out
    idx_ref[:, rows] = idx_out.T[:idx_ref.shape[0], :]
    rank_ref[:, rows] = rank_out.T[:rank_ref.shape[0], :]
    carry_scr[...] = carry_scr[...] + jnp.sum(onehot, axis=0, keepdims=True)


def _post_attn(o_dil, o_fox, x2, w_out_bf, g_a, sc_m, sh_m, ln_g, ln_b, w_router, b_router, seq):
    t, d_model = x2.shape
    dil_width = o_dil.shape[1]
    n_experts = w_router.shape[1]
    tm, sub = 512, 512
    wr = _split_weight(jnp.pad(w_router, ((0, 0), (0, LANES - n_experts))))
    br = jnp.pad(b_router, (0, LANES - n_experts)).reshape(1, LANES)
    kern = functools.partial(_post_attn_kernel, n_experts=n_experts, top_k=TOP_K, dil_width=dil_width,
                             sub=sub)
    row = lambda w: pl.BlockSpec((tm, w), lambda i: (i, 0))
    col = pl.BlockSpec((8, tm), lambda i: (0, i))
    const = lambda r, w: pl.BlockSpec((r, w), lambda i: (0, 0))
    bvec = pl.BlockSpec((1, 1, d_model), lambda i: (i * tm // seq, 0, 0))
    return pl.pallas_call(
        kern,
        grid=(t // tm,),
        in_specs=[row(dil_width), row(o_fox.shape[1]), row(d_model), const(d_model, d_model),
                  bvec, bvec, bvec, const(1, d_model), const(1, d_model),
                  const(d_model, 2 * LANES), const(1, LANES)],
        out_specs=[row(d_model), row(d_model), col, row(LANES), col,
                   pl.BlockSpec((8, LANES), lambda i: (0, 0))],
        out_shape=[jax.ShapeDtypeStruct((t, d_model), F32),
                   jax.ShapeDtypeStruct((t, d_model), F32),
                   jax.ShapeDtypeStruct((8, t), I32),
                   jax.ShapeDtypeStruct((t, LANES), F32),
                   jax.ShapeDtypeStruct((8, t), I32),
                   jax.ShapeDtypeStruct((8, LANES), F32)],
        scratch_shapes=[pltpu.VMEM((1, LANES), F32)],
        compiler_params=_cparams(("arbitrary",)),
        name="post_attn",
    )(o_dil, o_fox, x2, w_out_bf, g_a, sc_m, sh_m, ln_g.reshape(1, -1), ln_b.reshape(1, -1),
      wr, br)


def _scatter_kernel(dest_ref, fill_lo_ref, fill_hi_ref, used_ref, hp_ref, xs_ref, zbuf, sem, zsem,
                    *, top_k):
    i = pl.program_id(0)
    tm = hp_ref.shape[0]
    rb = ROW_BLOCK

    @pl.when(i == 0)
    def _():
        zbuf[...] = jnp.zeros(zbuf.shape, zbuf.dtype)

        def tails(wait):
            def row(r, c):
                copy = pltpu.make_async_copy(zbuf.at[pl.ds(0, 1)], xs_ref.at[pl.ds(r, 1)], zsem.at[0])
                copy.wait() if wait else copy.start()
                return c

            def expert(e, c):
                return lax.fori_loop(fill_lo_ref[e], fill_hi_ref[e], row, c)

            lax.fori_loop(0, fill_lo_ref.shape[0], expert, 0)

        def rest(wait):
            def blk(b, c):
                copy = pltpu.make_async_copy(zbuf, xs_ref.at[pl.ds(pl.multiple_of(b * rb, rb), rb)],
                                             zsem.at[1])
                copy.wait() if wait else copy.start()
                return c

            lax.fori_loop(used_ref[0], xs_ref.shape[0] // rb, blk, 0)

        tails(False)
        rest(False)
        tails(True)
        rest(True)

    n_tokens = dest_ref.shape[0] // top_k
    base = i * tm
    for t in range(tm):
        for k in range(top_k):
            d = dest_ref[base + (k * n_tokens + t)]
            pltpu.make_async_copy(hp_ref.at[pl.ds(t, 1)], xs_ref.at[pl.ds(d, 1)], sem).start(
                priority=(t * top_k + k) % 2)

    for k in range(top_k):
        pltpu.make_async_copy(hp_ref, xs_ref.at[pl.ds(0, tm)], sem).wait()


def _scatter_rows(dest_flat, fill_lo, fill_hi, used_blocks, hp, n_rows):
    t, w = hp.shape
    tm = 512
    kern = functools.partial(_scatter_kernel, top_k=TOP_K)
    return pl.pallas_call(
        kern,
        grid_spec=pltpu.PrefetchScalarGridSpec(
            num_scalar_prefetch=4,
            grid=(t // tm,),
            in_specs=[pl.BlockSpec((tm, w), lambda i, *_: (i, 0))],
            out_specs=pl.BlockSpec(memory_space=pl.ANY),
            scratch_shapes=[pltpu.VMEM((ROW_BLOCK, w), F32),
                            pltpu.SemaphoreType.DMA(()),
                            pltpu.SemaphoreType.DMA((2,))]),
        out_shape=jax.ShapeDtypeStruct((n_rows, w), F32),
        compiler_params=_cparams(("arbitrary",)),
        name="scatter_rows",
    )(dest_flat, fill_lo, fill_hi, used_blocks, hp)


def _moe_kernel(iexp_ref, irow_ref, inu_ref, used_ref, xs_ref, wg_ref, wu_ref, wd_ref, bg_ref, bu_ref,
                bd_ref, y_ref, xbuf, yacc, sem_in, sem_out, *, n_chunks):
    del iexp_ref
    it = pl.program_id(0)
    f = pl.program_id(1)
    n_items = pl.num_programs(0)
    nu = inu_ref[it]
    rb = ROW_BLOCK
    rows_of = lambda k, n=1: pl.ds(pl.multiple_of(k * rb, rb), n * rb)

    nxt = jnp.minimum(it + 1, n_items - 1)
    n_next = jnp.where(it + 1 < n_items, inu_ref[nxt], 0)
    n_prev = jnp.where(it > 0, inu_ref[jnp.maximum(it - 1, 0)], 0)

    def x_copy(item, k, src_unit=None):
        src_unit = k if src_unit is None else src_unit
        src = pl.ds(pl.multiple_of(irow_ref[item] + src_unit * rb, rb), rb)
        return pltpu.make_async_copy(xs_ref.at[src], xbuf.at[rows_of(k)], sem_in)

    def y_copy(k):
        dst = pl.ds(pl.multiple_of(irow_ref[it] + k * rb, rb), rb)
        return pltpu.make_async_copy(yacc.at[rows_of(k)], y_ref.at[dst], sem_out)

    def for_units(n, fn):
        def body(k, c):
            fn(k)
            return c
        lax.fori_loop(0, n, body, 0)

    def init_unit(k):
        yacc[rows_of(k), :] = jnp.broadcast_to(bd_ref[0], (rb, yacc.shape[1]))

    first = f == 0
    last = f == n_chunks - 1

    n_arriving = jnp.where(it == 0, nu, jnp.where(n_prev > 0, jnp.maximum(n_prev, nu), 0))

    def wait_rows():
        @pl.when(first)
        def _():
            for_units(n_arriving, lambda k: x_copy(it, 0).wait())

    @pl.when(nu == 0)
    def _():
        wait_rows()

    @pl.when(nu > 0)
    def _():
        @pl.when(first & (it == 0))
        def _():
            for_units(nu, lambda k: x_copy(0, k).start())

        @pl.when(first)
        def _():
            for_units(nu, init_unit)

        wg = wg_ref[0].astype(BF16)
        wu = wu_ref[0].astype(BF16)
        wd = wd_ref[0].astype(BF16)
        bg = bg_ref[0]
        bu = bu_ref[0]

        wait_rows()

        def block(k, n, stream):
            xb = xbuf[rows_of(k, n), :].astype(BF16)
            g = _dot(xb, wg) + bg
            u = _dot(xb, wu) + bu
            g = jnp.minimum(g, SWIGLU_LIMIT)
            u = jnp.clip(u, -SWIGLU_LIMIT, SWIGLU_LIMIT)
            act = (u + 1.0) * (g * jax.nn.sigmoid(SWIGLU_ALPHA * g))
            yacc[rows_of(k, n), :] += _dot(act.astype(BF16), wd)
            if stream:
                for j in range(n):
                    y_copy(k + j).start()
                    x_copy(nxt, k + j, jnp.minimum(k + j, jnp.maximum(n_next, 1) - 1)).start()

        def run_blocks(stream):
            def quad(t, c):
                block(4 * t, 2, stream)
                block(4 * t + 2, 2, stream)
                return c

            n_quads = lax.shift_right_logical(nu, 2)
            lax.fori_loop(0, n_quads, quad, 0)
            done = n_quads * 4

            @pl.when((nu & 2) != 0)
            def _():
                block(done, 2, stream)

            @pl.when((nu & 1) != 0)
            def _():
                block(done + (nu & 2), 1, stream)

        @pl.when(jnp.logical_not(last))
        def _():
            run_blocks(False)

        @pl.when(last)
        def _():
            run_blocks(True)

            def rest(k, c):
                x_copy(nxt, k).start()
                return c
            lax.fori_loop(nu, n_next, rest, 0)
            for_units(nu, lambda k: y_copy(k).wait())

            @pl.when(it == n_items - 1)
            def _():
                for_units(nu, lambda k: x_copy(it, 0).wait())

    @pl.when((it == n_items - 1) & last)
    def _():
        lo = used_ref[0]
        hi = y_ref.shape[0] // rb
        dst_of = lambda b: pl.ds(pl.multiple_of(b * rb, rb), rb)
        yacc[pl.ds(0, rb), :] = jnp.zeros((rb, yacc.shape[1]), yacc.dtype)

        def start(b, c):
            pltpu.make_async_copy(yacc.at[pl.ds(0, rb)], y_ref.at[dst_of(b)], sem_out).start()
            return c

        def wait(b, c):
            pltpu.make_async_copy(yacc.at[pl.ds(0, rb)], y_ref.at[dst_of(b)], sem_out).wait()
            return c

        lax.fori_loop(lo, hi, start, 0)
        lax.fori_loop(lo, hi, wait, 0)


def _moe_experts(item_exp, item_row, item_nblk, used_blocks, xs, w_gate_up, b_gate_up, w_down, b_down):
    n_rows = xs.shape[0]
    n_exp, d_model, two_f = w_gate_up.shape
    d_ff = two_f // 2
    fc = FF_CHUNK
    n_chunks = d_ff // fc
    n_items = item_exp.shape[0]
    kern = functools.partial(_moe_kernel, n_chunks=n_chunks)

    def fsel(it, f, nbk):
        return jnp.where(nbk[it] > 0, f, n_chunks - 1)

    return pl.pallas_call(
        kern,
        grid_spec=pltpu.PrefetchScalarGridSpec(
            num_scalar_prefetch=4,
            grid=(n_items, n_chunks),
            in_specs=[
                pl.BlockSpec(memory_space=pl.ANY),
                pl.BlockSpec((1, d_model, fc), lambda it, f, ex, ro, nbk, us: (ex[it], 0, fsel(it, f, nbk))),
                pl.BlockSpec((1, d_model, fc),
                             lambda it, f, ex, ro, nbk, us: (ex[it], 0, n_chunks + fsel(it, f, nbk))),
                pl.BlockSpec((1, fc, d_model), lambda it, f, ex, ro, nbk, us: (ex[it], fsel(it, f, nbk), 0)),
                pl.BlockSpec((1, 1, fc), lambda it, f, ex, ro, nbk, us: (ex[it], 0, fsel(it, f, nbk))),
                pl.BlockSpec((1, 1, fc),
                             lambda it, f, ex, ro, nbk, us: (ex[it], 0, n_chunks + fsel(it, f, nbk))),
                pl.BlockSpec((1, 1, d_model), lambda it, f, ex, ro, nbk, us: (ex[it], 0, 0)),
            ],
            out_specs=pl.BlockSpec(memory_space=pl.ANY),
            scratch_shapes=[pltpu.VMEM((ITEM_ROWS, d_model), F32),
                            pltpu.VMEM((ITEM_ROWS, d_model), F32),
                            pltpu.SemaphoreType.DMA(()),
                            pltpu.SemaphoreType.DMA(())]),
        out_shape=jax.ShapeDtypeStruct((n_rows, d_model), F32),
        compiler_params=_cparams(("arbitrary", "arbitrary")),
        name="moe_experts",
    )(item_exp, item_row, item_nblk, used_blocks, xs, w_gate_up, w_gate_up, w_down,
      b_gate_up.reshape(n_exp, 1, two_f), b_gate_up.reshape(n_exp, 1, two_f),
      b_down.reshape(n_exp, 1, d_model))


def _combine_kernel(dest_ref, y_ref, x1_ref, gate_ref, gm_ref, g_ref, b_ref, o_ref, ybuf, sem, *, top_k):
    i = pl.program_id(0)
    n_steps = pl.num_programs(0)
    tm = x1_ref.shape[0]
    slot = i % 2

    def gather(step, slot_):
        n_tokens = dest_ref.shape[0] // top_k
        base = step * tm
        for t in range(tm):
            for k in range(top_k):
                d = dest_ref[base + (k * n_tokens + t)]
                pltpu.make_async_copy(y_ref.at[pl.ds(d, 1)], ybuf.at[slot_, pl.ds(k * tm + t, 1)],
                                      sem.at[slot_]).start(priority=(t * top_k + k) % 2)

    @pl.when(i == 0)
    def _():
        gather(0, 0)

    for nxt in range(2):
        @pl.when((i + 1 < n_steps) & (slot == 1 - nxt))
        def _(nxt=nxt):
            gather(i + 1, nxt)

    pltpu.make_async_copy(y_ref.at[pl.ds(0, top_k * tm)], ybuf.at[slot], sem.at[slot]).wait()
    gates = gate_ref[...]
    ffn = gates[:, 0:1] * ybuf[slot, pl.ds(0, tm), :]
    for k in range(1, top_k):
        ffn = ffn + gates[:, k:k + 1] * ybuf[slot, pl.ds(k * tm, tm), :]
    o_ref[...] = _layer_norm(DEEPNORM_ALPHA * x1_ref[...] + gm_ref[0] * ffn, g_ref[...], b_ref[...])


def _combine(dest_flat, y, x1, gates, g_m, ln_g, ln_b, seq):
    t, d_model = x1.shape
    tm = 128
    kern = functools.partial(_combine_kernel, top_k=TOP_K)
    return pl.pallas_call(
        kern,
        grid_spec=pltpu.PrefetchScalarGridSpec(
            num_scalar_prefetch=1,
            grid=(t // tm,),
            in_specs=[pl.BlockSpec(memory_space=pl.ANY),
                      pl.BlockSpec((tm, d_model), lambda i, d: (i, 0)),
                      pl.BlockSpec((tm, LANES), lambda i, d: (i, 0)),
                      pl.BlockSpec((1, 1, d_model), lambda i, d: (i * tm // seq, 0, 0)),
                      pl.BlockSpec((1, d_model), lambda i, d: (0, 0)),
                      pl.BlockSpec((1, d_model), lambda i, d: (0, 0))],
            out_specs=pl.BlockSpec((tm, d_model), lambda i, d: (i, 0)),
            scratch_shapes=[pltpu.VMEM((2, TOP_K * tm, d_model), F32),
                            pltpu.SemaphoreType.DMA((2,))]),
        out_shape=jax.ShapeDtypeStruct((t, d_model), F32),
        compiler_params=_cparams(("arbitrary",)),
        name="combine_ln2",
    )(dest_flat, y, x1, gates, g_m, ln_g.reshape(1, -1), ln_b.reshape(1, -1))


def _routing_plan(counts, idx, rank, n_tokens):
    n_exp = counts.shape[0]

    def take(table, index):
        hit = index[..., None] == jnp.arange(n_exp, dtype=I32)
        return jnp.sum(jnp.where(hit, table, 0), axis=-1)

    padded = (counts + ROW_BLOCK - 1) // ROW_BLOCK * ROW_BLOCK
    pad_end = jnp.cumsum(padded)
    pad_start = pad_end - padded
    dest = (take(pad_start, idx) + rank).reshape(-1).astype(I32)

    max_rows = (n_tokens * TOP_K + n_exp * (ROW_BLOCK - 1) + ROW_BLOCK - 1) // ROW_BLOCK * ROW_BLOCK
    max_items = n_exp + max_rows // ITEM_ROWS
    items_per = (padded + ITEM_ROWS - 1) // ITEM_ROWS
    items_end = jnp.cumsum(items_per)
    total = items_end[-1]
    it = jnp.arange(max_items, dtype=I32)
    last = jnp.maximum(total - 1, 0)
    it_c = jnp.minimum(it, last)
    exp_of = jnp.sum((items_end[None, :] <= it_c[:, None]).astype(I32), axis=1)
    exp_of = jnp.minimum(exp_of, n_exp - 1)
    local = it_c - take(items_end - items_per, exp_of)
    row0 = take(pad_start, exp_of) + local * ITEM_ROWS
    rows = jnp.minimum(take(padded, exp_of) - local * ITEM_ROWS, ITEM_ROWS)
    nblk = jnp.where(it < total, rows // ROW_BLOCK, 0)
    used_blocks = (pad_end[-1:] // ROW_BLOCK).astype(I32)
    fill = ((pad_start + counts).astype(I32), pad_end.astype(I32))
    return dest, fill, (exp_of.astype(I32), row0.astype(I32), nblk.astype(I32), used_blocks), max_rows


def kernel(x, c, w_ada, b_ada, w_in, b_forget, w_out, ln1_g, ln1_b, w_router, b_router,
           w_gate_up, b_gate_up, w_down, b_down, ln2_g, ln2_b):
    nb, seq, d_model = x.shape
    n_heads = d_model // HEAD_DIM
    n_dil = n_heads // 2
    n_fox = n_heads - n_dil
    dil_width, fox_width = n_dil * HEAD_DIM, n_fox * HEAD_DIM
    t = nb * seq
    layer = 0

    mod = _ada_mod(c, w_ada[layer], b_ada[layer])
    sh_a, sc_a, g_a, sh_m, sc_m, g_m = [m.reshape(nb, 1, d_model) for m in jnp.split(mod, 6, axis=-1)]

    x2 = x.reshape(t, d_model)
    qkv_dil, qkv_fox, f_logit = _in_proj(x2, sc_a, sh_a, w_in, layer, seq, dil_width, fox_width, n_fox)

    slopes = 2.0 ** (-8.0 * jnp.arange(1, n_dil + 1, dtype=F32) / n_dil)
    o_dil = _dilated_attention(qkv_dil.reshape(nb, seq, 3 * dil_width), slopes, nb, seq, n_dil)

    fox_blk = 512
    cum = _forget_cumsum(f_logit, b_forget[layer], nb, seq).reshape(
        nb, n_fox, seq // fox_blk, 1, fox_blk)
    o_fox = _fox_attention(qkv_fox.reshape(nb, seq, 3 * fox_width), cum, nb, seq, n_fox, fox_blk)

    x1, hp, idx, gates, rank, cnt = _post_attn(
        o_dil.reshape(t, dil_width), o_fox.reshape(t, fox_width), x2, w_out[layer].astype(BF16),
        g_a, sc_m, sh_m, ln1_g[layer], ln1_b[layer], w_router[layer], b_router[layer], seq)

    counts = cnt[0, :N_EXPERTS].astype(I32)
    dest, fill, items, max_rows = _routing_plan(counts, idx[:TOP_K], rank[:TOP_K], t)
    xs = _scatter_rows(dest, *fill, items[3], hp, max_rows)
    y = _moe_experts(*items, xs, w_gate_up[layer], b_gate_up[layer], w_down[layer], b_down[layer])
    out = _combine(dest, y, x1, gates, g_m, ln2_g[layer], ln2_b[layer], seq)
    return out.reshape(nb, seq, d_model)
```

```python
import functools
import math

import jax
import jax.numpy as jnp
from jax import lax
from jax.experimental import pallas as pl
from jax.experimental.pallas import tpu as pltpu

F32 = jnp.float32
BF16 = jnp.bfloat16
I32 = jnp.int32
U32 = jnp.uint32

HEAD_DIM = 128
Q_BLOCK = 128
DIL_PATTERNS = ((128, 1), (512, 4), (2048, 16))
N_EXPERTS = 32
TOP_K = 4
SWIGLU_LIMIT = 7.0
SWIGLU_ALPHA = 1.702
LN_EPS = 1e-5
DEPTH = 1
DEEPNORM_ALPHA = (2.0 * DEPTH) ** 0.25

LOG2E = math.log2(math.e)
DIL_GROUP = 32
DEINT = 4
FOX_HEADS_PER_STEP = 8

LANES = 128
VMEM_LIMIT = 56 * 1024 * 1024

ROW_BLOCK = 128
ITEM_ROWS = 1280
FF_CHUNK = 512


def _cparams(sem):
    return pltpu.CompilerParams(dimension_semantics=sem, vmem_limit_bytes=VMEM_LIMIT)


def _dot(a, b):
    return jnp.dot(a, b, preferred_element_type=F32)


def _dot_nt(a, b):
    return lax.dot_general(a, b, (((1,), (1,)), ((), ())), preferred_element_type=F32)


def _split_bf16(x):
    hi = x.astype(BF16)
    lo = (x - hi.astype(F32)).astype(BF16)
    return hi, lo


def _split_weight(w):
    return jnp.concatenate(_split_bf16(w), axis=1)


def _dot_split(x_hi, x_lo, w_cat):
    r = _dot(x_hi, w_cat) + _dot(x_lo, w_cat)
    return r[:, :LANES] + r[:, LANES:]


def _layer_norm(y, g, b):
    mu = jnp.mean(y, axis=-1, keepdims=True)
    yc = y - mu
    var = jnp.mean(yc * yc, axis=-1, keepdims=True)
    return yc * lax.rsqrt(var + LN_EPS) * g + b


def _ada_kernel(ct_ref, w_ref, b_ref, o_ref):
    ct = ct_ref[...]
    s = ct * jax.nn.sigmoid(ct)
    d_model, nb = ct.shape
    tn = w_ref.shape[1]
    sb = [jnp.broadcast_to(s[:, b:b + 1], (d_model, LANES)) for b in range(nb)]
    for j in range(tn // LANES):
        cols = slice(j * LANES, (j + 1) * LANES)
        w = w_ref[:, cols]
        for b in range(nb):
            acc = jnp.sum(w * sb[b], axis=0, keepdims=True)
            o_ref[b:b + 1, cols] = acc + b_ref[:, cols]


def _ada_mod(c, w, b):
    nb, d_model = c.shape
    n = w.shape[1]
    tn = 1536 if n % 1536 == 0 else LANES
    return pl.pallas_call(
        _ada_kernel,
        grid=(n // tn,),
        in_specs=[pl.BlockSpec((d_model, nb), lambda j: (0, 0)),
                  pl.BlockSpec((d_model, tn), lambda j: (0, j)),
                  pl.BlockSpec((1, tn), lambda j: (0, j))],
        out_specs=pl.BlockSpec((nb, tn), lambda j: (0, j)),
        out_shape=jax.ShapeDtypeStruct((nb, n), F32),
        compiler_params=_cparams(("arbitrary",)),
        name="ada_mod",
    )(c.T, w, b.reshape(1, n))


def _inproj_kernel(x_ref, sc_ref, sh_ref, w_ref, wf_ref,
                   odil_ref, ofox_ref, of_ref, h_scr, *, n_dil_tiles, q_tiles, scale):
    j = pl.program_id(1)

    @pl.when(j == 0)
    def _():
        h = x_ref[...] * (1.0 + sc_ref[0]) + sh_ref[0]
        h_hi, h_lo = _split_bf16(h)
        h_scr[...] = h_hi
        of_ref[...] = _dot_split(h_hi, h_lo, wf_ref[...])

    acc = _dot(h_scr[...], w_ref[...])
    is_q =(j < q_tiles) | ((j >= n_dil_tiles) & (j < n_dil_tiles + q_tiles))
    acc = acc * jnp.where(is_q, scale, 1.0).astype(F32)

    @pl.when(j < n_dil_tiles)
    def _():
        odil_ref[...] = acc

    @pl.when(j >= n_dil_tiles)
    def _():
        ofox_ref[...] = acc.astype(BF16)


def _in_proj(x2, sc, sh, w_in, layer, seq, dil_width, fox_width, n_fox_heads):
    t, d_model = x2.shape
    tm, tn = min(1024, seq), min(1024, dil_width)
    n_dil_tiles = 3 * dil_width // tn
    n_fox_tiles = 3 * fox_width // tn
    q_tiles = dil_width // tn
    assert dil_width == fox_width and dil_width % tn == 0 and seq % tm == 0
    main = 3 * dil_width + 3 * fox_width
    wf = _split_weight(jnp.pad(w_in[layer, :, main:], ((0, 0), (0, LANES - n_fox_heads))))
    kern = functools.partial(_inproj_kernel, n_dil_tiles=n_dil_tiles, q_tiles=q_tiles,
                             scale=LOG2E / math.sqrt(HEAD_DIM))
    bidx = lambda i, j: (i * tm // seq, 0, 0)
    return pl.pallas_call(
        kern,
        grid=(t // tm, n_dil_tiles + n_fox_tiles),
        in_specs=[pl.BlockSpec((tm, d_model), lambda i, j: (i, 0)),
                  pl.BlockSpec((1, 1, d_model), bidx),
                  pl.BlockSpec((1, 1, d_model), bidx),
                  pl.BlockSpec((None, d_model, tn), lambda i, j: (layer, 0, j)),
                  pl.BlockSpec((d_model, 2 * LANES), lambda i, j: (0, 0))],
        out_specs=[pl.BlockSpec((tm, tn), lambda i, j: (i, jnp.minimum(j, n_dil_tiles - 1))),
                   pl.BlockSpec((tm, tn), lambda i, j: (i, jnp.maximum(j - n_dil_tiles, 0))),
                   pl.BlockSpec((tm, LANES), lambda i, j: (i, 0))],
        out_shape=[jax.ShapeDtypeStruct((t, 3 * dil_width), F32),
                   jax.ShapeDtypeStruct((t, 3 * fox_width), BF16),
                   jax.ShapeDtypeStruct((t, LANES), F32)],
        scratch_shapes=[pltpu.VMEM((tm, d_model), BF16)],
        compiler_params=_cparams(("arbitrary", "arbitrary")),
        name="in_proj",
    )(x2, sc, sh, w_in.astype(BF16), wf)


def _cum_kernel(f_ref, b_ref, o_ref):
    hf = o_ref.shape[1]
    z = f_ref[...].T[:hf, :] + b_ref[...]
    c = jnp.minimum(z, 0.0) - jnp.log1p(jnp.exp(-jnp.abs(z)))
    seq = z.shape[1]
    lane = lax.broadcasted_iota(I32, z.shape, 1)
    shift = 1
    while shift < seq:
        c = c + jnp.where(lane >= shift, pltpu.roll(c, shift, axis=1), 0.0)
        shift *= 2
    o_ref[0] = c * LOG2E


def _forget_cumsum(f_logit, b_forget, nb, seq):
    hf = b_forget.shape[0]
    return pl.pallas_call(
        _cum_kernel,
        grid=(nb,),
        in_specs=[pl.BlockSpec((seq, LANES), lambda b: (b, 0)),
                  pl.BlockSpec((hf, 1), lambda b: (0, 0))],
        out_specs=pl.BlockSpec((1, hf, seq), lambda b: (b, 0, 0)),
        out_shape=jax.ShapeDtypeStruct((nb, hf, seq), F32),
        compiler_params=_cparams(("arbitrary",)),
        name="forget_cumsum",
    )(f_logit, b_forget.reshape(hf, 1))


def _dil_kernel(slope_ref, q_ref, k_ref, v_ref, o_ref, q4_ref, k4_ref, v4_ref, acc_ref, l_ref, m_ref,
                bias_ref, *, patterns, seq):
    slope = slope_ref[pl.program_id(1)]
    qb = Q_BLOCK
    row = lax.broadcasted_iota(I32, (qb, 2 * qb), 0)
    col = lax.broadcasted_iota(I32, (qb, 2 * qb), 1)
    delta = row + qb - col
    is_prev = col < qb
    neg_inf = F32(-jnp.inf)
    ones = jnp.ones((2 * qb, HEAD_DIM), BF16)

    sub = seq // DEINT
    chunk = min(256, sub)
    for r4 in range(DEINT):
        for c0 in range(0, sub, chunk):
            src = pl.ds(r4 + DEINT * c0, chunk, stride=DEINT)
            dst = pl.ds(r4 * sub + c0, chunk)
            q4_ref[dst, :] = q_ref[0, src, :]
            k4_ref[dst, :] = k_ref[0, src, :]
            v4_ref[dst, :] = v_ref[0, src, :]

    for pi, (window, dil) in enumerate(sorted(patterns, key=lambda wd: -wd[1])):
        reach = window // dil
        nblk = seq // dil // qb
        n_blocks = dil * nblk
        group = min(DIL_GROUP, n_blocks)
        assert reach <= qb and nblk * qb * dil == seq and n_blocks % group == 0
        sd = slope * (float(dil) * LOG2E)
        bias_ref[...] = jnp.where((delta >= 0) & (delta <= reach), -sd * delta.astype(F32), neg_inf)

        def block(idx, dil=dil, nblk=nblk, pi=pi):
            r = idx // nblk
            n = idx - r * nblk
            start = r + n * (qb * dil)
            cur = pl.ds(start, qb, stride=dil) if dil > 1 else pl.ds(pl.multiple_of(start, qb), qb)
            if dil % DEINT == 0:
                step = dil // DEINT
                first = (r % DEINT) * sub + r // DEINT
                lstart = first + n * (qb * step)
                lprev = jnp.maximum(lstart - qb * step, first)
                if step > 1:
                    src_cur, src_prev = pl.ds(lstart, qb, stride=step), pl.ds(lprev, qb, stride=step)
                else:
                    src_cur = pl.ds(pl.multiple_of(lstart, qb), qb)
                    src_prev = pl.ds(pl.multiple_of(lprev, qb), qb)
                q = q4_ref[src_cur, :].astype(BF16)
                kk = jnp.concatenate([k4_ref[src_prev, :], k4_ref[src_cur, :]], axis=0).astype(BF16)
                vv = jnp.concatenate([v4_ref[src_prev, :], v4_ref[src_cur, :]], axis=0).astype(BF16)
            else:
                pstart = jnp.maximum(start - qb * dil, r)
                prev = (pl.ds(pstart, qb, stride=dil) if dil > 1
                        else pl.ds(pl.multiple_of(pstart, qb), qb))
                q = q_ref[0, cur, :].astype(BF16)
                kk = jnp.concatenate([k_ref[0, prev, :], k_ref[0, cur, :]], axis=0).astype(BF16)
                vv = jnp.concatenate([v_ref[0, prev, :], v_ref[0, cur, :]], axis=0).astype(BF16)
            s = _dot_nt(q, kk) + bias_ref[...]
            s = jnp.where(is_prev & (n == 0), neg_inf, s)
            m_blk = jnp.max(s, axis=1, keepdims=True)
            p = jnp.exp2(s - m_blk)
            pv = _dot(p.astype(BF16), jnp.concatenate([vv, ones], axis=1))
            o_blk, l_blk = pv[:, :HEAD_DIM], pv[:, HEAD_DIM:]
            if pi == 0:
                m_ref[cur, :] = jnp.broadcast_to(m_blk, (qb, LANES))
                acc_ref[cur, :] = o_blk
                l_ref[cur, :] = l_blk
            else:
                m_old = m_ref[cur, :]
                m_new = jnp.maximum(m_old, m_blk)
                a_old = jnp.exp2(m_old - m_new)
                a_blk = jnp.exp2(m_blk - m_new)
                m_ref[cur, :] = m_new
                acc_ref[cur, :] = acc_ref[cur, :] * a_old + o_blk * a_blk
                l_ref[cur, :] = l_ref[cur, :] * a_old + l_blk * a_blk

        def body(g, carry, block=block, group=group):
            for u in range(group):
                block(g * group + u)
            return carry

        lax.fori_loop(0, n_blocks // group, body, 0)

    chunk = 512
    for c0 in range(0, seq, chunk):
        rows = slice(c0, c0 + chunk)
        o_ref[0, rows, :] = (acc_ref[rows, :] / l_ref[rows, :]).astype(o_ref.dtype)


def _dilated_attention(qkv, slopes, nb, seq, n_heads):
    dh = HEAD_DIM
    kern = functools.partial(_dil_kernel, patterns=DIL_PATTERNS, seq=seq)
    blk = lambda off: pl.BlockSpec((1, seq, dh), lambda b, h, s: (b, 0, off + h))
    return pl.pallas_call(
        kern,
        grid_spec=pltpu.PrefetchScalarGridSpec(
            num_scalar_prefetch=1,
            grid=(nb, n_heads),
            in_specs=[blk(0), blk(n_heads), blk(2 * n_heads)],
            out_specs=pl.BlockSpec((1, seq, dh), lambda b, h, s: (b, 0, h)),
            scratch_shapes=[pltpu.VMEM((seq, dh), F32),
                            pltpu.VMEM((seq, dh), F32),
                            pltpu.VMEM((seq, dh), F32),
                            pltpu.VMEM((seq, dh), F32),
                            pltpu.VMEM((seq, LANES), F32),
                            pltpu.VMEM((seq, LANES), F32),
                            pltpu.VMEM((Q_BLOCK, 2 * Q_BLOCK), F32)]),
        out_shape=jax.ShapeDtypeStruct((nb, seq, n_heads * dh), BF16),
        compiler_params=_cparams(("arbitrary", "arbitrary")),
        name="dilated_attn",
    )(slopes, qkv, qkv, qkv)


def _fox_kernel(q_ref, k_ref, v_ref, c_ref, o_ref, m_scr, acc_scr, *, blk, hb):
    i = pl.program_id(2)
    dh = HEAD_DIM
    row = lax.broadcasted_iota(I32, (blk, blk), 0)
    col = lax.broadcasted_iota(I32, (blk, blk), 1)
    m_scr[...] = jnp.full(m_scr.shape, -jnp.inf, F32)
    acc_scr[...] = jnp.zeros(acc_scr.shape, F32)
    ones = jnp.ones((blk, dh), BF16)
    n_slab = blk // LANES

    def step(j, masked):
        rows = pl.ds(pl.multiple_of(j * blk, blk), blk)
        for h in range(hb):
            cols = slice(h * dh, (h + 1) * dh)
            s = _dot_nt(q_ref[0, :, cols], k_ref[0, rows, cols]) - c_ref[0, h, j]
            if masked:
                s = jnp.where(col <= row, s, -jnp.inf)
            m_old = m_scr[h]
            m_new = jnp.maximum(m_old, jnp.max(s, axis=1, keepdims=True))
            alpha = jnp.exp2(m_old - m_new)
            p = jnp.concatenate(
                [jnp.exp2(s[:, t * LANES:(t + 1) * LANES] - m_new) for t in range(n_slab)], axis=1)
            pv = _dot(p.astype(BF16), jnp.concatenate([v_ref[0, rows, cols], ones], axis=1))
            acc_scr[h] = jnp.concatenate([alpha, alpha], axis=1) * acc_scr[h] + pv
            m_scr[h] = m_new

    def body(j, carry):
        step(j, False)
        return carry

    lax.fori_loop(0, i, body, 0)
    step(i, True)
    for h in range(hb):
        acc = acc_scr[h]
        o_ref[0, :, h * dh:(h + 1) * dh] = (acc[:, :dh] / acc[:, dh:]).astype(o_ref.dtype)


def _fox_attention(qkv, cum, nb, seq, n_heads, blk):
    dh = HEAD_DIM
    hb = min(FOX_HEADS_PER_STEP, n_heads)
    assert n_heads % hb == 0
    ng = n_heads // hb
    w = hb * dh
    kern = functools.partial(_fox_kernel, blk=blk, hb=hb)
    return pl.pallas_call(
        kern,
        grid=(nb, ng, seq // blk),
        in_specs=[pl.BlockSpec((1, blk, w), lambda b, g, i: (b, i, g)),
                  pl.BlockSpec((1, seq, w), lambda b, g, i: (b, 0, ng + g)),
                  pl.BlockSpec((1, seq, w), lambda b, g, i: (b, 0, 2 * ng + g)),
                  pl.BlockSpec((1, hb, seq // blk, 1, blk), lambda b, g, i: (b, g, 0, 0, 0))],
        out_specs=pl.BlockSpec((1, blk, w), lambda b, g, i: (b, i, g)),
        out_shape=jax.ShapeDtypeStruct((nb, seq, n_heads * dh), BF16),
        scratch_shapes=[pltpu.VMEM((hb, blk, LANES), F32),
                        pltpu.VMEM((hb, blk, 2 * dh), F32)],
        compiler_params=_cparams(("arbitrary", "arbitrary", "arbitrary")),
        name="fox_attn",
    )(qkv, qkv, qkv, cum)


def _post_attn_kernel(od_ref, of_ref, x_ref, w_ref, ga_ref, scm_ref, shm_ref, g_ref, b_ref,
                      wr_ref, br_ref,
                      x1_ref, hp_ref, idx_ref, gate_ref, rank_ref, cnt_ref, carry_scr,
                      *, n_experts, top_k, dil_width, sub):
    i = pl.program_id(0)

    @pl.when(i == 0)
    def _():
        carry_scr[...] = jnp.zeros_like(carry_scr)

    for r0 in range(0, x_ref.shape[0], sub):
        _post_attn_rows(slice(r0, r0 + sub), od_ref, of_ref, x_ref, w_ref, ga_ref, scm_ref, shm_ref,
                        g_ref, b_ref, wr_ref, br_ref, x1_ref, hp_ref, idx_ref, gate_ref, rank_ref,
                        carry_scr, n_experts=n_experts, top_k=top_k, dil_width=dil_width)
    cnt_ref[...] = jnp.broadcast_to(carry_scr[...], cnt_ref.shape)


def _post_attn_rows(rows, od_ref, of_ref, x_ref, w_ref, ga_ref, scm_ref, shm_ref, g_ref, b_ref,
                    wr_ref, br_ref, x1_ref, hp_ref, idx_ref, gate_ref, rank_ref, carry_scr,
                    *, n_experts, top_k, dil_width):
    attn = (_dot(od_ref[rows, :], w_ref[:dil_width, :]) + _dot(of_ref[rows, :], w_ref[dil_width:, :]))
    x1 = _layer_norm(DEEPNORM_ALPHA * x_ref[rows, :] + ga_ref[0] * attn, g_ref[...], b_ref[...])
    x1_ref[rows, :] = x1
    h = x1 * (1.0 + scm_ref[0]) + shm_ref[0]
    h_hi, h_lo = _split_bf16(h)
    hp_ref[rows, :] = h

    logits = _dot_split(h_hi, h_lo, wr_ref[...]) + br_ref[...]
    tm = logits.shape[0]
    lane = lax.broadcasted_iota(I32, (tm, LANES), 1)
    work = jnp.where(lane < n_experts, logits, -jnp.inf)
    vals, idxs = [], []
    for _ in range(top_k):
        mx = jnp.max(work, axis=1, keepdims=True)
        ix = jnp.min(jnp.where(work == mx, lane, LANES), axis=1, keepdims=True)
        vals.append(mx)
        idxs.append(ix)
        work = jnp.where(lane == ix, -jnp.inf, work)
    exps = [jnp.exp(v - vals[0]) for v in vals]
    denom = exps[0]
    for e in exps[1:]:
        denom = denom + e

    onehot = jnp.zeros((tm, LANES), F32)
    for ix in idxs:
        onehot = onehot + (lane == ix).astype(F32)
    r_i = lax.broadcasted_iota(I32, (tm, tm), 0)
    c_i = lax.broadcasted_iota(I32, (tm, tm), 1)
    tri = (c_i < r_i).astype(BF16)
    before = _dot(tri, onehot.astype(BF16)) + carry_scr[...]

    idx_out = jnp.zeros((tm, LANES), I32)
    gate_out = jnp.zeros((tm, LANES), F32)
    rank_out = jnp.zeros((tm, LANES), I32)
    for k in range(top_k):
        rank_k = jnp.sum(jnp.where(lane == idxs[k], before, 0.0), axis=1, keepdims=True)
        idx_out = jnp.where(lane == k, idxs[k], idx_out)
        gate_out = jnp.where(lane == k, exps[k] / denom, gate_out)
        rank_out = jnp.where(lane == k, rank_k.astype(I32), rank_out)
    gate_ref[rows, :] = gate_out
    idx_ref[:, rows] = idx_out.T[:idx_ref.shape[0], :]
    rank_ref[:, rows] = rank_out.T[:rank_ref.shape[0], :]
    carry_scr[...] = carry_scr[...] + jnp.sum(onehot, axis=0, keepdims=True)


def _post_attn(o_dil, o_fox, x2, w_out_bf, g_a, sc_m, sh_m, ln_g, ln_b, w_router, b_router, seq):
    t, d_model = x2.shape
    dil_width = o_dil.shape[1]
    n_experts = w_router.shape[1]
    tm, sub = 512, 512
    wr = _split_weight(jnp.pad(w_router, ((0, 0), (0, LANES - n_experts))))
    br = jnp.pad(b_router, (0, LANES - n_experts)).reshape(1, LANES)
    kern = functools.partial(_post_attn_kernel, n_experts=n_experts, top_k=TOP_K, dil_width=dil_width,
                             sub=sub)
    row = lambda w: pl.BlockSpec((tm, w), lambda i: (i, 0))
    col = pl.BlockSpec((8, tm), lambda i: (0, i))
    const = lambda r, w: pl.BlockSpec((r, w), lambda i: (0, 0))
    bvec = pl.BlockSpec((1, 1, d_model), lambda i: (i * tm // seq, 0, 0))
    return pl.pallas_call(
        kern,
        grid=(t // tm,),
        in_specs=[row(dil_width), row(o_fox.shape[1]), row(d_model), const(d_model, d_model),
                  bvec, bvec, bvec, const(1, d_model), const(1, d_model),
                  const(d_model, 2 * LANES), const(1, LANES)],
        out_specs=[row(d_model), row(d_model), col, row(LANES), col,
                   pl.BlockSpec((8, LANES), lambda i: (0, 0))],
        out_shape=[jax.ShapeDtypeStruct((t, d_model), F32),
                   jax.ShapeDtypeStruct((t, d_model), F32),
                   jax.ShapeDtypeStruct((8, t), I32),
                   jax.ShapeDtypeStruct((t, LANES), F32),
                   jax.ShapeDtypeStruct((8, t), I32),
                   jax.ShapeDtypeStruct((8, LANES), F32)],
        scratch_shapes=[pltpu.VMEM((1, LANES), F32)],
        compiler_params=_cparams(("arbitrary",)),
        name="post_attn",
    )(o_dil, o_fox, x2, w_out_bf, g_a, sc_m, sh_m, ln_g.reshape(1, -1), ln_b.reshape(1, -1),
      wr, br)


def _scatter_kernel(dest_ref, fill_lo_ref, fill_hi_ref, used_ref, hp_ref, xs_ref, zbuf, sem, zsem,
                    *, top_k):
    i = pl.program_id(0)
    tm = hp_ref.shape[0]
    rb = ROW_BLOCK

    @pl.when(i == 0)
    def _():
        zbuf[...] = jnp.zeros(zbuf.shape, zbuf.dtype)

        def tails(wait):
            def row(r, c):
                copy = pltpu.make_async_copy(zbuf.at[pl.ds(0, 1)], xs_ref.at[pl.ds(r, 1)], zsem.at[0])
                copy.wait() if wait else copy.start()
                return c

            def expert(e, c):
                return lax.fori_loop(fill_lo_ref[e], fill_hi_ref[e], row, c)

            lax.fori_loop(0, fill_lo_ref.shape[0], expert, 0)

        def rest(wait):
            def blk(b, c):
                copy = pltpu.make_async_copy(zbuf, xs_ref.at[pl.ds(pl.multiple_of(b * rb, rb), rb)],
                                             zsem.at[1])
                copy.wait() if wait else copy.start()
                return c

            lax.fori_loop(used_ref[0], xs_ref.shape[0] // rb, blk, 0)

        tails(False)
        rest(False)
        tails(True)
        rest(True)

    n_tokens = dest_ref.shape[0] // top_k
    base = i * tm
    for t in range(tm):
        for k in range(top_k):
            d = dest_ref[base + (k * n_tokens + t)]
            pltpu.make_async_copy(hp_ref.at[pl.ds(t, 1)], xs_ref.at[pl.ds(d, 1)], sem).start(
                priority=(t * top_k + k) % 2)

    for k in range(top_k):
        pltpu.make_async_copy(hp_ref, xs_ref.at[pl.ds(0, tm)], sem).wait()


def _scatter_rows(dest_flat, fill_lo, fill_hi, used_blocks, hp, n_rows):
    t, w = hp.shape
    tm = 512
    kern = functools.partial(_scatter_kernel, top_k=TOP_K)
    return pl.pallas_call(
        kern,
        grid_spec=pltpu.PrefetchScalarGridSpec(
            num_scalar_prefetch=4,
            grid=(t // tm,),
            in_specs=[pl.BlockSpec((tm, w), lambda i, *_: (i, 0))],
            out_specs=pl.BlockSpec(memory_space=pl.ANY),
            scratch_shapes=[pltpu.VMEM((ROW_BLOCK, w), F32),
                            pltpu.SemaphoreType.DMA(()),
                            pltpu.SemaphoreType.DMA((2,))]),
        out_shape=jax.ShapeDtypeStruct((n_rows, w), F32),
        compiler_params=_cparams(("arbitrary",)),
        name="scatter_rows",
    )(dest_flat, fill_lo, fill_hi, used_blocks, hp)


def _moe_kernel(iexp_ref, irow_ref, inu_ref, used_ref, xs_ref, wg_ref, wu_ref, wd_ref, bg_ref, bu_ref,
                bd_ref, y_ref, xbuf, yacc, sem_in, sem_out, *, n_chunks):
    del iexp_ref
    it = pl.program_id(0)
    f = pl.program_id(1)
    n_items = pl.num_programs(0)
    nu = inu_ref[it]
    rb = ROW_BLOCK
    rows_of = lambda k, n=1: pl.ds(pl.multiple_of(k * rb, rb), n * rb)

    nxt = jnp.minimum(it + 1, n_items - 1)
    n_next = jnp.where(it + 1 < n_items, inu_ref[nxt], 0)
    n_prev = jnp.where(it > 0, inu_ref[jnp.maximum(it - 1, 0)], 0)

    def x_copy(item, k, src_unit=None):
        src_unit = k if src_unit is None else src_unit
        src = pl.ds(pl.multiple_of(irow_ref[item] + src_unit * rb, rb), rb)
        return pltpu.make_async_copy(xs_ref.at[src], xbuf.at[rows_of(k)], sem_in)

    def y_copy(k):
        dst = pl.ds(pl.multiple_of(irow_ref[it] + k * rb, rb), rb)
        return pltpu.make_async_copy(yacc.at[rows_of(k)], y_ref.at[dst], sem_out)

    def for_units(n, fn):
        def body(k, c):
            fn(k)
            return c
        lax.fori_loop(0, n, body, 0)

    def init_unit(k):
        yacc[rows_of(k), :] = jnp.broadcast_to(bd_ref[0], (rb, yacc.shape[1]))

    first = f == 0
    last = f == n_chunks - 1

    n_arriving = jnp.where(it == 0, nu, jnp.where(n_prev > 0, jnp.maximum(n_prev, nu), 0))

    def wait_rows():
        @pl.when(first)
        def _():
            for_units(n_arriving, lambda k: x_copy(it, 0).wait())

    @pl.when(nu == 0)
    def _():
        wait_rows()

    @pl.when(nu > 0)
    def _():
        @pl.when(first & (it == 0))
        def _():
            for_units(nu, lambda k: x_copy(0, k).start())

        @pl.when(first)
        def _():
            for_units(nu, init_unit)

        wg = wg_ref[0].astype(BF16)
        wu = wu_ref[0].astype(BF16)
        wd = wd_ref[0].astype(BF16)
        bg = bg_ref[0]
        bu = bu_ref[0]

        wait_rows()

        def block(k, n, stream):
            xb = xbuf[rows_of(k, n), :].astype(BF16)
            g = _dot(xb, wg) + bg
            u = _dot(xb, wu) + bu
            g = jnp.minimum(g, SWIGLU_LIMIT)
            u = jnp.clip(u, -SWIGLU_LIMIT, SWIGLU_LIMIT)
            act = (u + 1.0) * (g * jax.nn.sigmoid(SWIGLU_ALPHA * g))
            yacc[rows_of(k, n), :] += _dot(act.astype(BF16), wd)
            if stream:
                for j in range(n):
                    y_copy(k + j).start()
                    x_copy(nxt, k + j, jnp.minimum(k + j, jnp.maximum(n_next, 1) - 1)).start()

        def run_blocks(stream):
            def quad(t, c):
                block(4 * t, 2, stream)
                block(4 * t + 2, 2, stream)
                return c

            n_quads = lax.shift_right_logical(nu, 2)
            lax.fori_loop(0, n_quads, quad, 0)
            done = n_quads * 4

            @pl.when((nu & 2) != 0)
            def _():
                block(done, 2, stream)

            @pl.when((nu & 1) != 0)
            def _():
                block(done + (nu & 2), 1, stream)

        @pl.when(jnp.logical_not(last))
        def _():
            run_blocks(False)

        @pl.when(last)
        def _():
            run_blocks(True)

            def rest(k, c):
                x_copy(nxt, k).start()
                return c
            lax.fori_loop(nu, n_next, rest, 0)
            for_units(nu, lambda k: y_copy(k).wait())

            @pl.when(it == n_items - 1)
            def _():
                for_units(nu, lambda k: x_copy(it, 0).wait())

    @pl.when((it == n_items - 1) & last)
    def _():
        lo = used_ref[0]
        hi = y_ref.shape[0] // rb
        dst_of = lambda b: pl.ds(pl.multiple_of(b * rb, rb), rb)
        yacc[pl.ds(0, rb), :] = jnp.zeros((rb, yacc.shape[1]), yacc.dtype)

        def start(b, c):
            pltpu.make_async_copy(yacc.at[pl.ds(0, rb)], y_ref.at[dst_of(b)], sem_out).start()
            return c

        def wait(b, c):
            pltpu.make_async_copy(yacc.at[pl.ds(0, rb)], y_ref.at[dst_of(b)], sem_out).wait()
            return c

        lax.fori_loop(lo, hi, start, 0)
        lax.fori_loop(lo, hi, wait, 0)


def _moe_experts(item_exp, item_row, item_nblk, used_blocks, xs, w_gate_up, b_gate_up, w_down, b_down):
    n_rows = xs.shape[0]
    n_exp, d_model, two_f = w_gate_up.shape
    d_ff = two_f // 2
    fc = FF_CHUNK
    n_chunks = d_ff // fc
    n_items = item_exp.shape[0]
    kern = functools.partial(_moe_kernel, n_chunks=n_chunks)

    def fsel(it, f, nbk):
        return jnp.where(nbk[it] > 0, f, n_chunks - 1)

    return pl.pallas_call(
        kern,
        grid_spec=pltpu.PrefetchScalarGridSpec(
            num_scalar_prefetch=4,
            grid=(n_items, n_chunks),
            in_specs=[
                pl.BlockSpec(memory_space=pl.ANY),
                pl.BlockSpec((1, d_model, fc), lambda it, f, ex, ro, nbk, us: (ex[it], 0, fsel(it, f, nbk))),
                pl.BlockSpec((1, d_model, fc),
                             lambda it, f, ex, ro, nbk, us: (ex[it], 0, n_chunks + fsel(it, f, nbk))),
                pl.BlockSpec((1, fc, d_model), lambda it, f, ex, ro, nbk, us: (ex[it], fsel(it, f, nbk), 0)),
                pl.BlockSpec((1, 1, fc), lambda it, f, ex, ro, nbk, us: (ex[it], 0, fsel(it, f, nbk))),
                pl.BlockSpec((1, 1, fc),
                             lambda it, f, ex, ro, nbk, us: (ex[it], 0, n_chunks + fsel(it, f, nbk))),
                pl.BlockSpec((1, 1, d_model), lambda it, f, ex, ro, nbk, us: (ex[it], 0, 0)),
            ],
            out_specs=pl.BlockSpec(memory_space=pl.ANY),
            scratch_shapes=[pltpu.VMEM((ITEM_ROWS, d_model), F32),
                            pltpu.VMEM((ITEM_ROWS, d_model), F32),
                            pltpu.SemaphoreType.DMA(()),
                            pltpu.SemaphoreType.DMA(())]),
        out_shape=jax.ShapeDtypeStruct((n_rows, d_model), F32),
        compiler_params=_cparams(("arbitrary", "arbitrary")),
        name="moe_experts",
    )(item_exp, item_row, item_nblk, used_blocks, xs, w_gate_up, w_gate_up, w_down,
      b_gate_up.reshape(n_exp, 1, two_f), b_gate_up.reshape(n_exp, 1, two_f),
      b_down.reshape(n_exp, 1, d_model))


def _combine_kernel(dest_ref, y_ref, x1_ref, gate_ref, gm_ref, g_ref, b_ref, o_ref, ybuf, sem, *, top_k):
    i = pl.program_id(0)
    n_steps = pl.num_programs(0)
    tm = x1_ref.shape[0]
    slot = i % 2

    def gather(step, slot_):
        n_tokens = dest_ref.shape[0] // top_k
        base = step * tm
        for t in range(tm):
            for k in range(top_k):
                d = dest_ref[base + (k * n_tokens + t)]
                pltpu.make_async_copy(y_ref.at[pl.ds(d, 1)], ybuf.at[slot_, pl.ds(k * tm + t, 1)],
                                      sem.at[slot_]).start(priority=(t * top_k + k) % 2)

    @pl.when(i == 0)
    def _():
        gather(0, 0)

    for nxt in range(2):
        @pl.when((i + 1 < n_steps) & (slot == 1 - nxt))
        def _(nxt=nxt):
            gather(i + 1, nxt)

    pltpu.make_async_copy(y_ref.at[pl.ds(0, top_k * tm)], ybuf.at[slot], sem.at[slot]).wait()
    gates = gate_ref[...]
    ffn = gates[:, 0:1] * ybuf[slot, pl.ds(0, tm), :]
    for k in range(1, top_k):
        ffn = ffn + gates[:, k:k + 1] * ybuf[slot, pl.ds(k * tm, tm), :]
    o_ref[...] = _layer_norm(DEEPNORM_ALPHA * x1_ref[...] + gm_ref[0] * ffn, g_ref[...], b_ref[...])


def _combine(dest_flat, y, x1, gates, g_m, ln_g, ln_b, seq):
    t, d_model = x1.shape
    tm = 128
    kern = functools.partial(_combine_kernel, top_k=TOP_K)
    return pl.pallas_call(
        kern,
        grid_spec=pltpu.PrefetchScalarGridSpec(
            num_scalar_prefetch=1,
            grid=(t // tm,),
            in_specs=[pl.BlockSpec(memory_space=pl.ANY),
                      pl.BlockSpec((tm, d_model), lambda i, d: (i, 0)),
                      pl.BlockSpec((tm, LANES), lambda i, d: (i, 0)),
                      pl.BlockSpec((1, 1, d_model), lambda i, d: (i * tm // seq, 0, 0)),
                      pl.BlockSpec((1, d_model), lambda i, d: (0, 0)),
                      pl.BlockSpec((1, d_model), lambda i, d: (0, 0))],
            out_specs=pl.BlockSpec((tm, d_model), lambda i, d: (i, 0)),
            scratch_shapes=[pltpu.VMEM((2, TOP_K * tm, d_model), F32),
                            pltpu.SemaphoreType.DMA((2,))]),
        out_shape=jax.ShapeDtypeStruct((t, d_model), F32),
        compiler_params=_cparams(("arbitrary",)),
        name="combine_ln2",
    )(dest_flat, y, x1, gates, g_m, ln_g.reshape(1, -1), ln_b.reshape(1, -1))


def _routing_plan(counts, idx, rank, n_tokens):
    n_exp = counts.shape[0]

    def take(table, index):
        hit = index[..., None] == jnp.arange(n_exp, dtype=I32)
        return jnp.sum(jnp.where(hit, table, 0), axis=-1)

    padded = (counts + ROW_BLOCK - 1) // ROW_BLOCK * ROW_BLOCK
    pad_end = jnp.cumsum(padded)
    pad_start = pad_end - padded
    dest = (take(pad_start, idx) + rank).reshape(-1).astype(I32)

    max_rows = (n_tokens * TOP_K + n_exp * (ROW_BLOCK - 1) + ROW_BLOCK - 1) // ROW_BLOCK * ROW_BLOCK
    max_items = n_exp + max_rows // ITEM_ROWS
    items_per = (padded + ITEM_ROWS - 1) // ITEM_ROWS
    items_end = jnp.cumsum(items_per)
    total = items_end[-1]
    it = jnp.arange(max_items, dtype=I32)
    last = jnp.maximum(total - 1, 0)
    it_c = jnp.minimum(it, last)
    exp_of = jnp.sum((items_end[None, :] <= it_c[:, None]).astype(I32), axis=1)
    exp_of = jnp.minimum(exp_of, n_exp - 1)
    local = it_c - take(items_end - items_per, exp_of)
    row0 = take(pad_start, exp_of) + local * ITEM_ROWS
    rows = jnp.minimum(take(padded, exp_of) - local * ITEM_ROWS, ITEM_ROWS)
    nblk = jnp.where(it < total, rows // ROW_BLOCK, 0)
    used_blocks = (pad_end[-1:] // ROW_BLOCK).astype(I32)
    fill = ((pad_start + counts).astype(I32), pad_end.astype(I32))
    return dest, fill, (exp_of.astype(I32), row0.astype(I32), nblk.astype(I32), used_blocks), max_rows


def kernel(x, c, w_ada, b_ada, w_in, b_forget, w_out, ln1_g, ln1_b, w_router, b_router,
           w_gate_up, b_gate_up, w_down, b_down, ln2_g, ln2_b):
    nb, seq, d_model = x.shape
    n_heads = d_model // HEAD_DIM
    n_dil = n_heads // 2
    n_fox = n_heads - n_dil
    dil_width, fox_width = n_dil * HEAD_DIM, n_fox * HEAD_DIM
    t = nb * seq
    layer = 0

    mod = _ada_mod(c, w_ada[layer], b_ada[layer])
    sh_a, sc_a, g_a, sh_m, sc_m, g_m = [m.reshape(nb, 1, d_model) for m in jnp.split(mod, 6, axis=-1)]

    x2 = x.reshape(t, d_model)
    qkv_dil, qkv_fox, f_logit = _in_proj(x2, sc_a, sh_a, w_in, layer, seq, dil_width, fox_width, n_fox)

    slopes = 2.0 ** (-8.0 * jnp.arange(1, n_dil + 1, dtype=F32) / n_dil)
    o_dil = _dilated_attention(qkv_dil.reshape(nb, seq, 3 * dil_width), slopes, nb, seq, n_dil)

    fox_blk = 512
    cum = _forget_cumsum(f_logit, b_forget[layer], nb, seq).reshape(
        nb, n_fox, seq // fox_blk, 1, fox_blk)
    o_fox = _fox_attention(qkv_fox.reshape(nb, seq, 3 * fox_width), cum, nb, seq, n_fox, fox_blk)

    x1, hp, idx, gates, rank, cnt = _post_attn(
        o_dil.reshape(t, dil_width), o_fox.reshape(t, fox_width), x2, w_out[layer].astype(BF16),
        g_a, sc_m, sh_m, ln1_g[layer], ln1_b[layer], w_router[layer], b_router[layer], seq)

    counts = cnt[0, :N_EXPERTS].astype(I32)
    dest, fill, items, max_rows = _routing_plan(counts, idx[:TOP_K], rank[:TOP_K], t)
    xs = _scatter_rows(dest, *fill, items[3], hp, max_rows)
    y = _moe_experts(*items, xs, w_gate_up[layer], b_gate_up[layer], w_down[layer], b_down[layer])
    out = _combine(dest, y, x1, gates, g_m, ln2_g[layer], ln2_b[layer], seq)
    return out.reshape(nb, seq, d_model)
```

```python
import functools
import math

import jax
import jax.numpy as jnp
from jax import lax
from jax.experimental import pallas as pl
from jax.experimental.pallas import tpu as pltpu

F32 = jnp.float32
BF16 = jnp.bfloat16
I32 = jnp.int32
U32 = jnp.uint32

HEAD_DIM = 128
Q_BLOCK = 128
DIL_PATTERNS = ((128, 1), (512, 4), (2048, 16))
N_EXPERTS = 32
TOP_K = 4
SWIGLU_LIMIT = 7.0
SWIGLU_ALPHA = 1.702
LN_EPS = 1e-5
DEPTH = 1
DEEPNORM_ALPHA = (2.0 * DEPTH) ** 0.25

LOG2E = math.log2(math.e)
DIL_GROUP = 32
DEINT = 4
FOX_HEADS_PER_STEP = 8

LANES = 128
VMEM_LIMIT = 56 * 1024 * 1024

ROW_BLOCK = 128
ITEM_ROWS = 1280
FF_CHUNK = 512


def _cparams(sem):
    return pltpu.CompilerParams(dimension_semantics=sem, vmem_limit_bytes=VMEM_LIMIT)


def _dot(a, b):
    return jnp.dot(a, b, preferred_element_type=F32)


def _dot_nt(a, b):
    return lax.dot_general(a, b, (((1,), (1,)), ((), ())), preferred_element_type=F32)


def _split_bf16(x):
    hi = x.astype(BF16)
    lo = (x - hi.astype(F32)).astype(BF16)
    return hi, lo


def _split_weight(w):
    return jnp.concatenate(_split_bf16(w), axis=1)


def _dot_split(x_hi, x_lo, w_cat):
    r = _dot(x_hi, w_cat) + _dot(x_lo, w_cat)
    return r[:, :LANES] + r[:, LANES:]


def _layer_norm(y, g, b):
    mu = jnp.mean(y, axis=-1, keepdims=True)
    yc = y - mu
    var = jnp.mean(yc * yc, axis=-1, keepdims=True)
    return yc * lax.rsqrt(var + LN_EPS) * g + b


def _ada_kernel(ct_ref, w_ref, b_ref, o_ref):
    ct = ct_ref[...]
    s = ct * jax.nn.sigmoid(ct)
    d_model, nb = ct.shape
    tn = w_ref.shape[1]
    sb = [jnp.broadcast_to(s[:, b:b + 1], (d_model, LANES)) for b in range(nb)]
    for j in range(tn // LANES):
        cols = slice(j * LANES, (j + 1) * LANES)
        w = w_ref[:, cols]
        for b in range(nb):
            acc = jnp.sum(w * sb[b], axis=0, keepdims=True)
            o_ref[b:b + 1, cols] = acc + b_ref[:, cols]


def _ada_mod(c, w, b):
    nb, d_model = c.shape
    n = w.shape[1]
    tn = 1536 if n % 1536 == 0 else LANES
    return pl.pallas_call(
        _ada_kernel,
        grid=(n // tn,),
        in_specs=[pl.BlockSpec((d_model, nb), lambda j: (0, 0)),
                  pl.BlockSpec((d_model, tn), lambda j: (0, j)),
                  pl.BlockSpec((1, tn), lambda j: (0, j))],
        out_specs=pl.BlockSpec((nb, tn), lambda j: (0, j)),
        out_shape=jax.ShapeDtypeStruct((nb, n), F32),
        compiler_params=_cparams(("arbitrary",)),
        name="ada_mod",
    )(c.T, w, b.reshape(1, n))


def _inproj_kernel(x_ref, sc_ref, sh_ref, w_ref, wf_ref,
                   odil_ref, ofox_ref, of_ref, h_scr, *, n_dil_tiles, q_tiles, scale):
    j = pl.program_id(1)

    @pl.when(j == 0)
    def _():
        h = x_ref[...] * (1.0 + sc_ref[0]) + sh_ref[0]
        h_hi, h_lo = _split_bf16(h)
        h_scr[...] = h_hi
        of_ref[...] = _dot_split(h_hi, h_lo, wf_ref[...])

    is_q =(j < q_tiles) | ((j >= n_dil_tiles) & (j < n_dil_tiles + q_tiles))
    col_scale = jnp.where(is_q, scale, 1.0).astype(F32)

    @pl.when(j < n_dil_tiles)
    def _():
        odil_ref[...] = _dot(h_scr[...], w_ref[...]) * col_scale

    @pl.when(j >= n_dil_tiles)
    def _():
        ofox_ref[...] = (_dot(h_scr[...], w_ref[...]) * col_scale).astype(BF16)


def _in_proj(x2, sc, sh, w_in, layer, seq, dil_width, fox_width, n_fox_heads):
    t, d_model = x2.shape
    tm, tn = min(1024, seq), min(1024, dil_width)
    n_dil_tiles = 3 * dil_width // tn
    n_fox_tiles = 3 * fox_width // tn
    q_tiles = dil_width // tn
    assert dil_width == fox_width and dil_width % tn == 0 and seq % tm == 0
    main = 3 * dil_width + 3 * fox_width
    wf = _split_weight(jnp.pad(w_in[layer, :, main:], ((0, 0), (0, LANES - n_fox_heads))))
    kern = functools.partial(_inproj_kernel, n_dil_tiles=n_dil_tiles, q_tiles=q_tiles,
                             scale=LOG2E / math.sqrt(HEAD_DIM))
    bidx = lambda i, j: (i * tm // seq, 0, 0)
    return pl.pallas_call(
        kern,
        grid=(t // tm, n_dil_tiles + n_fox_tiles),
        in_specs=[pl.BlockSpec((tm, d_model), lambda i, j: (i, 0)),
                  pl.BlockSpec((1, 1, d_model), bidx),
                  pl.BlockSpec((1, 1, d_model), bidx),
                  pl.BlockSpec((None, d_model, tn), lambda i, j: (layer, 0, j)),
                  pl.BlockSpec((d_model, 2 * LANES), lambda i, j: (0, 0))],
        out_specs=[pl.BlockSpec((tm, tn), lambda i, j: (i, jnp.minimum(j, n_dil_tiles - 1))),
                   pl.BlockSpec((tm, tn), lambda i, j: (i, jnp.maximum(j - n_dil_tiles, 0))),
                   pl.BlockSpec((tm, LANES), lambda i, j: (i, 0))],
        out_shape=[jax.ShapeDtypeStruct((t, 3 * dil_width), F32),
                   jax.ShapeDtypeStruct((t, 3 * fox_width), BF16),
                   jax.ShapeDtypeStruct((t, LANES), F32)],
        scratch_shapes=[pltpu.VMEM((tm, d_model), BF16)],
        compiler_params=_cparams(("arbitrary", "arbitrary")),
        name="in_proj",
    )(x2, sc, sh, w_in.astype(BF16), wf)


def _cum_kernel(f_ref, b_ref, o_ref):
    hf = o_ref.shape[1]
    z = f_ref[...].T[:hf, :] + b_ref[...]
    c = jnp.minimum(z, 0.0) - jnp.log1p(jnp.exp(-jnp.abs(z)))
    seq = z.shape[1]
    lane = lax.broadcasted_iota(I32, z.shape, 1)
    shift = 1
    while shift < seq:
        c = c + jnp.where(lane >= shift, pltpu.roll(c, shift, axis=1), 0.0)
        shift *= 2
    o_ref[0] = c * LOG2E


def _forget_cumsum(f_logit, b_forget, nb, seq):
    hf = b_forget.shape[0]
    return pl.pallas_call(
        _cum_kernel,
        grid=(nb,),
        in_specs=[pl.BlockSpec((seq, LANES), lambda b: (b, 0)),
                  pl.BlockSpec((hf, 1), lambda b: (0, 0))],
        out_specs=pl.BlockSpec((1, hf, seq), lambda b: (b, 0, 0)),
        out_shape=jax.ShapeDtypeStruct((nb, hf, seq), F32),
        compiler_params=_cparams(("arbitrary",)),
        name="forget_cumsum",
    )(f_logit, b_forget.reshape(hf, 1))


def _dil_kernel(slope_ref, q_ref, k_ref, v_ref, o_ref, q4_ref, k4_ref, v4_ref, acc_ref, l_ref, m_ref,
                bias_ref, *, patterns, seq):
    slope = slope_ref[pl.program_id(1)]
    qb = Q_BLOCK
    row = lax.broadcasted_iota(I32, (qb, 2 * qb), 0)
    col = lax.broadcasted_iota(I32, (qb, 2 * qb), 1)
    delta = row + qb - col
    is_prev = col < qb
    neg_inf = F32(-jnp.inf)
    ones = jnp.ones((2 * qb, HEAD_DIM), BF16)

    sub = seq // DEINT
    chunk = min(256, sub)
    for r4 in range(DEINT):
        for c0 in range(0, sub, chunk):
            src = pl.ds(r4 + DEINT * c0, chunk, stride=DEINT)
            dst = pl.ds(r4 * sub + c0, chunk)
            q4_ref[dst, :] = q_ref[0, src, :]
            k4_ref[dst, :] = k_ref[0, src, :]
            v4_ref[dst, :] = v_ref[0, src, :]

    for pi, (window, dil) in enumerate(sorted(patterns, key=lambda wd: -wd[1])):
        reach = window // dil
        nblk = seq // dil // qb
        n_blocks = dil * nblk
        group = min(DIL_GROUP, n_blocks)
        assert reach <= qb and nblk * qb * dil == seq and n_blocks % group == 0
        sd = slope * (float(dil) * LOG2E)
        bias_ref[...] = jnp.where((delta >= 0) & (delta <= reach), -sd * delta.astype(F32), neg_inf)

        def block(idx, dil=dil, nblk=nblk, pi=pi):
            r = idx // nblk
            n = idx - r * nblk
            start = r + n * (qb * dil)
            cur = pl.ds(start, qb, stride=dil) if dil > 1 else pl.ds(pl.multiple_of(start, qb), qb)
            if dil % DEINT == 0:
                step = dil // DEINT
                first = (r % DEINT) * sub + r // DEINT
                lstart = first + n * (qb * step)
                lprev = jnp.maximum(lstart - qb * step, first)
                if step > 1:
                    src_cur, src_prev = pl.ds(lstart, qb, stride=step), pl.ds(lprev, qb, stride=step)
                else:
                    src_cur = pl.ds(pl.multiple_of(lstart, qb), qb)
                    src_prev = pl.ds(pl.multiple_of(lprev, qb), qb)
                q = q4_ref[src_cur, :].astype(BF16)
                kk = jnp.concatenate([k4_ref[src_prev, :], k4_ref[src_cur, :]], axis=0).astype(BF16)
                vv = jnp.concatenate([v4_ref[src_prev, :], v4_ref[src_cur, :]], axis=0).astype(BF16)
            else:
                pstart = jnp.maximum(start - qb * dil, r)
                prev = (pl.ds(pstart, qb, stride=dil) if dil > 1
                        else pl.ds(pl.multiple_of(pstart, qb), qb))
                q = q_ref[0, cur, :].astype(BF16)
                kk = jnp.concatenate([k_ref[0, prev, :], k_ref[0, cur, :]], axis=0).astype(BF16)
                vv = jnp.concatenate([v_ref[0, prev, :], v_ref[0, cur, :]], axis=0).astype(BF16)
            s = _dot_nt(q, kk) + bias_ref[...]
            s = jnp.where(is_prev & (n == 0), neg_inf, s)
            m_blk = jnp.max(s, axis=1, keepdims=True)
            p = jnp.exp2(s - m_blk)
            pv = _dot(p.astype(BF16), jnp.concatenate([vv, ones], axis=1))
            o_blk, l_blk = pv[:, :HEAD_DIM], pv[:, HEAD_DIM:]
            if pi == 0:
                m_ref[cur, :] = jnp.broadcast_to(m_blk, (qb, LANES))
                acc_ref[cur, :] = o_blk
                l_ref[cur, :] = l_blk
            else:
                m_old = m_ref[cur, :]
                m_new = jnp.maximum(m_old, m_blk)
                a_old = jnp.exp2(m_old - m_new)
                a_blk = jnp.exp2(m_blk - m_new)
                m_ref[cur, :] = m_new
                acc_ref[cur, :] = acc_ref[cur, :] * a_old + o_blk * a_blk
                l_ref[cur, :] = l_ref[cur, :] * a_old + l_blk * a_blk

        def body(g, carry, block=block, group=group):
            for u in range(group):
                block(g * group + u)
            return carry

        lax.fori_loop(0, n_blocks // group, body, 0)

    chunk = 512
    for c0 in range(0, seq, chunk):
        rows = slice(c0, c0 + chunk)
        o_ref[0, rows, :] = (acc_ref[rows, :] / l_ref[rows, :]).astype(o_ref.dtype)


def _dilated_attention(qkv, slopes, nb, seq, n_heads):
    dh = HEAD_DIM
    kern = functools.partial(_dil_kernel, patterns=DIL_PATTERNS, seq=seq)
    blk = lambda off: pl.BlockSpec((1, seq, dh), lambda b, h, s: (b, 0, off + h))
    return pl.pallas_call(
        kern,
        grid_spec=pltpu.PrefetchScalarGridSpec(
            num_scalar_prefetch=1,
            grid=(nb, n_heads),
            in_specs=[blk(0), blk(n_heads), blk(2 * n_heads)],
            out_specs=pl.BlockSpec((1, seq, dh), lambda b, h, s: (b, 0, h)),
            scratch_shapes=[pltpu.VMEM((seq, dh), F32),
                            pltpu.VMEM((seq, dh), F32),
                            pltpu.VMEM((seq, dh), F32),
                            pltpu.VMEM((seq, dh), F32),
                            pltpu.VMEM((seq, LANES), F32),
                            pltpu.VMEM((seq, LANES), F32),
                            pltpu.VMEM((Q_BLOCK, 2 * Q_BLOCK), F32)]),
        out_shape=jax.ShapeDtypeStruct((nb, seq, n_heads * dh), BF16),
        compiler_params=_cparams(("arbitrary", "arbitrary")),
        name="dilated_attn",
    )(slopes, qkv, qkv, qkv)


def _fox_kernel(q_ref, k_ref, v_ref, c_ref, o_ref, m_scr, acc_scr, *, blk, hb):
    i = pl.program_id(2)
    dh = HEAD_DIM
    row = lax.broadcasted_iota(I32, (blk, blk), 0)
    col = lax.broadcasted_iota(I32, (blk, blk), 1)
    m_scr[...] = jnp.full(m_scr.shape, -jnp.inf, F32)
    acc_scr[...] = jnp.zeros(acc_scr.shape, F32)
    ones = jnp.ones((blk, dh), BF16)
    n_slab = blk // LANES

    def step(j, masked):
        rows = pl.ds(pl.multiple_of(j * blk, blk), blk)
        for h in range(hb):
            cols = slice(h * dh, (h + 1) * dh)
            s = _dot_nt(q_ref[0, :, cols], k_ref[0, rows, cols]) - c_ref[0, h, j]
            if masked:
                s = jnp.where(col <= row, s, -jnp.inf)
            m_old = m_scr[h]
            m_new = jnp.maximum(m_old, jnp.max(s, axis=1, keepdims=True))
            alpha = jnp.exp2(m_old - m_new)
            p = jnp.concatenate(
                [jnp.exp2(s[:, t * LANES:(t + 1) * LANES] - m_new) for t in range(n_slab)], axis=1)
            pv = _dot(p.astype(BF16), jnp.concatenate([v_ref[0, rows, cols], ones], axis=1))
            acc_scr[h] = jnp.concatenate([alpha, alpha], axis=1) * acc_scr[h] + pv
            m_scr[h] = m_new

    def body(j, carry):
        step(j, False)
        return carry

    lax.fori_loop(0, i, body, 0)
    step(i, True)
    for h in range(hb):
        acc = acc_scr[h]
        o_ref[0, :, h * dh:(h + 1) * dh] = (acc[:, :dh] / acc[:, dh:]).astype(o_ref.dtype)


def _fox_attention(qkv, cum, nb, seq, n_heads, blk):
    dh = HEAD_DIM
    hb = min(FOX_HEADS_PER_STEP, n_heads)
    assert n_heads % hb == 0
    ng = n_heads // hb
    w = hb * dh
    kern = functools.partial(_fox_kernel, blk=blk, hb=hb)
    return pl.pallas_call(
        kern,
        grid=(nb, ng, seq // blk),
        in_specs=[pl.BlockSpec((1, blk, w), lambda b, g, i: (b, i, g)),
                  pl.BlockSpec((1, seq, w), lambda b, g, i: (b, 0, ng + g)),
                  pl.BlockSpec((1, seq, w), lambda b, g, i: (b, 0, 2 * ng + g)),
                  pl.BlockSpec((1, hb, seq // blk, 1, blk), lambda b, g, i: (b, g, 0, 0, 0))],
        out_specs=pl.BlockSpec((1, blk, w), lambda b, g, i: (b, i, g)),
        out_shape=jax.ShapeDtypeStruct((nb, seq, n_heads * dh), BF16),
        scratch_shapes=[pltpu.VMEM((hb, blk, LANES), F32),
                        pltpu.VMEM((hb, blk, 2 * dh), F32)],
        compiler_params=_cparams(("arbitrary", "arbitrary", "arbitrary")),
        name="fox_attn",
    )(qkv, qkv, qkv, cum)


def _post_attn_kernel(od_ref, of_ref, x_ref, w_ref, ga_ref, scm_ref, shm_ref, g_ref, b_ref,
                      wr_ref, br_ref,
                      x1_ref, hp_ref, idx_ref, gate_ref, rank_ref, cnt_ref, carry_scr,
                      *, n_experts, top_k, dil_width, sub):
    i = pl.program_id(0)

    @pl.when(i == 0)
    def _():
        carry_scr[...] = jnp.zeros_like(carry_scr)

    for r0 in range(0, x_ref.shape[0], sub):
        _post_attn_rows(slice(r0, r0 + sub), od_ref, of_ref, x_ref, w_ref, ga_ref, scm_ref, shm_ref,
                        g_ref, b_ref, wr_ref, br_ref, x1_ref, hp_ref, idx_ref, gate_ref, rank_ref,
                        carry_scr, n_experts=n_experts, top_k=top_k, dil_width=dil_width)
    cnt_ref[...] = jnp.broadcast_to(carry_scr[...], cnt_ref.shape)


def _post_attn_rows(rows, od_ref, of_ref, x_ref, w_ref, ga_ref, scm_ref, shm_ref, g_ref, b_ref,
                    wr_ref, br_ref, x1_ref, hp_ref, idx_ref, gate_ref, rank_ref, carry_scr,
                    *, n_experts, top_k, dil_width):
    attn = (_dot(od_ref[rows, :], w_ref[:dil_width, :]) + _dot(of_ref[rows, :], w_ref[dil_width:, :]))
    x1 = _layer_norm(DEEPNORM_ALPHA * x_ref[rows, :] + ga_ref[0] * attn, g_ref[...], b_ref[...])
    x1_ref[rows, :] = x1
    h = x1 * (1.0 + scm_ref[0]) + shm_ref[0]
    h_hi, h_lo = _split_bf16(h)
    hp_ref[rows, :] = h

    logits = _dot_split(h_hi, h_lo, wr_ref[...]) + br_ref[...]
    tm = logits.shape[0]
    lane = lax.broadcasted_iota(I32, (tm, LANES), 1)
    work = jnp.where(lane < n_experts, logits, -jnp.inf)
    vals, idxs = [], []
    for _ in range(top_k):
        mx = jnp.max(work, axis=1, keepdims=True)
        ix = jnp.min(jnp.where(work == mx, lane, LANES), axis=1, keepdims=True)
        vals.append(mx)
        idxs.append(ix)
        work = jnp.where(lane == ix, -jnp.inf, work)
    exps = [jnp.exp(v - vals[0]) for v in vals]
    denom = exps[0]
    for e in exps[1:]:
        denom = denom + e

    onehot = jnp.zeros((tm, LANES), F32)
    for ix in idxs:
        onehot = onehot + (lane == ix).astype(F32)
    r_i = lax.broadcasted_iota(I32, (tm, tm), 0)
    c_i = lax.broadcasted_iota(I32, (tm, tm), 1)
    tri = (c_i < r_i).astype(BF16)
    before = _dot(tri, onehot.astype(BF16)) + carry_scr[...]

    idx_out = jnp.zeros((tm, LANES), I32)
    gate_out = jnp.zeros((tm, LANES), F32)
    rank_out = jnp.zeros((tm, LANES), I32)
    for k in range(top_k):
        rank_k = jnp.sum(jnp.where(lane == idxs[k], before, 0.0), axis=1, keepdims=True)
        idx_out = jnp.where(lane == k, idxs[k], idx_out)
        gate_out = jnp.where(lane == k, exps[k] / denom, gate_out)
        rank_out = jnp.where(lane == k, rank_k.astype(I32), rank_out)
    gate_ref[rows, :] = gate_out
    idx_ref[:, rows] = idx_out.T[:idx_ref.shape[0], :]
    rank_ref[:, rows] = rank_out.T[:rank_ref.shape[0], :]
    carry_scr[...] = carry_scr[...] + jnp.sum(onehot, axis=0, keepdims=True)


def _post_attn(o_dil, o_fox, x2, w_out_bf, g_a, sc_m, sh_m, ln_g, ln_b, w_router, b_router, seq):
    t, d_model = x2.shape
    dil_width = o_dil.shape[1]
    n_experts = w_router.shape[1]
    tm, sub = 512, 512
    wr = _split_weight(jnp.pad(w_router, ((0, 0), (0, LANES - n_experts))))
    br = jnp.pad(b_router, (0, LANES - n_experts)).reshape(1, LANES)
    kern = functools.partial(_post_attn_kernel, n_experts=n_experts, top_k=TOP_K, dil_width=dil_width,
                             sub=sub)
    row = lambda w: pl.BlockSpec((tm, w), lambda i: (i, 0))
    col = pl.BlockSpec((8, tm), lambda i: (0, i))
    const = lambda r, w: pl.BlockSpec((r, w), lambda i: (0, 0))
    bvec = pl.BlockSpec((1, 1, d_model), lambda i: (i * tm // seq, 0, 0))
    return pl.pallas_call(
        kern,
        grid=(t // tm,),
        in_specs=[row(dil_width), row(o_fox.shape[1]), row(d_model), const(d_model, d_model),
                  bvec, bvec, bvec, const(1, d_model), const(1, d_model),
                  const(d_model, 2 * LANES), const(1, LANES)],
        out_specs=[row(d_model), row(d_model), col, row(LANES), col,
                   pl.BlockSpec((8, LANES), lambda i: (0, 0))],
        out_shape=[jax.ShapeDtypeStruct((t, d_model), F32),
                   jax.ShapeDtypeStruct((t, d_model), F32),
                   jax.ShapeDtypeStruct((8, t), I32),
                   jax.ShapeDtypeStruct((t, LANES), F32),
                   jax.ShapeDtypeStruct((8, t), I32),
                   jax.ShapeDtypeStruct((8, LANES), F32)],
        scratch_shapes=[pltpu.VMEM((1, LANES), F32)],
        compiler_params=_cparams(("arbitrary",)),
        name="post_attn",
    )(o_dil, o_fox, x2, w_out_bf, g_a, sc_m, sh_m, ln_g.reshape(1, -1), ln_b.reshape(1, -1),
      wr, br)


def _scatter_kernel(dest_ref, fill_lo_ref, fill_hi_ref, used_ref, hp_ref, xs_ref, zbuf, sem, zsem,
                    *, top_k):
    i = pl.program_id(0)
    tm = hp_ref.shape[0]
    rb = ROW_BLOCK

    @pl.when(i == 0)
    def _():
        zbuf[...] = jnp.zeros(zbuf.shape, zbuf.dtype)

        def tails(wait):
            def row(r, c):
                copy = pltpu.make_async_copy(zbuf.at[pl.ds(0, 1)], xs_ref.at[pl.ds(r, 1)], zsem.at[0])
                copy.wait() if wait else copy.start()
                return c

            def expert(e, c):
                return lax.fori_loop(fill_lo_ref[e], fill_hi_ref[e], row, c)

            lax.fori_loop(0, fill_lo_ref.shape[0], expert, 0)

        def rest(wait):
            def blk(b, c):
                copy = pltpu.make_async_copy(zbuf, xs_ref.at[pl.ds(pl.multiple_of(b * rb, rb), rb)],
                                             zsem.at[1])
                copy.wait() if wait else copy.start()
                return c

            lax.fori_loop(used_ref[0], xs_ref.shape[0] // rb, blk, 0)

        tails(False)
        rest(False)
        tails(True)
        rest(True)

    n_tokens = dest_ref.shape[0] // top_k
    base = i * tm
    for t in range(tm):
        for k in range(top_k):
            d = dest_ref[base + (k * n_tokens + t)]
            pltpu.make_async_copy(hp_ref.at[pl.ds(t, 1)], xs_ref.at[pl.ds(d, 1)], sem).start(
                priority=(t * top_k + k) % 2)

    for k in range(top_k):
        pltpu.make_async_copy(hp_ref, xs_ref.at[pl.ds(0, tm)], sem).wait()


def _scatter_rows(dest_flat, fill_lo, fill_hi, used_blocks, hp, n_rows):
    t, w = hp.shape
    tm = 512
    kern = functools.partial(_scatter_kernel, top_k=TOP_K)
    return pl.pallas_call(
        kern,
        grid_spec=pltpu.PrefetchScalarGridSpec(
            num_scalar_prefetch=4,
            grid=(t // tm,),
            in_specs=[pl.BlockSpec((tm, w), lambda i, *_: (i, 0))],
            out_specs=pl.BlockSpec(memory_space=pl.ANY),
            scratch_shapes=[pltpu.VMEM((ROW_BLOCK, w), F32),
                            pltpu.SemaphoreType.DMA(()),
                            pltpu.SemaphoreType.DMA((2,))]),
        out_shape=jax.ShapeDtypeStruct((n_rows, w), F32),
        compiler_params=_cparams(("arbitrary",)),
        name="scatter_rows",
    )(dest_flat, fill_lo, fill_hi, used_blocks, hp)


def _moe_kernel(iexp_ref, irow_ref, inu_ref, used_ref, xs_ref, wg_ref, wu_ref, wd_ref, bg_ref, bu_ref,
                bd_ref, y_ref, xbuf, yacc, sem_in, sem_out, *, n_chunks):
    del iexp_ref
    it = pl.program_id(0)
    f = pl.program_id(1)
    n_items = pl.num_programs(0)
    nu = inu_ref[it]
    rb = ROW_BLOCK
    rows_of = lambda k, n=1: pl.ds(pl.multiple_of(k * rb, rb), n * rb)

    nxt = jnp.minimum(it + 1, n_items - 1)
    n_next = jnp.where(it + 1 < n_items, inu_ref[nxt], 0)
    n_prev = jnp.where(it > 0, inu_ref[jnp.maximum(it - 1, 0)], 0)

    def x_copy(item, k, src_unit=None):
        src_unit = k if src_unit is None else src_unit
        src = pl.ds(pl.multiple_of(irow_ref[item] + src_unit * rb, rb), rb)
        return pltpu.make_async_copy(xs_ref.at[src], xbuf.at[rows_of(k)], sem_in)

    def y_copy(k):
        dst = pl.ds(pl.multiple_of(irow_ref[it] + k * rb, rb), rb)
        return pltpu.make_async_copy(yacc.at[rows_of(k)], y_ref.at[dst], sem_out)

    def for_units(n, fn):
        def body(k, c):
            fn(k)
            return c
        lax.fori_loop(0, n, body, 0)

    def init_unit(k):
        yacc[rows_of(k), :] = jnp.broadcast_to(bd_ref[0], (rb, yacc.shape[1]))

    first = f == 0
    last = f == n_chunks - 1

    n_arriving = jnp.where(it == 0, nu, jnp.where(n_prev > 0, jnp.maximum(n_prev, nu), 0))

    def wait_rows():
        @pl.when(first)
        def _():
            for_units(n_arriving, lambda k: x_copy(it, 0).wait())

    @pl.when(nu == 0)
    def _():
        wait_rows()

    @pl.when(nu > 0)
    def _():
        @pl.when(first & (it == 0))
        def _():
            for_units(nu, lambda k: x_copy(0, k).start())

        @pl.when(first)
        def _():
            for_units(nu, init_unit)

        wg = wg_ref[0].astype(BF16)
        wu = wu_ref[0].astype(BF16)
        wd = wd_ref[0].astype(BF16)
        bg = bg_ref[0]
        bu = bu_ref[0]

        wait_rows()

        def block(k, n, stream):
            xb = xbuf[rows_of(k, n), :].astype(BF16)
            g = _dot(xb, wg) + bg
            u = _dot(xb, wu) + bu
            g = jnp.minimum(g, SWIGLU_LIMIT)
            u = jnp.clip(u, -SWIGLU_LIMIT, SWIGLU_LIMIT)
            act = (u + 1.0) * (g * jax.nn.sigmoid(SWIGLU_ALPHA * g))
            yacc[rows_of(k, n), :] += _dot(act.astype(BF16), wd)
            if stream:
                for j in range(n):
                    y_copy(k + j).start()
                    x_copy(nxt, k + j, jnp.minimum(k + j, jnp.maximum(n_next, 1) - 1)).start()

        def run_blocks(stream):
            def quad(t, c):
                block(4 * t, 2, stream)
                block(4 * t + 2, 2, stream)
                return c

            n_quads = lax.shift_right_logical(nu, 2)
            lax.fori_loop(0, n_quads, quad, 0)
            done = n_quads * 4

            @pl.when((nu & 2) != 0)
            def _():
                block(done, 2, stream)

            @pl.when((nu & 1) != 0)
            def _():
                block(done + (nu & 2), 1, stream)

        @pl.when(jnp.logical_not(last))
        def _():
            run_blocks(False)

        @pl.when(last)
        def _():
            run_blocks(True)

            def rest(k, c):
                x_copy(nxt, k).start()
                return c
            lax.fori_loop(nu, n_next, rest, 0)
            for_units(nu, lambda k: y_copy(k).wait())

            @pl.when(it == n_items - 1)
            def _():
                for_units(nu, lambda k: x_copy(it, 0).wait())

    @pl.when((it == n_items - 1) & last)
    def _():
        lo = used_ref[0]
        hi = y_ref.shape[0] // rb
        dst_of = lambda b: pl.ds(pl.multiple_of(b * rb, rb), rb)
        yacc[pl.ds(0, rb), :] = jnp.zeros((rb, yacc.shape[1]), yacc.dtype)

        def start(b, c):
            pltpu.make_async_copy(yacc.at[pl.ds(0, rb)], y_ref.at[dst_of(b)], sem_out).start()
            return c

        def wait(b, c):
            pltpu.make_async_copy(yacc.at[pl.ds(0, rb)], y_ref.at[dst_of(b)], sem_out).wait()
            return c

        lax.fori_loop(lo, hi, start, 0)
        lax.fori_loop(lo, hi, wait, 0)


def _moe_experts(item_exp, item_row, item_nblk, used_blocks, xs, w_gate_up, b_gate_up, w_down, b_down):
    n_rows = xs.shape[0]
    n_exp, d_model, two_f = w_gate_up.shape
    d_ff = two_f // 2
    fc = FF_CHUNK
    n_chunks = d_ff // fc
    n_items = item_exp.shape[0]
    kern = functools.partial(_moe_kernel, n_chunks=n_chunks)

    def fsel(it, f, nbk):
        return jnp.where(nbk[it] > 0, f, n_chunks - 1)

    return pl.pallas_call(
        kern,
        grid_spec=pltpu.PrefetchScalarGridSpec(
            num_scalar_prefetch=4,
            grid=(n_items, n_chunks),
            in_specs=[
                pl.BlockSpec(memory_space=pl.ANY),
                pl.BlockSpec((1, d_model, fc), lambda it, f, ex, ro, nbk, us: (ex[it], 0, fsel(it, f, nbk))),
                pl.BlockSpec((1, d_model, fc),
                             lambda it, f, ex, ro, nbk, us: (ex[it], 0, n_chunks + fsel(it, f, nbk))),
                pl.BlockSpec((1, fc, d_model), lambda it, f, ex, ro, nbk, us: (ex[it], fsel(it, f, nbk), 0)),
                pl.BlockSpec((1, 1, fc), lambda it, f, ex, ro, nbk, us: (ex[it], 0, fsel(it, f, nbk))),
                pl.BlockSpec((1, 1, fc),
                             lambda it, f, ex, ro, nbk, us: (ex[it], 0, n_chunks + fsel(it, f, nbk))),
                pl.BlockSpec((1, 1, d_model), lambda it, f, ex, ro, nbk, us: (ex[it], 0, 0)),
            ],
            out_specs=pl.BlockSpec(memory_space=pl.ANY),
            scratch_shapes=[pltpu.VMEM((ITEM_ROWS, d_model), F32),
                            pltpu.VMEM((ITEM_ROWS, d_model), F32),
                            pltpu.SemaphoreType.DMA(()),
                            pltpu.SemaphoreType.DMA(())]),
        out_shape=jax.ShapeDtypeStruct((n_rows, d_model), F32),
        compiler_params=_cparams(("arbitrary", "arbitrary")),
        name="moe_experts",
    )(item_exp, item_row, item_nblk, used_blocks, xs, w_gate_up, w_gate_up, w_down,
      b_gate_up.reshape(n_exp, 1, two_f), b_gate_up.reshape(n_exp, 1, two_f),
      b_down.reshape(n_exp, 1, d_model))


def _combine_kernel(dest_ref, y_ref, x1_ref, gate_ref, gm_ref, g_ref, b_ref, o_ref, ybuf, sem, *, top_k):
    i = pl.program_id(0)
    n_steps = pl.num_programs(0)
    tm = x1_ref.shape[0]
    slot = i % 2

    def gather(step, slot_):
        n_tokens = dest_ref.shape[0] // top_k
        base = step * tm
        for t in range(tm):
            for k in range(top_k):
                d = dest_ref[base + (k * n_tokens + t)]
                pltpu.make_async_copy(y_ref.at[pl.ds(d, 1)], ybuf.at[slot_, pl.ds(k * tm + t, 1)],
                                      sem.at[slot_]).start(priority=(t * top_k + k) % 2)

    @pl.when(i == 0)
    def _():
        gather(0, 0)

    for nxt in range(2):
        @pl.when((i + 1 < n_steps) & (slot == 1 - nxt))
        def _(nxt=nxt):
            gather(i + 1, nxt)

    pltpu.make_async_copy(y_ref.at[pl.ds(0, top_k * tm)], ybuf.at[slot], sem.at[slot]).wait()
    gates = gate_ref[...]
    ffn = gates[:, 0:1] * ybuf[slot, pl.ds(0, tm), :]
    for k in range(1, top_k):
        ffn = ffn + gates[:, k:k + 1] * ybuf[slot, pl.ds(k * tm, tm), :]
    o_ref[...] = _layer_norm(DEEPNORM_ALPHA * x1_ref[...] + gm_ref[0] * ffn, g_ref[...], b_ref[...])


def _combine(dest_flat, y, x1, gates, g_m, ln_g, ln_b, seq):
    t, d_model = x1.shape
    tm = 128
    kern = functools.partial(_combine_kernel, top_k=TOP_K)
    return pl.pallas_call(
        kern,
        grid_spec=pltpu.PrefetchScalarGridSpec(
            num_scalar_prefetch=1,
            grid=(t // tm,),
            in_specs=[pl.BlockSpec(memory_space=pl.ANY),
                      pl.BlockSpec((tm, d_model), lambda i, d: (i, 0)),
                      pl.BlockSpec((tm, LANES), lambda i, d: (i, 0)),
                      pl.BlockSpec((1, 1, d_model), lambda i, d: (i * tm // seq, 0, 0)),
                      pl.BlockSpec((1, d_model), lambda i, d: (0, 0)),
                      pl.BlockSpec((1, d_model), lambda i, d: (0, 0))],
            out_specs=pl.BlockSpec((tm, d_model), lambda i, d: (i, 0)),
            scratch_shapes=[pltpu.VMEM((2, TOP_K * tm, d_model), F32),
                            pltpu.SemaphoreType.DMA((2,))]),
        out_shape=jax.ShapeDtypeStruct((t, d_model), F32),
        compiler_params=_cparams(("arbitrary",)),
        name="combine_ln2",
    )(dest_flat, y, x1, gates, g_m, ln_g.reshape(1, -1), ln_b.reshape(1, -1))


def _routing_plan(counts, idx, rank, n_tokens):
    n_exp = counts.shape[0]

    def take(table, index):
        hit = index[..., None] == jnp.arange(n_exp, dtype=I32)
        return jnp.sum(jnp.where(hit, table, 0), axis=-1)

    padded = (counts + ROW_BLOCK - 1) // ROW_BLOCK * ROW_BLOCK
    pad_end = jnp.cumsum(padded)
    pad_start = pad_end - padded
    dest = (take(pad_start, idx) + rank).reshape(-1).astype(I32)

    max_rows = (n_tokens * TOP_K + n_exp * (ROW_BLOCK - 1) + ROW_BLOCK - 1) // ROW_BLOCK * ROW_BLOCK
    max_items = n_exp + max_rows // ITEM_ROWS
    items_per = (padded + ITEM_ROWS - 1) // ITEM_ROWS
    items_end = jnp.cumsum(items_per)
    total = items_end[-1]
    it = jnp.arange(max_items, dtype=I32)
    last = jnp.maximum(total - 1, 0)
    it_c = jnp.minimum(it, last)
    exp_of = jnp.sum((items_end[None, :] <= it_c[:, None]).astype(I32), axis=1)
    exp_of = jnp.minimum(exp_of, n_exp - 1)
    local = it_c - take(items_end - items_per, exp_of)
    row0 = take(pad_start, exp_of) + local * ITEM_ROWS
    rows = jnp.minimum(take(padded, exp_of) - local * ITEM_ROWS, ITEM_ROWS)
    nblk = jnp.where(it < total, rows // ROW_BLOCK, 0)
    used_blocks = (pad_end[-1:] // ROW_BLOCK).astype(I32)
    fill = ((pad_start + counts).astype(I32), pad_end.astype(I32))
    return dest, fill, (exp_of.astype(I32), row0.astype(I32), nblk.astype(I32), used_blocks), max_rows


def kernel(x, c, w_ada, b_ada, w_in, b_forget, w_out, ln1_g, ln1_b, w_router, b_router,
           w_gate_up, b_gate_up, w_down, b_down, ln2_g, ln2_b):
    nb, seq, d_model = x.shape
    n_heads = d_model // HEAD_DIM
    n_dil = n_heads // 2
    n_fox = n_heads - n_dil
    dil_width, fox_width = n_dil * HEAD_DIM, n_fox * HEAD_DIM
    t = nb * seq
    layer = 0

    mod = _ada_mod(c, w_ada[layer], b_ada[layer])
    sh_a, sc_a, g_a, sh_m, sc_m, g_m = [m.reshape(nb, 1, d_model) for m in jnp.split(mod, 6, axis=-1)]

    x2 = x.reshape(t, d_model)
    qkv_dil, qkv_fox, f_logit = _in_proj(x2, sc_a, sh_a, w_in, layer, seq, dil_width, fox_width, n_fox)

    slopes = 2.0 ** (-8.0 * jnp.arange(1, n_dil + 1, dtype=F32) / n_dil)
    o_dil = _dilated_attention(qkv_dil.reshape(nb, seq, 3 * dil_width), slopes, nb, seq, n_dil)

    fox_blk = 512
    cum = _forget_cumsum(f_logit, b_forget[layer], nb, seq).reshape(
        nb, n_fox, seq // fox_blk, 1, fox_blk)
    o_fox = _fox_attention(qkv_fox.reshape(nb, seq, 3 * fox_width), cum, nb, seq, n_fox, fox_blk)

    x1, hp, idx, gates, rank, cnt = _post_attn(
        o_dil.reshape(t, dil_width), o_fox.reshape(t, fox_width), x2, w_out[layer].astype(BF16),
        g_a, sc_m, sh_m, ln1_g[layer], ln1_b[layer], w_router[layer], b_router[layer], seq)

    counts = cnt[0, :N_EXPERTS].astype(I32)
    dest, fill, items, max_rows = _routing_plan(counts, idx[:TOP_K], rank[:TOP_K], t)
    xs = _scatter_rows(dest, *fill, items[3], hp, max_rows)
    y = _moe_experts(*items, xs, w_gate_up[layer], b_gate_up[layer], w_down[layer], b_down[layer])
    out = _combine(dest, y, x1, gates, g_m, ln2_g[layer], ln2_b[layer], seq)
    return out.reshape(nb, seq, d_model)
```
